```python
import math
import jax, jax.numpy as jnp
from jax import lax
import numpy as np

D_MODEL = 2048
BATCH = 8
SEQ = 2048
DEPTH = 2
DEC_BATCH = 128
DEC_SEQ = 1
PAST_LEN = 8192
PAGE_SIZE = 128

N_EVEN = (DEPTH + 1) // 2
N_ODD = DEPTH // 2
N_HEADS = 16
N_KV_HEADS = 4
HEAD_DIM = 64
Q_PER_KV = N_HEADS // N_KV_HEADS
WINDOW = 128
ATTN_WIDTH = N_HEADS * HEAD_DIM
KV_WIDTH = N_KV_HEADS * HEAD_DIM
ATTN_SCALE = HEAD_DIM ** -0.5
S5_WIDTH = D_MODEL // 2
S5_GROUP_CH = 16
S5_GROUPS = S5_WIDTH // S5_GROUP_CH
S5_STATE = 64
EVEN_IN = ATTN_WIDTH + 2 * KV_WIDTH + S5_WIDTH
EVEN_MIX = ATTN_WIDTH + S5_WIDTH
DN_K_HEADS = 16
DN_V_HEADS = 32
DN_K_DIM = 128
DN_V_DIM = 128
DN_CONV_K = 4
DN_CHUNK = 64
DN_QK_W = DN_K_HEADS * DN_K_DIM
DN_V_W = DN_V_HEADS * DN_V_DIM
DN_CONV_DIM = 2 * DN_QK_W + DN_V_W
ODD_IN = DN_CONV_DIM + DN_V_W + 2 * DN_V_HEADS
D_FF = 5632
N_EXPERTS = 8
TOP_K = 2
EPS = 1e-6
NEG_INF = -1e30

kernel_name = 'hybrid_swa_s5_gdn_moe_step'


def rms_norm(x, g):
    xf = x.astype(jnp.float32)
    y = xf * lax.rsqrt(jnp.mean(xf * xf, -1, keepdims=True) + EPS)
    return (y * g.astype(jnp.float32)).astype(x.dtype)


def l2norm(x):
    return x * lax.rsqrt(jnp.sum(x * x, -1, keepdims=True) + EPS)


def swiglu(x, w_gate, w_up, w_down):
    return (jax.nn.silu(x @ w_gate) * (x @ w_up)) @ w_down


def sink_attend(s, sinks, v, eq):
    sk = sinks.astype(jnp.float32).reshape(N_KV_HEADS, Q_PER_KV, 1, 1)
    m = jnp.maximum(jnp.max(s, -1, keepdims=True), sk)
    p = jnp.exp(s - m)
    p = p / (jnp.sum(p, -1, keepdims=True) + jnp.exp(sk - m))
    return jnp.einsum(eq, p.astype(v.dtype), v)


def swa_prompt(q, k, v, sinks):
    B_, L = q.shape[:2]
    nb = L // WINDOW
    qb = q.reshape(B_, nb, WINDOW, N_KV_HEADS, Q_PER_KV, HEAD_DIM)

    def band(t):
        t = t.reshape(B_, nb, WINDOW, N_KV_HEADS, HEAD_DIM)
        prev = jnp.concatenate([jnp.zeros_like(t[:, :1]), t[:, :-1]], 1)
        return jnp.concatenate([prev, t], 2)

    s = jnp.einsum('bnqkgd,bnskd->bnkgqs', qb, band(k), preferred_element_type=jnp.float32) * ATTN_SCALE
    rel = (WINDOW + jnp.arange(WINDOW))[:, None] - jnp.arange(2 * WINDOW)[None, :]
    in_band = (rel >= 0) & (rel <= WINDOW)
    has_prev = (jnp.arange(nb) > 0)[:, None, None] | (jnp.arange(2 * WINDOW) >= WINDOW)[None, None, :]
    mask = in_band[None] & has_prev
    s = jnp.where(mask[None, :, None, None], s, NEG_INF)
    o = sink_attend(s, sinks, band(v), 'bnkgqs,bnskd->bnqkgd')
    return o.reshape(B_, L, ATTN_WIDTH)


def swa_sample(q, k_all, v_all, sinks):
    B_, T = q.shape[:2]
    S = k_all.shape[1]
    qg = q.reshape(B_, T, N_KV_HEADS, Q_PER_KV, HEAD_DIM)
    s = jnp.einsum('btkgd,bskd->bkgts', qg, k_all, preferred_element_type=jnp.float32) * ATTN_SCALE
    rel = (WINDOW + jnp.arange(T))[:, None] - jnp.arange(S)[None, :]
    mask = (rel >= 0) & (rel <= WINDOW)
    s = jnp.where(mask, s, NEG_INF)
    o = sink_attend(s, sinks, v_all, 'bkgts,bskd->btkgd')
    return o.reshape(B_, T, ATTN_WIDTH)


def _complex_affine_combine(e1, e2):
    a1r, a1i, b1r, b1i = e1
    a2r, a2i, b2r, b2i = e2
    ar, ai = a2r[:, None], a2i[:, None]
    return (a2r * a1r - a2i * a1i, a2r * a1i + a2i * a1r,
            ar * b1r - ai * b1i + b2r, ar * b1i + ai * b1r + b2i)


def s5_mix(u, h0_re, h0_im, a_re, a_im, log_dt, b_re, b_im, c_re, c_im, d_skip, w_glu):
    f32 = jnp.float32
    B_, L, _ = u.shape
    ug = u.reshape(B_, L, S5_GROUPS, S5_GROUP_CH).astype(f32)
    lr, li = a_re.astype(f32), a_im.astype(f32)
    dt = jnp.exp(log_dt.astype(f32))[:, None]
    mag = jnp.exp(lr * dt)
    ab_re, ab_im = mag * jnp.cos(li * dt), mag * jnp.sin(li * dt)
    den = lr * lr + li * li
    f_re = ((ab_re - 1.0) * lr + ab_im * li) / den
    f_im = (ab_im * lr - (ab_re - 1.0) * li) / den
    br, bi = b_re.astype(f32), b_im.astype(f32)
    bb_re = f_re[..., None] * br - f_im[..., None] * bi
    bb_im = f_re[..., None] * bi + f_im[..., None] * br
    bu_re = jnp.einsum('blgc,gpc->lbgp', ug, bb_re)
    bu_im = jnp.einsum('blgc,gpc->lbgp', ug, bb_im)
    a_seq_re = jnp.broadcast_to(ab_re, (L,) + ab_re.shape)
    a_seq_im = jnp.broadcast_to(ab_im, (L,) + ab_im.shape)
    acc_re, acc_im, h_re, h_im = lax.associative_scan(
        _complex_affine_combine, (a_seq_re, a_seq_im, bu_re, bu_im))
    if h0_re is not None:
        p_re, p_im = h0_re.astype(f32)[None], h0_im.astype(f32)[None]
        h_re, h_im = (h_re + acc_re[:, None] * p_re - acc_im[:, None] * p_im,
                      h_im + acc_re[:, None] * p_im + acc_im[:, None] * p_re)
    y = (jnp.einsum('lbgp,gcp->blgc', h_re, c_re.astype(f32))
         - jnp.einsum('lbgp,gcp->blgc', h_im, c_im.astype(f32))
         + d_skip.astype(f32) * ug)
    y = jax.nn.gelu(y.reshape(B_, L, S5_WIDTH)).astype(u.dtype)
    out = y * jax.nn.sigmoid(y @ w_glu)
    return out, h_re[-1].astype(u.dtype), h_im[-1].astype(u.dtype)


def gated_delta_chunked(q, k, v, g, beta):
    f32 = jnp.float32
    B_, L, H, DK = k.shape
    DV = v.shape[-1]
    C = DN_CHUNK
    n = L // C

    def chunks(t):
        t = t.reshape((B_, n, C, H) + t.shape[3:])
        return jnp.moveaxis(t, (1, 2), (0, 3))

    qc, kc, vc, gc, bc = map(chunks, (q, k, v, g, beta))
    gcum = jnp.cumsum(gc, axis=-1)
    idx = jnp.arange(C)
    incl = idx[:, None] >= idx[None, :]
    strict = idx[:, None] > idx[None, :]
    decay = jnp.exp(jnp.where(incl, gcum[..., :, None] - gcum[..., None, :], -jnp.inf))
    kb = kc * bc[..., None]
    lower = jnp.where(strict, jnp.einsum('nbhid,nbhjd->nbhij', kb, kc) * decay, 0.0)
    rhs = jnp.concatenate([vc * bc[..., None], kb * jnp.exp(gcum)[..., None]], -1)
    sol = lax.linalg.triangular_solve(lower + jnp.eye(C, dtype=f32), rhs,
                                      left_side=True, lower=True, unit_diagonal=True)
    u_c, w_c = sol[..., :DV], sol[..., DV:]
    qk = jnp.einsum('nbhid,nbhjd->nbhij', qc, kc) * decay

    def step(S, xs):
        qi, ki, ui, wi, gi, ai = xs
        v_new = ui - jnp.einsum('bhcd,bhde->bhce', wi, S)
        o = (jnp.einsum('bhcd,bhde->bhce', qi * jnp.exp(gi)[..., None], S)
             + jnp.einsum('bhij,bhje->bhie', ai, v_new))
        g_last = gi[..., -1:]
        S = (S * jnp.exp(g_last)[..., None]
             + jnp.einsum('bhcd,bhce->bhde', ki * jnp.exp(g_last - gi)[..., None], v_new))
        return S, o

    S0 = jnp.zeros((B_, H, DK, DV), f32)
    S, o = lax.scan(step, S0, (qc, kc, u_c, w_c, gcum, qk))
    o = jnp.moveaxis(o, (0, 3), (1, 2)).reshape(B_, L, H, DV)
    return o, S


def gated_delta_recurrent(q, k, v, g, beta, S0):
    def step(S, xs):
        qt, kt, vt, gt, bt = xs
        S = S * jnp.exp(gt)[..., None, None]
        delta = (vt - jnp.einsum('bhd,bhde->bhe', kt, S)) * bt[..., None]
        S = S + jnp.einsum('bhd,bhe->bhde', kt, delta)
        return S, jnp.einsum('bhd,bhde->bhe', qt, S)

    xs = tuple(jnp.swapaxes(t, 0, 1) for t in (q, k, v, g, beta))
    S, o = lax.scan(step, S0, xs)
    return jnp.swapaxes(o, 0, 1), S


def moe_swiglu(x, router, w_gate, w_up, w_down):
    shp = x.shape
    xt = x.reshape(-1, shp[-1])
    logits = jnp.einsum('nd,de->ne', xt, router, preferred_element_type=jnp.float32)
    top_val, top_idx = lax.top_k(logits, TOP_K)
    top_w = jax.nn.softmax(top_val, axis=-1)
    gate = jnp.einsum('nk,nke->ne', top_w,
                      jax.nn.one_hot(top_idx, N_EXPERTS, dtype=jnp.float32)).astype(x.dtype)
    out = jnp.zeros_like(xt)
    for e in range(N_EXPERTS):
        out = out + gate[:, e:e + 1] * swiglu(xt, w_gate[e], w_up[e], w_down[e])
    return out.reshape(shp)


def even_layer(x, kbuf, vbuf, h_re, h_im, norm_mix, w_in, q_norm, k_norm, sinks,
               a_re, a_im, log_dt, b_re, b_im, c_re, c_im, d_skip, w_glu,
               w_out, norm_ffn, w_gate, w_up, w_down):
    B_, L, _ = x.shape
    proj = rms_norm(x, norm_mix) @ w_in
    q = rms_norm(proj[..., :ATTN_WIDTH].reshape(B_, L, N_HEADS, HEAD_DIM), q_norm)
    k = rms_norm(proj[..., ATTN_WIDTH:ATTN_WIDTH + KV_WIDTH].reshape(B_, L, N_KV_HEADS, HEAD_DIM), k_norm)
    v = proj[..., ATTN_WIDTH + KV_WIDTH:ATTN_WIDTH + 2 * KV_WIDTH].reshape(B_, L, N_KV_HEADS, HEAD_DIM)
    u = proj[..., ATTN_WIDTH + 2 * KV_WIDTH:]
    if kbuf is None:
        attn = swa_prompt(q, k, v, sinks)
        new_k, new_v = k[:, -WINDOW:], v[:, -WINDOW:]
    else:
        k_all = jnp.concatenate([kbuf, k], 1)
        v_all = jnp.concatenate([vbuf, v], 1)
        attn = swa_sample(q, k_all, v_all, sinks)
        new_k, new_v = k_all[:, -WINDOW:], v_all[:, -WINDOW:]
    ssm, new_re, new_im = s5_mix(u, h_re, h_im, a_re, a_im, log_dt, b_re, b_im, c_re, c_im, d_skip, w_glu)
    x = x + jnp.concatenate([attn, ssm], -1) @ w_out
    x = x + swiglu(rms_norm(x, norm_ffn), w_gate, w_up, w_down)
    return x, new_k, new_v, new_re, new_im


def odd_layer(x, S0, conv_buf, norm_mix, w_in, conv_w, a_log, dt_bias, out_norm, w_out,
              norm_ffn, router, w_gate, w_up, w_down):
    f32 = jnp.float32
    B_, L, _ = x.shape
    proj = rms_norm(x, norm_mix) @ w_in
    qkv = proj[..., :DN_CONV_DIM]
    z = proj[..., DN_CONV_DIM:DN_CONV_DIM + DN_V_W]
    b = proj[..., DN_CONV_DIM + DN_V_W:DN_CONV_DIM + DN_V_W + DN_V_HEADS]
    a = proj[..., DN_CONV_DIM + DN_V_W + DN_V_HEADS:]
    if conv_buf is None:
        conv_buf = jnp.zeros((B_, DN_CONV_K - 1, DN_CONV_DIM), x.dtype)
    xp = jnp.concatenate([conv_buf, qkv], 1)
    conv = xp[:, :L] * conv_w[0]
    for i in range(1, DN_CONV_K):
        conv = conv + xp[:, i:i + L] * conv_w[i]
    new_buf = xp[:, L:]
    c = jax.nn.silu(conv.astype(f32))
    rep = DN_V_HEADS // DN_K_HEADS
    q = l2norm(c[..., :DN_QK_W].reshape(B_, L, DN_K_HEADS, DN_K_DIM)) * (DN_K_DIM ** -0.5)
    k = l2norm(c[..., DN_QK_W:2 * DN_QK_W].reshape(B_, L, DN_K_HEADS, DN_K_DIM))
    v = c[..., 2 * DN_QK_W:].reshape(B_, L, DN_V_HEADS, DN_V_DIM)
    q = jnp.repeat(q, rep, axis=2)
    k = jnp.repeat(k, rep, axis=2)
    beta = jax.nn.sigmoid(b.astype(f32))
    g = -jnp.exp(a_log.astype(f32)) * jax.nn.softplus(a.astype(f32) + dt_bias.astype(f32))
    if S0 is None:
        o, S = gated_delta_chunked(q, k, v, g, beta)
    else:
        o, S = gated_delta_recurrent(q, k, v, g, beta, S0.astype(f32))
    o = rms_norm(o, out_norm) * jax.nn.silu(z.astype(f32).reshape(B_, L, DN_V_HEADS, DN_V_DIM))
    x = x + o.reshape(B_, L, DN_V_W).astype(x.dtype) @ w_out
    x = x + moe_swiglu(rms_norm(x, norm_ffn), router, w_gate, w_up, w_down)
    return x, S.astype(x.dtype), new_buf


def setup_inputs(seed: int = 0) -> dict:
    key = jax.random.key(seed)
    keys = iter(jax.random.split(key, 64))
    f32 = jnp.float32

    def nrm(shape, scale=1.0):
        return scale * jax.random.normal(next(keys), shape, f32)

    def gain(shape):
        return 1.0 + 0.01 * jax.random.normal(next(keys), shape, f32)

    def unif(shape, lo, hi):
        return jax.random.uniform(next(keys), shape, f32, lo, hi)

    NE, NO = N_EVEN, N_ODD
    dt_dn = jnp.exp(unif((NO, DN_V_HEADS), math.log(1e-3), math.log(1e-1)))
    return {
        'x_prompt': nrm((BATCH, SEQ, D_MODEL)),
        'x_sample': nrm((DEC_BATCH, DEC_SEQ, D_MODEL)),
        'cache_win_k': nrm((NE, DEC_BATCH, WINDOW, N_KV_HEADS, HEAD_DIM)),
        'cache_win_v': nrm((NE, DEC_BATCH, WINDOW, N_KV_HEADS, HEAD_DIM)),
        'state_s5_re': nrm((NE, DEC_BATCH, S5_GROUPS, S5_STATE), 0.1),
        'state_s5_im': nrm((NE, DEC_BATCH, S5_GROUPS, S5_STATE), 0.1),
        'state_dn': nrm((NO, DEC_BATCH, DN_V_HEADS, DN_K_DIM, DN_V_DIM), 0.1),
        'state_dn_conv': nrm((NO, DEC_BATCH, DN_CONV_K - 1, DN_CONV_DIM)),
        'e_norm_mix': gain((NE, D_MODEL)),
        'e_w_in': nrm((NE, D_MODEL, EVEN_IN), D_MODEL ** -0.5),
        'e_q_norm': gain((NE, HEAD_DIM)),
        'e_k_norm': gain((NE, HEAD_DIM)),
        'e_sinks': nrm((NE, N_HEADS), 0.5),
        'e_s5_a_re': -0.5 + 0.01 * nrm((NE, S5_GROUPS, S5_STATE)),
        'e_s5_a_im': jnp.pi * jnp.arange(S5_STATE, dtype=f32) + 0.01 * nrm((NE, S5_GROUPS, S5_STATE)),
        'e_s5_log_dt': unif((NE, S5_GROUPS), math.log(1e-3), math.log(1e-1)),
        'e_s5_b_re': nrm((NE, S5_GROUPS, S5_STATE, S5_GROUP_CH), S5_GROUP_CH ** -0.5),
        'e_s5_b_im': nrm((NE, S5_GROUPS, S5_STATE, S5_GROUP_CH), S5_GROUP_CH ** -0.5),
        'e_s5_c_re': nrm((NE, S5_GROUPS, S5_GROUP_CH, S5_STATE), S5_STATE ** -0.5),
        'e_s5_c_im': nrm((NE, S5_GROUPS, S5_GROUP_CH, S5_STATE), S5_STATE ** -0.5),
        'e_s5_d': nrm((NE, S5_GROUPS, S5_GROUP_CH)),
        'e_s5_w_glu': nrm((NE, S5_WIDTH, S5_WIDTH), S5_WIDTH ** -0.5),
        'e_w_out': nrm((NE, EVEN_MIX, D_MODEL), EVEN_MIX ** -0.5),
        'e_norm_ffn': gain((NE, D_MODEL)),
        'e_ffn_w_gate': nrm((NE, D_MODEL, D_FF), D_MODEL ** -0.5),
        'e_ffn_w_up': nrm((NE, D_MODEL, D_FF), D_MODEL ** -0.5),
        'e_ffn_w_down': nrm((NE, D_FF, D_MODEL), D_FF ** -0.5),
        'o_norm_mix': gain((NO, D_MODEL)),
        'o_w_in': nrm((NO, D_MODEL, ODD_IN), D_MODEL ** -0.5),
        'o_conv_w': nrm((NO, DN_CONV_K, DN_CONV_DIM), DN_CONV_K ** -0.5),
        'o_a_log': jnp.log(unif((NO, DN_V_HEADS), 1.0, 16.0)),
        'o_dt_bias': dt_dn + jnp.log(-jnp.expm1(-dt_dn)),
        'o_out_norm': gain((NO, DN_V_DIM)),
        'o_w_out': nrm((NO, DN_V_W, D_MODEL), DN_V_W ** -0.5),
        'o_norm_ffn': gain((NO, D_MODEL)),
        'o_router': nrm((NO, D_MODEL, N_EXPERTS), D_MODEL ** -0.5),
        'o_exp_w_gate': nrm((NO, N_EXPERTS, D_MODEL, D_FF), D_MODEL ** -0.5),
        'o_exp_w_up': nrm((NO, N_EXPERTS, D_MODEL, D_FF), D_MODEL ** -0.5),
        'o_exp_w_down': nrm((NO, N_EXPERTS, D_FF, D_MODEL), D_FF ** -0.5),
    }


def reference(x_prompt, x_sample, cache_win_k, cache_win_v, state_s5_re, state_s5_im, state_dn, state_dn_conv,
              e_norm_mix, e_w_in, e_q_norm, e_k_norm, e_sinks,
              e_s5_a_re, e_s5_a_im, e_s5_log_dt, e_s5_b_re, e_s5_b_im, e_s5_c_re, e_s5_c_im, e_s5_d, e_s5_w_glu,
              e_w_out, e_norm_ffn, e_ffn_w_gate, e_ffn_w_up, e_ffn_w_down,
              o_norm_mix, o_w_in, o_conv_w, o_a_log, o_dt_bias, o_out_norm, o_w_out, o_norm_ffn,
              o_router, o_exp_w_gate, o_exp_w_up, o_exp_w_down):
    hp, hs = x_prompt, x_sample
    wk_p, wv_p, sr_p, si_p, dn_p, dc_p = [], [], [], [], [], []
    wk_s, wv_s, sr_s, si_s, dn_s, dc_s = [], [], [], [], [], []
    for layer in range(DEPTH):
        j = layer // 2
        if layer % 2 == 0:
            w = (e_norm_mix[j], e_w_in[j], e_q_norm[j], e_k_norm[j], e_sinks[j],
                 e_s5_a_re[j], e_s5_a_im[j], e_s5_log_dt[j], e_s5_b_re[j], e_s5_b_im[j],
                 e_s5_c_re[j], e_s5_c_im[j], e_s5_d[j], e_s5_w_glu[j],
                 e_w_out[j], e_norm_ffn[j], e_ffn_w_gate[j], e_ffn_w_up[j], e_ffn_w_down[j])
            hp, kp, vp, rp, ip = even_layer(hp, None, None, None, None, *w)
            hs, ks, vs, rs, is_ = even_layer(hs, cache_win_k[j], cache_win_v[j],
                                             state_s5_re[j], state_s5_im[j], *w)
            wk_p.append(kp); wv_p.append(vp); sr_p.append(rp); si_p.append(ip)
            wk_s.append(ks); wv_s.append(vs); sr_s.append(rs); si_s.append(is_)
        else:
            w = (o_norm_mix[j], o_w_in[j], o_conv_w[j], o_a_log[j], o_dt_bias[j], o_out_norm[j],
                 o_w_out[j], o_norm_ffn[j], o_router[j], o_exp_w_gate[j], o_exp_w_up[j], o_exp_w_down[j])
            hp, Sp, cp = odd_layer(hp, None, None, *w)
            hs, Ss, cs = odd_layer(hs, state_dn[j], state_dn_conv[j], *w)
            dn_p.append(Sp); dc_p.append(cp)
            dn_s.append(Ss); dc_s.append(cs)
    return (hp, hs,
            jnp.stack(wk_p), jnp.stack(wv_p), jnp.stack(sr_p), jnp.stack(si_p), jnp.stack(dn_p), jnp.stack(dc_p),
            jnp.stack(wk_s), jnp.stack(wv_s), jnp.stack(sr_s), jnp.stack(si_s), jnp.stack(dn_s), jnp.stack(dc_s))
```

```python
import functools

import jax
import jax.numpy as jnp
from jax import lax
from jax.experimental import pallas as pl
from jax.experimental.pallas import tpu as pltpu

F32 = jnp.float32
BF16 = jnp.bfloat16

D_MODEL = 2048
N_HEADS = 16
N_KV_HEADS = 4
HEAD_DIM = 64
Q_PER_KV = N_HEADS // N_KV_HEADS
WINDOW = 128
ATTN_WIDTH = N_HEADS * HEAD_DIM
KV_WIDTH = N_KV_HEADS * HEAD_DIM
ATTN_SCALE = HEAD_DIM ** -0.5
S5_WIDTH = D_MODEL // 2
S5_GROUP_CH = 16
S5_GROUPS = S5_WIDTH // S5_GROUP_CH
S5_STATE = 64
S5_LANES = S5_GROUPS * S5_STATE
DN_K_HEADS = 16
DN_V_HEADS = 32
DN_K_DIM = 128
DN_V_DIM = 128
DN_CONV_K = 4
DN_CHUNK = 64
DN_QK_W = DN_K_HEADS * DN_K_DIM
DN_V_W = DN_V_HEADS * DN_V_DIM
DN_CONV_DIM = 2 * DN_QK_W + DN_V_W
D_FF = 5632
N_EXPERTS = 8
EPS = 1e-6
NEG_INF = -1e30

LANES = 128
SUBLANES = 8
VMEM_LIMIT = 56 * 1024 * 1024


def _params(*sem):
    return pltpu.CompilerParams(dimension_semantics=sem, vmem_limit_bytes=VMEM_LIMIT)


def _rms(x, g):
    return x * lax.rsqrt(jnp.mean(x * x, axis=-1, keepdims=True) + EPS) * g


def _dot(a, b):
    return jnp.dot(a, b, preferred_element_type=F32)


def _dot_nt(a, b):
    return lax.dot_general(a, b, (((1,), (1,)), ((), ())), preferred_element_type=F32)


def _split3(x):
    hi = x.astype(BF16)
    r = x - hi.astype(F32)
    mid = r.astype(BF16)
    lo = (r - mid.astype(F32)).astype(BF16)
    return hi, mid, lo


def _norm_matmul_kernel(x_ref, g_ref, w_ref, o_ref, xn_ref):
    @pl.when(pl.program_id(1) == 0)
    def _():
        xn_ref[...] = _rms(x_ref[...], g_ref[...]).astype(BF16)

    o_ref[...] = _dot(xn_ref[...], w_ref[...]).astype(o_ref.dtype)


def norm_matmul(x, g, w, *, tm, tn, out_dtype=F32):
    m, d = x.shape
    n = w.shape[1]
    return pl.pallas_call(
        _norm_matmul_kernel,
        grid=(m // tm, n // tn),
        in_specs=[pl.BlockSpec((tm, d), lambda i, j: (i, 0)),
                  pl.BlockSpec((1, d), lambda i, j: (0, 0)),
                  pl.BlockSpec((d, tn), lambda i, j: (0, j))],
        out_specs=pl.BlockSpec((tm, tn), lambda i, j: (i, j)),
        out_shape=jax.ShapeDtypeStruct((m, n), out_dtype),
        scratch_shapes=[pltpu.VMEM((tm, d), BF16)],
        compiler_params=_params("parallel", "arbitrary"),
        name="norm_matmul",
    )(x, g.reshape(1, d), w)


def _matmul_residual_kernel(*refs, n_pairs):
    x_ref = refs[0]
    o_ref = refs[-1]
    acc = x_ref[...]
    for a_ref, w_ref in zip(refs[1:1 + n_pairs], refs[1 + n_pairs:1 + 2 * n_pairs]):
        acc = acc + _dot(a_ref[...], w_ref[...])
    o_ref[...] = acc


def matmul_residual(x, a_list, w_list, *, tm, tn):
    m, n = x.shape
    in_specs = [pl.BlockSpec((tm, tn), lambda i, j: (i, j))]
    in_specs += [pl.BlockSpec((tm, a.shape[1]), lambda i, j: (i, 0)) for a in a_list]
    in_specs += [pl.BlockSpec((w.shape[0], tn), lambda i, j: (0, j)) for w in w_list]
    return pl.pallas_call(
        functools.partial(_matmul_residual_kernel, n_pairs=len(a_list)),
        grid=(m // tm, n // tn),
        in_specs=in_specs,
        out_specs=pl.BlockSpec((tm, tn), lambda i, j: (i, j)),
        out_shape=jax.ShapeDtypeStruct((m, n), F32),
        compiler_params=_params("parallel", "arbitrary"),
        name="matmul_residual",
    )(x, *a_list, *w_list)


def _attn_prompt_kernel(q_ref, kc_ref, kp_ref, vc_ref, vp_ref, qn_ref, kn_ref, sink_ref,
                        o_ref, wk_ref, wv_ref):
    has_prev = pl.program_id(1) > 0
    qg = qn_ref[...]
    kg = kn_ref[...]
    row = lax.broadcasted_iota(jnp.int32, (WINDOW, 2 * WINDOW), 0)
    col = lax.broadcasted_iota(jnp.int32, (WINDOW, 2 * WINDOW), 1)
    rel = (WINDOW + row) - col
    mask = (rel >= 0) & (rel <= WINDOW) & ((col >= WINDOW) | has_prev)
    for kvh in range(N_KV_HEADS):
        sl = slice(kvh * HEAD_DIM, (kvh + 1) * HEAD_DIM)
        kc = _rms(kc_ref[:, sl], kg)
        kp = _rms(kp_ref[:, sl], kg)
        vc = vc_ref[:, sl]
        wk_ref[0, :, sl] = kc
        wv_ref[0, :, sl] = vc
        kband = jnp.concatenate([kp, kc], axis=0).astype(BF16)
        vband = jnp.concatenate([vp_ref[:, sl], vc], axis=0).astype(BF16)
        for g in range(Q_PER_KV):
            h = kvh * Q_PER_KV + g
            hs = slice(h * HEAD_DIM, (h + 1) * HEAD_DIM)
            q = _rms(q_ref[:, hs], qg).astype(BF16)
            s = _dot_nt(q, kband) * ATTN_SCALE
            s = jnp.where(mask, s, NEG_INF)
            sk = sink_ref[h:h + 1, :]
            m = jnp.maximum(jnp.max(s, axis=-1, keepdims=True), sk)
            p = jnp.exp(s - m)
            p = p / (jnp.sum(p, axis=-1, keepdims=True) + jnp.exp(sk - m))
            o_ref[:, hs] = _dot(p.astype(BF16), vband).astype(o_ref.dtype)


def attn_prompt(qkv, q_norm, k_norm, sinks, *, batch, seq):
    nb = seq // WINDOW
    kcol = ATTN_WIDTH // KV_WIDTH
    cur = lambda c: (lambda b, n: (b * nb + n, c))
    prev = lambda c: (lambda b, n: (jnp.maximum(b * nb + n - 1, 0), c))
    return pl.pallas_call(
        _attn_prompt_kernel,
        grid=(batch, nb),
        in_specs=[pl.BlockSpec((WINDOW, ATTN_WIDTH), cur(0)),
                  pl.BlockSpec((WINDOW, KV_WIDTH), cur(kcol)),
                  pl.BlockSpec((WINDOW, KV_WIDTH), prev(kcol)),
                  pl.BlockSpec((WINDOW, KV_WIDTH), cur(kcol + 1)),
                  pl.BlockSpec((WINDOW, KV_WIDTH), prev(kcol + 1)),
                  pl.BlockSpec((1, HEAD_DIM), lambda b, n: (0, 0)),
                  pl.BlockSpec((1, HEAD_DIM), lambda b, n: (0, 0)),
                  pl.BlockSpec((N_HEADS, 1), lambda b, n: (0, 0))],
        out_specs=[pl.BlockSpec((WINDOW, ATTN_WIDTH), lambda b, n: (b * nb + n, 0)),
                   pl.BlockSpec((1, WINDOW, KV_WIDTH), lambda b, n: (b, 0, 0)),
                   pl.BlockSpec((1, WINDOW, KV_WIDTH), lambda b, n: (b, 0, 0))],
        out_shape=[jax.ShapeDtypeStruct((batch * seq, ATTN_WIDTH), BF16),
                   jax.ShapeDtypeStruct((batch, WINDOW, KV_WIDTH), F32),
                   jax.ShapeDtypeStruct((batch, WINDOW, KV_WIDTH), F32)],
        compiler_params=_params("parallel", "arbitrary"),
        name="attn_prompt",
    )(qkv, qkv, qkv, qkv, qkv, q_norm.reshape(1, HEAD_DIM), k_norm.reshape(1, HEAD_DIM),
      sinks.reshape(N_HEADS, 1))


def _attn_sample_kernel(qkv_ref, ck_ref, cv_ref, qn_ref, kn_ref, sink_ref,
                        o_ref, wk_ref, wv_ref, *, bt):
    qg = qn_ref[...]
    kg = kn_ref[...]
    for b in range(bt):
        wk_ref[b, 0:WINDOW - 1, :] = ck_ref[b, 1:WINDOW, :]
        wv_ref[b, 0:WINDOW - 1, :] = cv_ref[b, 1:WINDOW, :]
        wv_ref[b, WINDOW - 1:WINDOW, :] = qkv_ref[b:b + 1, ATTN_WIDTH + KV_WIDTH:ATTN_WIDTH + 2 * KV_WIDTH]
        for kvh in range(N_KV_HEADS):
            sl = slice(kvh * HEAD_DIM, (kvh + 1) * HEAD_DIM)
            kn = _rms(qkv_ref[b:b + 1, ATTN_WIDTH + kvh * HEAD_DIM:ATTN_WIDTH + (kvh + 1) * HEAD_DIM], kg)
            vn = qkv_ref[b:b + 1, ATTN_WIDTH + KV_WIDTH + kvh * HEAD_DIM:
                         ATTN_WIDTH + KV_WIDTH + (kvh + 1) * HEAD_DIM]
            wk_ref[b, WINDOW - 1:WINDOW, sl] = kn
            q4 = jnp.concatenate(
                [qkv_ref[b:b + 1, (kvh * Q_PER_KV + g) * HEAD_DIM:(kvh * Q_PER_KV + g + 1) * HEAD_DIM]
                 for g in range(Q_PER_KV)], axis=0)
            q4 = _rms(q4, qg)
            kwin = ck_ref[b, :, sl].astype(BF16)
            vwin = cv_ref[b, :, sl].astype(BF16)
            s_c = _dot_nt(q4.astype(BF16), kwin) * ATTN_SCALE
            s_n = jnp.sum(q4 * kn, axis=-1, keepdims=True) * ATTN_SCALE
            sk = sink_ref[kvh * Q_PER_KV:(kvh + 1) * Q_PER_KV, :]
            m = jnp.maximum(jnp.maximum(jnp.max(s_c, axis=-1, keepdims=True), s_n), sk)
            p_c = jnp.exp(s_c - m)
            p_n = jnp.exp(s_n - m)
            den = jnp.sum(p_c, axis=-1, keepdims=True) + p_n + jnp.exp(sk - m)
            o4 = _dot((p_c / den).astype(BF16), vwin) + (p_n / den) * vn
            for g in range(Q_PER_KV):
                h = kvh * Q_PER_KV + g
                o_ref[b:b + 1, h * HEAD_DIM:(h + 1) * HEAD_DIM] = o4[g:g + 1, :].astype(o_ref.dtype)


def attn_sample(qkv, cache_k, cache_v, q_norm, k_norm, sinks, *, bt=8):
    nb = qkv.shape[0]
    width = qkv.shape[1]
    win = pl.BlockSpec((bt, WINDOW, KV_WIDTH), lambda i: (i, 0, 0))
    return pl.pallas_call(
        functools.partial(_attn_sample_kernel, bt=bt),
        grid=(nb // bt,),
        in_specs=[pl.BlockSpec((bt, width), lambda i: (i, 0)), win, win,
                  pl.BlockSpec((1, HEAD_DIM), lambda i: (0, 0)),
                  pl.BlockSpec((1, HEAD_DIM), lambda i: (0, 0)),
                  pl.BlockSpec((N_HEADS, 1), lambda i: (0, 0))],
        out_specs=[pl.BlockSpec((bt, ATTN_WIDTH), lambda i: (i, 0)), win, win],
        out_shape=[jax.ShapeDtypeStruct((nb, ATTN_WIDTH), BF16),
                   jax.ShapeDtypeStruct((nb, WINDOW, KV_WIDTH), F32),
                   jax.ShapeDtypeStruct((nb, WINDOW, KV_WIDTH), F32)],
        compiler_params=_params("parallel"),
        name="attn_sample",
    )(qkv, cache_k, cache_v, q_norm.reshape(1, HEAD_DIM), k_norm.reshape(1, HEAD_DIM),
      sinks.reshape(N_HEADS, 1))


S5_KCH = S5_WIDTH // LANES
S5_CHUNK_STATES = S5_LANES // S5_KCH
S5_SCAN_TILES = 4


def _s5_kernel(u_ref, h0re_ref, h0im_ref, are_ref, aim_ref, wbu_ref, wc_ref, d_ref, wglu_ref,
               y_ref, hre_ref, him_ref, hs_ref, st_ref, *, rows, steps):
    rt = rows * steps
    u = u_ref[...].reshape(rt, S5_WIDTH)

    @pl.when(pl.program_id(0) == 0)
    def _():
        st_ref[:, :S5_LANES] = h0re_ref[...]
        st_ref[:, S5_LANES:] = h0im_ref[...]

    ub = u.astype(BF16)
    nre = S5_LANES // LANES
    tpk = S5_CHUNK_STATES // LANES
    for k in range(S5_KCH):
        r = _dot(ub[:, k * LANES:(k + 1) * LANES], wbu_ref[k])
        for a in range(tpk):
            hs_ref[k * tpk + a] = r[:, a * LANES:(a + 1) * LANES]
            hs_ref[nre + k * tpk + a] = r[:, (tpk + a) * LANES:(tpk + a + 1) * LANES]

    nt = S5_SCAN_TILES
    for j in range(nre // nt):
        tiles = range(j * nt, (j + 1) * nt)
        ar = [jnp.broadcast_to(are_ref[:, a * LANES:(a + 1) * LANES], (SUBLANES, LANES)) for a in tiles]
        ai = [jnp.broadcast_to(aim_ref[:, a * LANES:(a + 1) * LANES], (SUBLANES, LANES)) for a in tiles]

        def advance(r8, hr, hi):
            nr, ni = [], []
            for n, a in enumerate(tiles):
                nr.append(ar[n] * hr[n] - ai[n] * hi[n] + hs_ref[a, r8, :])
                ni.append(ar[n] * hi[n] + ai[n] * hr[n] + hs_ref[nre + a, r8, :])
                hs_ref[a, r8, :] = nr[n]
                hs_ref[nre + a, r8, :] = ni[n]
            return nr, ni

        if steps == 1:
            def group(rg, carry):
                r8 = pl.ds(pl.multiple_of(rg * SUBLANES, SUBLANES), SUBLANES)
                hr = [st_ref[r8, a * LANES:(a + 1) * LANES] for a in tiles]
                hi = [st_ref[r8, S5_LANES + a * LANES:S5_LANES + (a + 1) * LANES] for a in tiles]
                nr, ni = advance(r8, hr, hi)
                for n, a in enumerate(tiles):
                    st_ref[r8, a * LANES:(a + 1) * LANES] = nr[n]
                    st_ref[r8, S5_LANES + a * LANES:S5_LANES + (a + 1) * LANES] = ni[n]
                return carry
            lax.fori_loop(0, rows // SUBLANES, group, 0)
        else:
            def step(t, carry):
                nr, ni = advance(pl.ds(t, SUBLANES, stride=steps), *carry)
                return tuple(nr), tuple(ni)
            hr0 = tuple(st_ref[:, a * LANES:(a + 1) * LANES] for a in tiles)
            hi0 = tuple(st_ref[:, S5_LANES + a * LANES:S5_LANES + (a + 1) * LANES] for a in tiles)
            hr, hi = lax.fori_loop(0, steps, step, (hr0, hi0))
            for n, a in enumerate(tiles):
                st_ref[:, a * LANES:(a + 1) * LANES] = hr[n]
                st_ref[:, S5_LANES + a * LANES:S5_LANES + (a + 1) * LANES] = hi[n]

    ys = []
    for k in range(S5_KCH):
        hre = jnp.concatenate([hs_ref[k * tpk + a] for a in range(tpk)], axis=1).astype(BF16)
        him = jnp.concatenate([hs_ref[nre + k * tpk + a] for a in range(tpk)], axis=1).astype(BF16)
        yk = _dot(hre, wc_ref[0, k]) + _dot(him, wc_ref[1, k])
        ys.append(yk + d_ref[:, k * LANES:(k + 1) * LANES] * u[:, k * LANES:(k + 1) * LANES])
    y = jax.nn.gelu(jnp.concatenate(ys, axis=1))
    out = y * jax.nn.sigmoid(_dot(y.astype(BF16), wglu_ref[...]))
    y_ref[...] = out.reshape(y_ref.shape).astype(y_ref.dtype)

    @pl.when(pl.program_id(0) == pl.num_programs(0) - 1)
    def _():
        hre_ref[...] = st_ref[:, :S5_LANES]
        him_ref[...] = st_ref[:, S5_LANES:]


def s5_mix(u, h0_re, h0_im, consts, *, rows, seq, steps):
    a_re, a_im, w_bu, w_c, d_skip, w_glu = consts
    if seq > 1:
        u_spec = pl.BlockSpec((rows, steps, S5_WIDTH), lambda c: (0, c, 0))
    else:
        u_spec = pl.BlockSpec((rows, S5_WIDTH), lambda c: (0, 0))
    full = lambda shape: pl.BlockSpec(shape, lambda c: (0,) * len(shape))
    return pl.pallas_call(
        functools.partial(_s5_kernel, rows=rows, steps=steps),
        grid=(seq // steps,),
        in_specs=[u_spec, full((rows, S5_LANES)), full((rows, S5_LANES)),
                  full((1, S5_LANES)), full((1, S5_LANES)),
                  full(w_bu.shape), full(w_c.shape), full((1, S5_WIDTH)), full(w_glu.shape)],
        out_specs=[u_spec, full((rows, S5_LANES)), full((rows, S5_LANES))],
        out_shape=[jax.ShapeDtypeStruct(u.shape, BF16),
                   jax.ShapeDtypeStruct((rows, S5_LANES), F32),
                   jax.ShapeDtypeStruct((rows, S5_LANES), F32)],
        scratch_shapes=[pltpu.VMEM((2 * S5_LANES // LANES, rows * steps, LANES), F32),
                        pltpu.VMEM((rows, 2 * S5_LANES), F32)],
        compiler_params=_params("arbitrary"),
        name="s5_mix",
    )(u, h0_re, h0_im, a_re, a_im, w_bu, w_c, d_skip, w_glu)


def s5_constants(a_re, a_im, log_dt, b_re, b_im, c_re, c_im, d_skip, w_glu):
    lr, li = a_re, a_im
    dt = jnp.exp(log_dt)[:, None]
    mag = jnp.exp(lr * dt)
    ab_re, ab_im = mag * jnp.cos(li * dt), mag * jnp.sin(li * dt)
    den = lr * lr + li * li
    f_re = ((ab_re - 1.0) * lr + ab_im * li) / den
    f_im = (ab_im * lr - (ab_re - 1.0) * li) / den
    bb_re = f_re[..., None] * b_re - f_im[..., None] * b_im
    bb_im = f_re[..., None] * b_im + f_im[..., None] * b_re
    gpc = LANES // S5_GROUP_CH
    eye = jnp.eye(gpc, dtype=F32)

    def bu_blocks(bb):
        t = bb.reshape(S5_KCH, gpc, S5_STATE, S5_GROUP_CH)
        return jnp.einsum('kgpc,gh->kgchp', t, eye).reshape(S5_KCH, LANES, gpc * S5_STATE)

    def c_blocks(c):
        t = c.reshape(S5_KCH, gpc, S5_GROUP_CH, S5_STATE)
        return jnp.einsum('kgcp,gh->kgphc', t, eye).reshape(S5_KCH, gpc * S5_STATE, LANES)

    w_bu = jnp.concatenate([bu_blocks(bb_re), bu_blocks(bb_im)], axis=-1).astype(BF16)
    w_c = jnp.stack([c_blocks(c_re), -c_blocks(c_im)]).astype(BF16)
    return (ab_re.reshape(1, S5_LANES), ab_im.reshape(1, S5_LANES), w_bu, w_c,
            d_skip.reshape(1, S5_WIDTH), w_glu.astype(BF16))


def _ffn_kernel(x_ref, g_ref, wg_ref, wu_ref, wd_ref, o_ref, xn_ref):
    @pl.when(pl.program_id(1) == 0)
    def _():
        x = x_ref[...]
        xn_ref[...] = _rms(x, g_ref[...]).astype(BF16)
        o_ref[...] = x

    xn = xn_ref[...]
    h = jax.nn.silu(_dot(xn, wg_ref[...])) * _dot(xn, wu_ref[...])
    o_ref[...] += _dot(h.astype(BF16), wd_ref[...])


def ffn(x, g, w_gate, w_up, w_down, *, tm, tf):
    m, d = x.shape
    f = w_gate.shape[1]
    return pl.pallas_call(
        _ffn_kernel,
        grid=(m // tm, f // tf),
        in_specs=[pl.BlockSpec((tm, d), lambda i, j: (i, 0), pipeline_mode=pl.Buffered(1)),
                  pl.BlockSpec((1, d), lambda i, j: (0, 0)),
                  pl.BlockSpec((d, tf), lambda i, j: (0, j)),
                  pl.BlockSpec((d, tf), lambda i, j: (0, j)),
                  pl.BlockSpec((tf, d), lambda i, j: (j, 0))],
        out_specs=pl.BlockSpec((tm, d), lambda i, j: (i, 0)),
        out_shape=jax.ShapeDtypeStruct((m, d), F32),
        scratch_shapes=[pltpu.VMEM((tm, d), BF16)],
        compiler_params=_params("parallel", "arbitrary"),
        name="ffn",
    )(x, g.reshape(1, d), w_gate, w_up, w_down)


def _router_kernel(x_ref, g_ref, r_ref, xn_ref, gate_ref):
    xn = _rms(x_ref[...], g_ref[...])
    xn_ref[...] = xn.astype(BF16)
    x_hi, x_mid, _ = _split3(xn)
    r_hi, r_mid, _ = _split3(r_ref[...])
    logits = _dot(x_hi, r_hi) + (_dot(x_mid, r_hi) + _dot(x_hi, r_mid))
    lane = lax.broadcasted_iota(jnp.int32, logits.shape, 1)
    neg = -jnp.inf
    l1 = jnp.where(lane < N_EXPERTS, logits, neg)
    m1 = jnp.max(l1, axis=-1, keepdims=True)
    i1 = jnp.min(jnp.where(l1 == m1, lane, LANES), axis=-1, keepdims=True)
    l2 = jnp.where(lane == i1, neg, l1)
    m2 = jnp.max(l2, axis=-1, keepdims=True)
    i2 = jnp.min(jnp.where(l2 == m2, lane, LANES), axis=-1, keepdims=True)
    e = jnp.exp(m2 - m1)
    den = 1.0 + e
    gate_ref[...] = jnp.where(lane == i1, 1.0 / den, 0.0) + jnp.where(lane == i2, e / den, 0.0)


def router(x, g, r, *, tm):
    m, d = x.shape
    r_pad = jnp.zeros((d, LANES), F32).at[:, :N_EXPERTS].set(r)
    return pl.pallas_call(
        _router_kernel,
        grid=(m // tm,),
        in_specs=[pl.BlockSpec((tm, d), lambda i: (i, 0)),
                  pl.BlockSpec((1, d), lambda i: (0, 0)),
                  pl.BlockSpec((d, LANES), lambda i: (0, 0))],
        out_specs=[pl.BlockSpec((tm, d), lambda i: (i, 0)),
                   pl.BlockSpec((tm, LANES), lambda i: (i, 0))],
        out_shape=[jax.ShapeDtypeStruct((m, d), BF16),
                   jax.ShapeDtypeStruct((m, LANES), F32)],
        compiler_params=_params("parallel"),
        name="router",
    )(x, g.reshape(1, d), r_pad)


def _moe_kernel(x_ref, xn_ref, gate_ref, wg_ref, wu_ref, wd_ref, o_ref):
    e = pl.program_id(1)

    @pl.when((e == 0) & (pl.program_id(2) == 0))
    def _():
        o_ref[...] = x_ref[...]

    gate = gate_ref[...]
    lane = lax.broadcasted_iota(jnp.int32, gate.shape, 1)
    ge = jnp.sum(jnp.where(lane == e, gate, 0.0), axis=-1, keepdims=True)
    xn = xn_ref[...]
    h = jax.nn.silu(_dot(xn, wg_ref[0])) * _dot(xn, wu_ref[0])
    o_ref[...] += ge * _dot(h.astype(BF16), wd_ref[0])


def moe_dense(x, xn, gate, w_gate, w_up, w_down, *, tm, tf):
    m, d = x.shape
    ne, _, f = w_gate.shape
    return pl.pallas_call(
        _moe_kernel,
        grid=(m // tm, ne, f // tf),
        in_specs=[pl.BlockSpec((tm, d), lambda i, e, j: (i, 0), pipeline_mode=pl.Buffered(1)),
                  pl.BlockSpec((tm, d), lambda i, e, j: (i, 0)),
                  pl.BlockSpec((tm, LANES), lambda i, e, j: (i, 0)),
                  pl.BlockSpec((1, d, tf), lambda i, e, j: (e, 0, j)),
                  pl.BlockSpec((1, d, tf), lambda i, e, j: (e, 0, j)),
                  pl.BlockSpec((1, tf, d), lambda i, e, j: (e, j, 0))],
        out_specs=pl.BlockSpec((tm, d), lambda i, e, j: (i, 0)),
        out_shape=jax.ShapeDtypeStruct((m, d), F32),
        compiler_params=_params("parallel", "arbitrary", "arbitrary"),
        name="moe_dense",
    )(x, xn, gate, w_gate, w_up, w_down)


def _gates_kernel(ba_ref, alog_ref, dtb_ref, beta_ref, eg_ref, gcum_ref, *, tg):
    ba = ba_ref[...]
    b = ba[:, :DN_V_HEADS]
    a = ba[:, DN_V_HEADS:2 * DN_V_HEADS]
    beta_ref[...] = jax.nn.sigmoid(b)
    z = a + dtb_ref[...]
    softplus = jnp.maximum(z, 0.0) + jnp.log1p(jnp.exp(-jnp.abs(z)))
    g = -jnp.exp(alog_ref[...]) * softplus
    eg_ref[...] = jnp.exp(g)
    r = lax.broadcasted_iota(jnp.int32, (tg, tg), 0)
    c = lax.broadcasted_iota(jnp.int32, (tg, tg), 1)
    tri = jnp.where((c <= r) & (r // DN_CHUNK == c // DN_CHUNK), 1.0, 0.0).astype(BF16)
    hi, mid, lo = _split3(g)
    gcum_ref[...] = _dot(tri, hi) + (_dot(tri, mid) + _dot(tri, lo))


def dn_gates(ba, a_log, dt_bias, *, tg):
    m = ba.shape[0]
    out = jax.ShapeDtypeStruct((m, DN_V_HEADS), F32)
    spec = pl.BlockSpec((tg, DN_V_HEADS), lambda i: (i, 0))
    return pl.pallas_call(
        functools.partial(_gates_kernel, tg=tg),
        grid=(m // tg,),
        in_specs=[pl.BlockSpec((tg, LANES), lambda i: (i, 0)),
                  pl.BlockSpec((1, DN_V_HEADS), lambda i: (0, 0)),
                  pl.BlockSpec((1, DN_V_HEADS), lambda i: (0, 0))],
        out_specs=[spec, spec, spec],
        out_shape=[out, out, out],
        compiler_params=_params("parallel"),
        name="dn_gates",
    )(ba, a_log.reshape(1, DN_V_HEADS), dt_bias.reshape(1, DN_V_HEADS))


def _l2norm_heads(c, scale):
    parts = []
    for h in range(c.shape[1] // DN_K_DIM):
        t = c[:, h * DN_K_DIM:(h + 1) * DN_K_DIM]
        t = t * lax.rsqrt(jnp.sum(t * t, axis=-1, keepdims=True) + EPS)
        parts.append(t * scale if scale != 1.0 else t)
    return jnp.concatenate(parts, axis=1)


def _conv_prompt_kernel(x_ref, halo_ref, w_ref, o_ref, *, tl, tc):
    j = pl.program_id(2)
    halo = jnp.where(pl.program_id(1) > 0, halo_ref[0], 0.0)
    xcat = jnp.concatenate([halo, x_ref[0]], axis=0)
    conv = xcat[SUBLANES - 3:SUBLANES - 3 + tl] * w_ref[0:1, :]
    for i in range(1, DN_CONV_K):
        conv = conv + xcat[SUBLANES - 3 + i:SUBLANES - 3 + i + tl] * w_ref[i:i + 1, :]
    c = jax.nn.silu(conv)
    nq = DN_QK_W // tc

    @pl.when(j < nq)
    def _():
        o_ref[0] = _l2norm_heads(c, DN_K_DIM ** -0.5)

    @pl.when((j >= nq) & (j < 2 * nq))
    def _():
        o_ref[0] = _l2norm_heads(c, 1.0)

    @pl.when(j >= 2 * nq)
    def _():
        o_ref[0] = c


def dn_conv_prompt(proj, conv_w, *, tl, tc):
    b, l, _ = proj.shape
    hb = tl // SUBLANES
    return pl.pallas_call(
        functools.partial(_conv_prompt_kernel, tl=tl, tc=tc),
        grid=(b, l // tl, DN_CONV_DIM // tc),
        in_specs=[pl.BlockSpec((1, tl, tc), lambda i, t, j: (i, t, j)),
                  pl.BlockSpec((1, SUBLANES, tc), lambda i, t, j: (i, jnp.maximum(t * hb - 1, 0), j)),
                  pl.BlockSpec((DN_CONV_K, tc), lambda i, t, j: (0, j))],
        out_specs=pl.BlockSpec((1, tl, tc), lambda i, t, j: (i, t, j)),
        out_shape=jax.ShapeDtypeStruct((b, l, DN_CONV_DIM), F32),
        compiler_params=_params("parallel", "parallel", "arbitrary"),
        name="dn_conv_prompt",
    )(proj, proj, conv_w)


DN_GROUP = 4
DN_LBLOCK = 512


def _dn_chunk_kernel(q_ref, k_ref, v_ref, z_ref, gc_ref, beta_ref, gt_ref, onorm_ref,
                     o_ref, s_out_ref, s_ref, wq_ref, ak_ref, u_ref, oacc_ref, egl_ref, *, nc):
    hg = pl.program_id(1)
    lb = pl.program_id(2)
    cz = DN_CHUNK

    @pl.when(lb == 0)
    def _():
        s_ref[...] = jnp.zeros_like(s_ref)

    gc_all = gc_ref[0]
    beta_all = beta_ref[0]
    lane = lax.broadcasted_iota(jnp.int32, gc_all.shape, 1)
    ri = lax.broadcasted_iota(jnp.int32, (cz, cz), 0)
    ci = lax.broadcasted_iota(jnp.int32, (cz, cz), 1)
    incl = (ri >= ci)[None]
    strict = (ri > ci)[None]

    for g in range(DN_GROUP):
        kh = g // 2
        head = hg * DN_GROUP + g
        q = q_ref[0, :, kh * DN_K_DIM:(kh + 1) * DN_K_DIM].reshape(nc, cz, DN_K_DIM)
        k = k_ref[0, :, kh * DN_K_DIM:(kh + 1) * DN_K_DIM].reshape(nc, cz, DN_K_DIM)
        v = v_ref[0, :, g * DN_V_DIM:(g + 1) * DN_V_DIM].reshape(nc, cz, DN_V_DIM)
        gcol = jnp.sum(jnp.where(lane == head, gc_all, 0.0), axis=-1, keepdims=True).reshape(nc, cz, 1)
        bcol = jnp.sum(jnp.where(lane == head, beta_all, 0.0), axis=-1, keepdims=True).reshape(nc, cz, 1)
        grow = gt_ref[0, pl.ds(head, 1), :]
        grow = jnp.stack([grow[:, c * cz:(c + 1) * cz] for c in range(nc)])
        decay = jnp.where(incl, jnp.exp(gcol - grow), 0.0)
        kb = k * bcol
        kbf = k.astype(BF16)
        kk = jnp.einsum('cid,cjd->cij', kb.astype(BF16), kbf, preferred_element_type=F32)
        neg_l = jnp.where(strict, -(kk * decay), 0.0)
        n_acc = neg_l
        pw = neg_l
        for _ in range(5):
            pwb = pw.astype(BF16)
            pw = jnp.einsum('cij,cjk->cik', pwb, pwb, preferred_element_type=F32)
            n_acc = n_acc + pw + jnp.einsum('cij,cjk->cik', n_acc.astype(BF16), pw.astype(BF16),
                                            preferred_element_type=F32)
        egc = jnp.exp(gcol)
        rhs = jnp.concatenate([v * bcol, kb * egc], axis=-1)
        n_hi, n_mid, _ = _split3(n_acc)
        r_hi, r_mid, _ = _split3(rhs)
        bmm = lambda a, b: jnp.einsum('cij,cjd->cid', a, b, preferred_element_type=F32)
        sol = rhs + (bmm(n_hi, r_hi) + (bmm(n_mid, r_hi) + bmm(n_hi, r_mid)))
        u_ref[g] = sol[..., :DN_V_DIM]
        qk = jnp.einsum('cid,cjd->cij', q.astype(BF16), kbf, preferred_element_type=F32) * decay
        glast = gcol[:, cz - 1:cz, :]
        kd = k * jnp.exp(glast - gcol)
        wq_ref[g, :, 0:cz, :] = sol[..., DN_V_DIM:].astype(BF16)
        wq_ref[g, :, cz:2 * cz, :] = (q * egc).astype(BF16)
        ak_ref[g, :, 0:cz, :] = qk.astype(BF16)
        for c in range(nc):
            ak_ref[g, c, cz:cz + DN_K_DIM, :] = jnp.transpose(kd[c]).astype(BF16)
        egl_ref[g] = jnp.broadcast_to(jnp.exp(glast), (nc, 1, DN_V_DIM))

    def chunk(c, carry):
        for g in range(DN_GROUP):
            s = s_ref[g]
            ws_qs = _dot(wq_ref[g, c], s.astype(BF16))
            v_new = u_ref[g, c] - ws_qs[0:cz]
            av_kv = _dot(ak_ref[g, c], v_new.astype(BF16))
            oacc_ref[g, pl.ds(pl.multiple_of(c * cz, cz), cz), :] = ws_qs[cz:2 * cz] + av_kv[0:cz]
            s_ref[g] = s * egl_ref[g, c] + av_kv[cz:cz + DN_K_DIM]
        return carry

    lax.fori_loop(0, nc, chunk, 0)

    for g in range(DN_GROUP):
        o = oacc_ref[g]
        z = z_ref[0, :, g * DN_V_DIM:(g + 1) * DN_V_DIM]
        o_ref[0, :, g * DN_V_DIM:(g + 1) * DN_V_DIM] = (
            _rms(o, onorm_ref[...]) * jax.nn.silu(z)).astype(o_ref.dtype)

    @pl.when(lb == pl.num_programs(2) - 1)
    def _():
        s_out_ref[0] = s_ref[...]


def dn_chunked(qkvc, proj, gcum, beta, gcum_t, out_norm):
    b, l, _ = qkvc.shape
    lbk = min(DN_LBLOCK, l)
    nc = lbk // DN_CHUNK
    gk = DN_GROUP // 2 * DN_K_DIM
    gv = DN_GROUP * DN_V_DIM
    cz = DN_CHUNK
    return pl.pallas_call(
        functools.partial(_dn_chunk_kernel, nc=nc),
        grid=(b, DN_V_HEADS // DN_GROUP, l // lbk),
        in_specs=[pl.BlockSpec((1, lbk, gk), lambda i, h, t: (i, t, h)),
                  pl.BlockSpec((1, lbk, gk), lambda i, h, t: (i, t, DN_QK_W // gk + h)),
                  pl.BlockSpec((1, lbk, gv), lambda i, h, t: (i, t, 2 * DN_QK_W // gv + h)),
                  pl.BlockSpec((1, lbk, gv), lambda i, h, t: (i, t, DN_CONV_DIM // gv + h)),
                  pl.BlockSpec((1, lbk, DN_V_HEADS), lambda i, h, t: (i, t, 0)),
                  pl.BlockSpec((1, lbk, DN_V_HEADS), lambda i, h, t: (i, t, 0)),
                  pl.BlockSpec((1, DN_V_HEADS, lbk), lambda i, h, t: (i, 0, t)),
                  pl.BlockSpec((1, DN_V_DIM), lambda i, h, t: (0, 0))],
        out_specs=[pl.BlockSpec((1, lbk, gv), lambda i, h, t: (i, t, h)),
                   pl.BlockSpec((1, DN_GROUP, DN_K_DIM, DN_V_DIM), lambda i, h, t: (i, h, 0, 0))],
        out_shape=[jax.ShapeDtypeStruct((b, l, DN_V_W), BF16),
                   jax.ShapeDtypeStruct((b, DN_V_HEADS, DN_K_DIM, DN_V_DIM), F32)],
        scratch_shapes=[pltpu.VMEM((DN_GROUP, DN_K_DIM, DN_V_DIM), F32),
                        pltpu.VMEM((DN_GROUP, nc, 2 * cz, DN_K_DIM), BF16),
                        pltpu.VMEM((DN_GROUP, nc, cz + DN_K_DIM, cz), BF16),
                        pltpu.VMEM((DN_GROUP, nc, cz, DN_V_DIM), F32),
                        pltpu.VMEM((DN_GROUP, lbk, DN_V_DIM), F32),
                        pltpu.VMEM((DN_GROUP, nc, 1, DN_V_DIM), F32)],
        compiler_params=_params("parallel", "parallel", "arbitrary"),
        name="dn_chunked",
    )(qkvc, qkvc, qkvc, proj, gcum, beta, gcum_t, out_norm.reshape(1, DN_V_DIM))


DN_ROWS = DN_CONV_DIM // LANES
DN_QROWS = DN_K_HEADS
DN_VROW0 = 2 * DN_K_HEADS


def _dn_sample_kernel(x_ref, buf_ref, w_ref, z_ref, eg_ref, beta_ref, s_ref, onorm_ref,
                      o_ref, s_out_ref, buf_out_ref):
    b = pl.program_id(0)
    x = x_ref[0]
    buf = buf_ref[0]
    conv = buf[0] * w_ref[0]
    for i in range(1, DN_CONV_K - 1):
        conv = conv + buf[i] * w_ref[i]
    conv = conv + x * w_ref[DN_CONV_K - 1]
    buf_out_ref[0, 0:DN_CONV_K - 2] = buf[1:DN_CONV_K - 1]
    buf_out_ref[0, DN_CONV_K - 2] = x
    c = jax.nn.silu(conv)
    qk = c[0:DN_VROW0]
    qk = qk * lax.rsqrt(jnp.sum(qk * qk, axis=-1, keepdims=True) + EPS)
    q_t = jnp.transpose(qk[0:DN_QROWS] * (DN_K_DIM ** -0.5))
    k_t = jnp.transpose(qk[DN_QROWS:DN_VROW0])
    for h in range(DN_V_HEADS):
        kh = h // (DN_V_HEADS // DN_K_HEADS)
        kcol = k_t[:, kh:kh + 1]
        qcol = q_t[:, kh:kh + 1]
        v = c[DN_VROW0 + h:DN_VROW0 + h + 1]
        s = s_ref[0, h] * eg_ref[b, h]
        delta = (v - jnp.sum(kcol * s, axis=0, keepdims=True)) * beta_ref[b, h]
        s = s + kcol * delta
        s_out_ref[0, h] = s
        o = jnp.sum(qcol * s, axis=0, keepdims=True)
        o_ref[0, h:h + 1, :] = (_rms(o, onorm_ref[...]) * jax.nn.silu(z_ref[0, h:h + 1, :])
                                ).astype(o_ref.dtype)


def dn_sample(qkv_rows, conv_buf, conv_w, z_rows, eg, beta, state, out_norm):
    nb = qkv_rows.shape[0]
    smem = pl.BlockSpec(memory_space=pltpu.SMEM)
    return pl.pallas_call(
        _dn_sample_kernel,
        grid=(nb,),
        in_specs=[pl.BlockSpec((1, DN_ROWS, LANES), lambda i: (i, 0, 0)),
                  pl.BlockSpec((1, DN_CONV_K - 1, DN_ROWS, LANES), lambda i: (i, 0, 0, 0)),
                  pl.BlockSpec((DN_CONV_K, DN_ROWS, LANES), lambda i: (0, 0, 0)),
                  pl.BlockSpec((1, DN_V_HEADS, DN_V_DIM), lambda i: (i, 0, 0)),
                  smem, smem,
                  pl.BlockSpec((1, DN_V_HEADS, DN_K_DIM, DN_V_DIM), lambda i: (i, 0, 0, 0)),
                  pl.BlockSpec((1, DN_V_DIM), lambda i: (0, 0))],
        out_specs=[pl.BlockSpec((1, DN_V_HEADS, DN_V_DIM), lambda i: (i, 0, 0)),
                   pl.BlockSpec((1, DN_V_HEADS, DN_K_DIM, DN_V_DIM), lambda i: (i, 0, 0, 0)),
                   pl.BlockSpec((1, DN_CONV_K - 1, DN_ROWS, LANES), lambda i: (i, 0, 0, 0))],
        out_shape=[jax.ShapeDtypeStruct((nb, DN_V_HEADS, DN_V_DIM), BF16),
                   jax.ShapeDtypeStruct(state.shape, F32),
                   jax.ShapeDtypeStruct(conv_buf.shape, F32)],
        compiler_params=_params("arbitrary"),
        name="dn_sample",
    )(qkv_rows, conv_buf, conv_w, z_rows, eg, beta, state, out_norm.reshape(1, DN_V_DIM))


def _tile(m, cap):
    return min(m, cap)


def _even_layer(x, cache, w, *, batch, seq):
    m = x.shape[0]
    tm = _tile(m, 1024)
    qkv = norm_matmul(x, w['norm_mix'], w['w_in_qkv'], tm=tm, tn=512)
    u = norm_matmul(x, w['norm_mix'], w['w_in_u'], tm=tm, tn=512)
    if cache is None:
        attn, new_k, new_v = attn_prompt(qkv, w['q_norm'], w['k_norm'], w['sinks'], batch=batch, seq=seq)
        zeros = jnp.zeros((batch, S5_LANES), F32)
        ssm, h_re, h_im = s5_mix(u.reshape(batch, seq, S5_WIDTH), zeros, zeros, w['s5'],
                                 rows=batch, seq=seq, steps=64)
        ssm = ssm.reshape(m, S5_WIDTH)
    else:
        k_win, v_win, h0_re, h0_im = cache
        attn, new_k, new_v = attn_sample(qkv, k_win.reshape(batch, WINDOW, KV_WIDTH),
                                         v_win.reshape(batch, WINDOW, KV_WIDTH),
                                         w['q_norm'], w['k_norm'], w['sinks'])
        ssm, h_re, h_im = s5_mix(u, h0_re.reshape(batch, S5_LANES), h0_im.reshape(batch, S5_LANES),
                                 w['s5'], rows=batch, seq=1, steps=1)
    x = matmul_residual(x, [attn, ssm], [w['w_out_a'], w['w_out_b']], tm=tm, tn=512)
    x = ffn(x, w['norm_ffn'], w['ffn_gate'], w['ffn_up'], w['ffn_down'], tm=tm, tf=512)
    shp = (batch, WINDOW, N_KV_HEADS, HEAD_DIM)
    st = (batch, S5_GROUPS, S5_STATE)
    return x, new_k.reshape(shp), new_v.reshape(shp), h_re.reshape(st), h_im.reshape(st)


def _odd_layer(x, cache, w, *, batch, seq):
    m = x.shape[0]
    tm = _tile(m, 1024)
    proj = norm_matmul(x, w['norm_mix'], w['w_in_main'], tm=tm, tn=1024)
    ba = norm_matmul(x, w['norm_mix'], w['w_in_ba'], tm=tm, tn=LANES)
    beta, eg, gcum = dn_gates(ba, w['a_log'], w['dt_bias'], tg=_tile(m, 512))
    if cache is None:
        proj3 = proj.reshape(batch, seq, DN_CONV_DIM + DN_V_W)
        qkvc = dn_conv_prompt(proj3, w['conv_w'], tl=256, tc=1024)
        gcum3 = gcum.reshape(batch, seq, DN_V_HEADS)
        o, s_new = dn_chunked(qkvc, proj3, gcum3, beta.reshape(batch, seq, DN_V_HEADS),
                              jnp.swapaxes(gcum3, 1, 2), w['out_norm'])
        o = o.reshape(m, DN_V_W)
        new_buf = proj3[:, seq - (DN_CONV_K - 1):, :DN_CONV_DIM]
    else:
        s0, conv_buf = cache
        o, s_new, new_buf = dn_sample(
            proj[:, :DN_CONV_DIM].reshape(batch, DN_ROWS, LANES),
            conv_buf.reshape(batch, DN_CONV_K - 1, DN_ROWS, LANES),
            w['conv_w'].reshape(DN_CONV_K, DN_ROWS, LANES),
            proj[:, DN_CONV_DIM:].reshape(batch, DN_V_HEADS, DN_V_DIM),
            eg, beta, s0, w['out_norm'])
        o = o.reshape(m, DN_V_W)
        new_buf = new_buf.reshape(batch, DN_CONV_K - 1, DN_CONV_DIM)
    x = matmul_residual(x, [o], [w['w_out']], tm=tm, tn=512)
    xn, gate = router(x, w['norm_ffn'], w['router'], tm=_tile(m, 512))
    x = moe_dense(x, xn, gate, w['exp_gate'], w['exp_up'], w['exp_down'], tm=tm, tf=512)
    return x, s_new, new_buf


def kernel(x_prompt, x_sample, cache_win_k, cache_win_v, state_s5_re, state_s5_im, state_dn, state_dn_conv,
           e_norm_mix, e_w_in, e_q_norm, e_k_norm, e_sinks,
           e_s5_a_re, e_s5_a_im, e_s5_log_dt, e_s5_b_re, e_s5_b_im, e_s5_c_re, e_s5_c_im, e_s5_d, e_s5_w_glu,
           e_w_out, e_norm_ffn, e_ffn_w_gate, e_ffn_w_up, e_ffn_w_down,
           o_norm_mix, o_w_in, o_conv_w, o_a_log, o_dt_bias, o_out_norm, o_w_out, o_norm_ffn,
           o_router, o_exp_w_gate, o_exp_w_up, o_exp_w_down):
    bp, lp, d = x_prompt.shape
    bs, ls, _ = x_sample.shape
    assert ls == 1, "the sample group advances one token per step"
    hp = x_prompt.reshape(bp * lp, d)
    hs = x_sample.reshape(bs * ls, d)
    qkv_w = ATTN_WIDTH + 2 * KV_WIDTH
    main_w = DN_CONV_DIM + DN_V_W

    j = 0
    we = dict(
        norm_mix=e_norm_mix[j], q_norm=e_q_norm[j], k_norm=e_k_norm[j], sinks=e_sinks[j],
        w_in_qkv=e_w_in[j, :, :qkv_w].astype(BF16), w_in_u=e_w_in[j, :, qkv_w:].astype(BF16),
        s5=s5_constants(e_s5_a_re[j], e_s5_a_im[j], e_s5_log_dt[j], e_s5_b_re[j], e_s5_b_im[j],
                        e_s5_c_re[j], e_s5_c_im[j], e_s5_d[j], e_s5_w_glu[j]),
        w_out_a=e_w_out[j, :ATTN_WIDTH].astype(BF16), w_out_b=e_w_out[j, ATTN_WIDTH:].astype(BF16),
        norm_ffn=e_norm_ffn[j], ffn_gate=e_ffn_w_gate[j].astype(BF16),
        ffn_up=e_ffn_w_up[j].astype(BF16), ffn_down=e_ffn_w_down[j].astype(BF16))
    ba_pad = jnp.zeros((d, LANES), F32).at[:, :2 * DN_V_HEADS].set(o_w_in[j, :, main_w:])
    wo = dict(
        norm_mix=o_norm_mix[j], w_in_main=o_w_in[j, :, :main_w].astype(BF16), w_in_ba=ba_pad.astype(BF16),
        conv_w=o_conv_w[j], a_log=o_a_log[j], dt_bias=o_dt_bias[j], out_norm=o_out_norm[j],
        w_out=o_w_out[j].astype(BF16), norm_ffn=o_norm_ffn[j], router=o_router[j],
        exp_gate=o_exp_w_gate[j].astype(BF16), exp_up=o_exp_w_up[j].astype(BF16),
        exp_down=o_exp_w_down[j].astype(BF16))

    hp, kp, vp, rp, ip = _even_layer(hp, None, we, batch=bp, seq=lp)
    hs, ks, vs, rs, is_ = _even_layer(
        hs, (cache_win_k[j], cache_win_v[j], state_s5_re[j], state_s5_im[j]), we, batch=bs, seq=1)
    hp, sp, cp = _odd_layer(hp, None, wo, batch=bp, seq=lp)
    hs, ss, cs = _odd_layer(hs, (state_dn[j], state_dn_conv[j]), wo, batch=bs, seq=1)

    one = lambda t: t[None]
    return (hp.reshape(bp, lp, d), hs.reshape(bs, ls, d),
            one(kp), one(vp), one(rp), one(ip), one(sp), one(cp),
            one(ks), one(vs), one(rs), one(is_), one(ss), one(cs))
```

```python
import functools

import jax
import jax.numpy as jnp
from jax import lax
from jax.experimental import pallas as pl
from jax.experimental.pallas import tpu as pltpu

F32 = jnp.float32
BF16 = jnp.bfloat16

D_MODEL = 2048
N_HEADS = 16
N_KV_HEADS = 4
HEAD_DIM = 64
Q_PER_KV = N_HEADS // N_KV_HEADS
WINDOW = 128
ATTN_WIDTH = N_HEADS * HEAD_DIM
KV_WIDTH = N_KV_HEADS * HEAD_DIM
ATTN_SCALE = HEAD_DIM ** -0.5
S5_WIDTH = D_MODEL // 2
S5_GROUP_CH = 16
S5_GROUPS = S5_WIDTH // S5_GROUP_CH
S5_STATE = 64
S5_LANES = S5_GROUPS * S5_STATE
DN_K_HEADS = 16
DN_V_HEADS = 32
DN_K_DIM = 128
DN_V_DIM = 128
DN_CONV_K = 4
DN_CHUNK = 64
DN_QK_W = DN_K_HEADS * DN_K_DIM
DN_V_W = DN_V_HEADS * DN_V_DIM
DN_CONV_DIM = 2 * DN_QK_W + DN_V_W
D_FF = 5632
N_EXPERTS = 8
EPS = 1e-6
NEG_INF = -1e30

LANES = 128
SUBLANES = 8
VMEM_LIMIT = 56 * 1024 * 1024


def _params(*sem):
    return pltpu.CompilerParams(dimension_semantics=sem, vmem_limit_bytes=VMEM_LIMIT)


def _rms(x, g):
    return x * lax.rsqrt(jnp.mean(x * x, axis=-1, keepdims=True) + EPS) * g


def _dot(a, b):
    return jnp.dot(a, b, preferred_element_type=F32)


def _dot_nt(a, b):
    return lax.dot_general(a, b, (((1,), (1,)), ((), ())), preferred_element_type=F32)


def _split3(x):
    hi = x.astype(BF16)
    r = x - hi.astype(F32)
    mid = r.astype(BF16)
    lo = (r - mid.astype(F32)).astype(BF16)
    return hi, mid, lo


def _norm_matmul_kernel(x_ref, g_ref, w_ref, o_ref, xn_ref):
    @pl.when(pl.program_id(1) == 0)
    def _():
        xn_ref[...] = _rms(x_ref[...], g_ref[...]).astype(BF16)

    o_ref[...] = _dot(xn_ref[...], w_ref[...]).astype(o_ref.dtype)


def norm_matmul(x, g, w, *, tm, tn, out_dtype=F32):
    m, d = x.shape
    n = w.shape[1]
    return pl.pallas_call(
        _norm_matmul_kernel,
        grid=(m // tm, n // tn),
        in_specs=[pl.BlockSpec((tm, d), lambda i, j: (i, 0)),
                  pl.BlockSpec((1, d), lambda i, j: (0, 0)),
                  pl.BlockSpec((d, tn), lambda i, j: (0, j))],
        out_specs=pl.BlockSpec((tm, tn), lambda i, j: (i, j)),
        out_shape=jax.ShapeDtypeStruct((m, n), out_dtype),
        scratch_shapes=[pltpu.VMEM((tm, d), BF16)],
        compiler_params=_params("parallel", "arbitrary"),
        name="norm_matmul",
    )(x, g.reshape(1, d), w)


def _matmul_residual_kernel(*refs, n_pairs):
    x_ref = refs[0]
    o_ref = refs[-1]
    acc = x_ref[...]
    for a_ref, w_ref in zip(refs[1:1 + n_pairs], refs[1 + n_pairs:1 + 2 * n_pairs]):
        acc = acc + _dot(a_ref[...], w_ref[...])
    o_ref[...] = acc


def matmul_residual(x, a_list, w_list, *, tm, tn):
    m, n = x.shape
    in_specs = [pl.BlockSpec((tm, tn), lambda i, j: (i, j))]
    in_specs += [pl.BlockSpec((tm, a.shape[1]), lambda i, j: (i, 0)) for a in a_list]
    in_specs += [pl.BlockSpec((w.shape[0], tn), lambda i, j: (0, j)) for w in w_list]
    return pl.pallas_call(
        functools.partial(_matmul_residual_kernel, n_pairs=len(a_list)),
        grid=(m // tm, n // tn),
        in_specs=in_specs,
        out_specs=pl.BlockSpec((tm, tn), lambda i, j: (i, j)),
        out_shape=jax.ShapeDtypeStruct((m, n), F32),
        compiler_params=_params("parallel", "arbitrary"),
        name="matmul_residual",
    )(x, *a_list, *w_list)


def _attn_prompt_kernel(q_ref, kc_ref, kp_ref, vc_ref, vp_ref, qn_ref, kn_ref, sink_ref,
                        o_ref, wk_ref, wv_ref):
    has_prev = pl.program_id(1) > 0
    qg = qn_ref[...]
    kg = kn_ref[...]
    row = lax.broadcasted_iota(jnp.int32, (WINDOW, 2 * WINDOW), 0)
    col = lax.broadcasted_iota(jnp.int32, (WINDOW, 2 * WINDOW), 1)
    rel = (WINDOW + row) - col
    mask = (rel >= 0) & (rel <= WINDOW) & ((col >= WINDOW) | has_prev)
    for kvh in range(N_KV_HEADS):
        sl = slice(kvh * HEAD_DIM, (kvh + 1) * HEAD_DIM)
        kc = _rms(kc_ref[:, sl], kg)
        kp = _rms(kp_ref[:, sl], kg)
        vc = vc_ref[:, sl]
        wk_ref[0, :, sl] = kc
        wv_ref[0, :, sl] = vc
        kband = jnp.concatenate([kp, kc], axis=0).astype(BF16)
        vband = jnp.concatenate([vp_ref[:, sl], vc], axis=0).astype(BF16)
        for g in range(Q_PER_KV):
            h = kvh * Q_PER_KV + g
            hs = slice(h * HEAD_DIM, (h + 1) * HEAD_DIM)
            q = _rms(q_ref[:, hs], qg).astype(BF16)
            s = _dot_nt(q, kband) * ATTN_SCALE
            s = jnp.where(mask, s, NEG_INF)
            sk = sink_ref[h:h + 1, :]
            m = jnp.maximum(jnp.max(s, axis=-1, keepdims=True), sk)
            p = jnp.exp(s - m)
            p = p / (jnp.sum(p, axis=-1, keepdims=True) + jnp.exp(sk - m))
            o_ref[:, hs] = _dot(p.astype(BF16), vband).astype(o_ref.dtype)


def attn_prompt(qkv, q_norm, k_norm, sinks, *, batch, seq):
    nb = seq // WINDOW
    kcol = ATTN_WIDTH // KV_WIDTH
    cur = lambda c: (lambda b, n: (b * nb + n, c))
    prev = lambda c: (lambda b, n: (jnp.maximum(b * nb + n - 1, 0), c))
    return pl.pallas_call(
        _attn_prompt_kernel,
        grid=(batch, nb),
        in_specs=[pl.BlockSpec((WINDOW, ATTN_WIDTH), cur(0)),
                  pl.BlockSpec((WINDOW, KV_WIDTH), cur(kcol)),
                  pl.BlockSpec((WINDOW, KV_WIDTH), prev(kcol)),
                  pl.BlockSpec((WINDOW, KV_WIDTH), cur(kcol + 1)),
                  pl.BlockSpec((WINDOW, KV_WIDTH), prev(kcol + 1)),
                  pl.BlockSpec((1, HEAD_DIM), lambda b, n: (0, 0)),
                  pl.BlockSpec((1, HEAD_DIM), lambda b, n: (0, 0)),
                  pl.BlockSpec((N_HEADS, 1), lambda b, n: (0, 0))],
        out_specs=[pl.BlockSpec((WINDOW, ATTN_WIDTH), lambda b, n: (b * nb + n, 0)),
                   pl.BlockSpec((1, WINDOW, KV_WIDTH), lambda b, n: (b, 0, 0)),
                   pl.BlockSpec((1, WINDOW, KV_WIDTH), lambda b, n: (b, 0, 0))],
        out_shape=[jax.ShapeDtypeStruct((batch * seq, ATTN_WIDTH), BF16),
                   jax.ShapeDtypeStruct((batch, WINDOW, KV_WIDTH), F32),
                   jax.ShapeDtypeStruct((batch, WINDOW, KV_WIDTH), F32)],
        compiler_params=_params("parallel", "arbitrary"),
        name="attn_prompt",
    )(qkv, qkv, qkv, qkv, qkv, q_norm.reshape(1, HEAD_DIM), k_norm.reshape(1, HEAD_DIM),
      sinks.reshape(N_HEADS, 1))


def _attn_sample_kernel(qkv_ref, ck_ref, cv_ref, qn_ref, kn_ref, sink_ref,
                        o_ref, wk_ref, wv_ref, *, bt):
    qg = qn_ref[...]
    kg = kn_ref[...]
    for b in range(bt):
        wk_ref[b, 0:WINDOW - 1, :] = ck_ref[b, 1:WINDOW, :]
        wv_ref[b, 0:WINDOW - 1, :] = cv_ref[b, 1:WINDOW, :]
        wv_ref[b, WINDOW - 1:WINDOW, :] = qkv_ref[b:b + 1, ATTN_WIDTH + KV_WIDTH:ATTN_WIDTH + 2 * KV_WIDTH]
        for kvh in range(N_KV_HEADS):
            sl = slice(kvh * HEAD_DIM, (kvh + 1) * HEAD_DIM)
            kn = _rms(qkv_ref[b:b + 1, ATTN_WIDTH + kvh * HEAD_DIM:ATTN_WIDTH + (kvh + 1) * HEAD_DIM], kg)
            vn = qkv_ref[b:b + 1, ATTN_WIDTH + KV_WIDTH + kvh * HEAD_DIM:
                         ATTN_WIDTH + KV_WIDTH + (kvh + 1) * HEAD_DIM]
            wk_ref[b, WINDOW - 1:WINDOW, sl] = kn
            q4 = jnp.concatenate(
                [qkv_ref[b:b + 1, (kvh * Q_PER_KV + g) * HEAD_DIM:(kvh * Q_PER_KV + g + 1) * HEAD_DIM]
                 for g in range(Q_PER_KV)], axis=0)
            q4 = _rms(q4, qg)
            kwin = ck_ref[b, :, sl].astype(BF16)
            vwin = cv_ref[b, :, sl].astype(BF16)
            s_c = _dot_nt(q4.astype(BF16), kwin) * ATTN_SCALE
            s_n = jnp.sum(q4 * kn, axis=-1, keepdims=True) * ATTN_SCALE
            sk = sink_ref[kvh * Q_PER_KV:(kvh + 1) * Q_PER_KV, :]
            m = jnp.maximum(jnp.maximum(jnp.max(s_c, axis=-1, keepdims=True), s_n), sk)
            p_c = jnp.exp(s_c - m)
            p_n = jnp.exp(s_n - m)
            den = jnp.sum(p_c, axis=-1, keepdims=True) + p_n + jnp.exp(sk - m)
            o4 = _dot((p_c / den).astype(BF16), vwin) + (p_n / den) * vn
            for g in range(Q_PER_KV):
                h = kvh * Q_PER_KV + g
                o_ref[b:b + 1, h * HEAD_DIM:(h + 1) * HEAD_DIM] = o4[g:g + 1, :].astype(o_ref.dtype)


def attn_sample(qkv, cache_k, cache_v, q_norm, k_norm, sinks, *, bt=8):
    nb = qkv.shape[0]
    width = qkv.shape[1]
    win = pl.BlockSpec((bt, WINDOW, KV_WIDTH), lambda i: (i, 0, 0))
    return pl.pallas_call(
        functools.partial(_attn_sample_kernel, bt=bt),
        grid=(nb // bt,),
        in_specs=[pl.BlockSpec((bt, width), lambda i: (i, 0)), win, win,
                  pl.BlockSpec((1, HEAD_DIM), lambda i: (0, 0)),
                  pl.BlockSpec((1, HEAD_DIM), lambda i: (0, 0)),
                  pl.BlockSpec((N_HEADS, 1), lambda i: (0, 0))],
        out_specs=[pl.BlockSpec((bt, ATTN_WIDTH), lambda i: (i, 0)), win, win],
        out_shape=[jax.ShapeDtypeStruct((nb, ATTN_WIDTH), BF16),
                   jax.ShapeDtypeStruct((nb, WINDOW, KV_WIDTH), F32),
                   jax.ShapeDtypeStruct((nb, WINDOW, KV_WIDTH), F32)],
        compiler_params=_params("parallel"),
        name="attn_sample",
    )(qkv, cache_k, cache_v, q_norm.reshape(1, HEAD_DIM), k_norm.reshape(1, HEAD_DIM),
      sinks.reshape(N_HEADS, 1))


S5_KCH = S5_WIDTH // LANES
S5_CHUNK_STATES = S5_LANES // S5_KCH
S5_SCAN_TILES = 4


def _s5_kernel(u_ref, h0re_ref, h0im_ref, are_ref, aim_ref, wbu_ref, wc_ref, d_ref, wglu_ref,
               y_ref, hre_ref, him_ref, hs_ref, st_ref, *, rows, steps):
    rt = rows * steps
    u = u_ref[...].reshape(rt, S5_WIDTH)

    @pl.when(pl.program_id(0) == 0)
    def _():
        st_ref[:, :S5_LANES] = h0re_ref[...]
        st_ref[:, S5_LANES:] = h0im_ref[...]

    ub = u.astype(BF16)
    nre = S5_LANES // LANES
    tpk = S5_CHUNK_STATES // LANES
    for k in range(S5_KCH):
        r = _dot(ub[:, k * LANES:(k + 1) * LANES], wbu_ref[k])
        for a in range(tpk):
            hs_ref[k * tpk + a] = r[:, a * LANES:(a + 1) * LANES]
            hs_ref[nre + k * tpk + a] = r[:, (tpk + a) * LANES:(tpk + a + 1) * LANES]

    nt = S5_SCAN_TILES
    for j in range(nre // nt):
        tiles = range(j * nt, (j + 1) * nt)
        ar = [jnp.broadcast_to(are_ref[:, a * LANES:(a + 1) * LANES], (SUBLANES, LANES)) for a in tiles]
        ai = [jnp.broadcast_to(aim_ref[:, a * LANES:(a + 1) * LANES], (SUBLANES, LANES)) for a in tiles]

        def advance(r8, hr, hi):
            nr, ni = [], []
            for n, a in enumerate(tiles):
                nr.append(ar[n] * hr[n] - ai[n] * hi[n] + hs_ref[a, r8, :])
                ni.append(ar[n] * hi[n] + ai[n] * hr[n] + hs_ref[nre + a, r8, :])
                hs_ref[a, r8, :] = nr[n]
                hs_ref[nre + a, r8, :] = ni[n]
            return nr, ni

        if steps == 1:
            def group(rg, carry):
                r8 = pl.ds(pl.multiple_of(rg * SUBLANES, SUBLANES), SUBLANES)
                hr = [st_ref[r8, a * LANES:(a + 1) * LANES] for a in tiles]
                hi = [st_ref[r8, S5_LANES + a * LANES:S5_LANES + (a + 1) * LANES] for a in tiles]
                nr, ni = advance(r8, hr, hi)
                for n, a in enumerate(tiles):
                    st_ref[r8, a * LANES:(a + 1) * LANES] = nr[n]
                    st_ref[r8, S5_LANES + a * LANES:S5_LANES + (a + 1) * LANES] = ni[n]
                return carry
            lax.fori_loop(0, rows // SUBLANES, group, 0)
        else:
            def step(t, carry):
                nr, ni = advance(pl.ds(t, SUBLANES, stride=steps), *carry)
                return tuple(nr), tuple(ni)
            hr0 = tuple(st_ref[:, a * LANES:(a + 1) * LANES] for a in tiles)
            hi0 = tuple(st_ref[:, S5_LANES + a * LANES:S5_LANES + (a + 1) * LANES] for a in tiles)
            hr, hi = lax.fori_loop(0, steps, step, (hr0, hi0))
            for n, a in enumerate(tiles):
                st_ref[:, a * LANES:(a + 1) * LANES] = hr[n]
                st_ref[:, S5_LANES + a * LANES:S5_LANES + (a + 1) * LANES] = hi[n]

    ys = []
    for k in range(S5_KCH):
        hre = jnp.concatenate([hs_ref[k * tpk + a] for a in range(tpk)], axis=1).astype(BF16)
        him = jnp.concatenate([hs_ref[nre + k * tpk + a] for a in range(tpk)], axis=1).astype(BF16)
        yk = _dot(hre, wc_ref[0, k]) + _dot(him, wc_ref[1, k])
        ys.append(yk + d_ref[:, k * LANES:(k + 1) * LANES] * u[:, k * LANES:(k + 1) * LANES])
    y = jax.nn.gelu(jnp.concatenate(ys, axis=1))
    out = y * jax.nn.sigmoid(_dot(y.astype(BF16), wglu_ref[...]))
    y_ref[...] = out.reshape(y_ref.shape).astype(y_ref.dtype)

    @pl.when(pl.program_id(0) == pl.num_programs(0) - 1)
    def _():
        hre_ref[...] = st_ref[:, :S5_LANES]
        him_ref[...] = st_ref[:, S5_LANES:]


def s5_mix(u, h0_re, h0_im, consts, *, rows, seq, steps):
    a_re, a_im, w_bu, w_c, d_skip, w_glu = consts
    if seq > 1:
        u_spec = pl.BlockSpec((rows, steps, S5_WIDTH), lambda c: (0, c, 0))
    else:
        u_spec = pl.BlockSpec((rows, S5_WIDTH), lambda c: (0, 0))
    full = lambda shape: pl.BlockSpec(shape, lambda c: (0,) * len(shape))
    return pl.pallas_call(
        functools.partial(_s5_kernel, rows=rows, steps=steps),
        grid=(seq // steps,),
        in_specs=[u_spec, full((rows, S5_LANES)), full((rows, S5_LANES)),
                  full((1, S5_LANES)), full((1, S5_LANES)),
                  full(w_bu.shape), full(w_c.shape), full((1, S5_WIDTH)), full(w_glu.shape)],
        out_specs=[u_spec, full((rows, S5_LANES)), full((rows, S5_LANES))],
        out_shape=[jax.ShapeDtypeStruct(u.shape, BF16),
                   jax.ShapeDtypeStruct((rows, S5_LANES), F32),
                   jax.ShapeDtypeStruct((rows, S5_LANES), F32)],
        scratch_shapes=[pltpu.VMEM((2 * S5_LANES // LANES, rows * steps, LANES), F32),
                        pltpu.VMEM((rows, 2 * S5_LANES), F32)],
        compiler_params=_params("arbitrary"),
        name="s5_mix",
    )(u, h0_re, h0_im, a_re, a_im, w_bu, w_c, d_skip, w_glu)


def s5_constants(a_re, a_im, log_dt, b_re, b_im, c_re, c_im, d_skip, w_glu):
    lr, li = a_re, a_im
    dt = jnp.exp(log_dt)[:, None]
    mag = jnp.exp(lr * dt)
    ab_re, ab_im = mag * jnp.cos(li * dt), mag * jnp.sin(li * dt)
    den = lr * lr + li * li
    f_re = ((ab_re - 1.0) * lr + ab_im * li) / den
    f_im = (ab_im * lr - (ab_re - 1.0) * li) / den
    bb_re = f_re[..., None] * b_re - f_im[..., None] * b_im
    bb_im = f_re[..., None] * b_im + f_im[..., None] * b_re
    gpc = LANES // S5_GROUP_CH
    eye = jnp.eye(gpc, dtype=F32)

    def bu_blocks(bb):
        t = bb.reshape(S5_KCH, gpc, S5_STATE, S5_GROUP_CH)
        return jnp.einsum('kgpc,gh->kgchp', t, eye).reshape(S5_KCH, LANES, gpc * S5_STATE)

    def c_blocks(c):
        t = c.reshape(S5_KCH, gpc, S5_GROUP_CH, S5_STATE)
        return jnp.einsum('kgcp,gh->kgphc', t, eye).reshape(S5_KCH, gpc * S5_STATE, LANES)

    w_bu = jnp.concatenate([bu_blocks(bb_re), bu_blocks(bb_im)], axis=-1).astype(BF16)
    w_c = jnp.stack([c_blocks(c_re), -c_blocks(c_im)]).astype(BF16)
    return (ab_re.reshape(1, S5_LANES), ab_im.reshape(1, S5_LANES), w_bu, w_c,
            d_skip.reshape(1, S5_WIDTH), w_glu.astype(BF16))


def _ffn_kernel(x_ref, g_ref, wg_ref, wu_ref, wd_ref, o_ref, xn_ref):
    @pl.when(pl.program_id(1) == 0)
    def _():
        x = x_ref[...]
        xn_ref[...] = _rms(x, g_ref[...]).astype(BF16)
        o_ref[...] = x

    xn = xn_ref[...]
    h = jax.nn.silu(_dot(xn, wg_ref[...])) * _dot(xn, wu_ref[...])
    o_ref[...] += _dot(h.astype(BF16), wd_ref[...])


def ffn(x, g, w_gate, w_up, w_down, *, tm, tf):
    m, d = x.shape
    f = w_gate.shape[1]
    return pl.pallas_call(
        _ffn_kernel,
        grid=(m // tm, f // tf),
        in_specs=[pl.BlockSpec((tm, d), lambda i, j: (i, 0), pipeline_mode=pl.Buffered(1)),
                  pl.BlockSpec((1, d), lambda i, j: (0, 0)),
                  pl.BlockSpec((d, tf), lambda i, j: (0, j)),
                  pl.BlockSpec((d, tf), lambda i, j: (0, j)),
                  pl.BlockSpec((tf, d), lambda i, j: (j, 0))],
        out_specs=pl.BlockSpec((tm, d), lambda i, j: (i, 0)),
        out_shape=jax.ShapeDtypeStruct((m, d), F32),
        scratch_shapes=[pltpu.VMEM((tm, d), BF16)],
        compiler_params=_params("parallel", "arbitrary"),
        name="ffn",
    )(x, g.reshape(1, d), w_gate, w_up, w_down)


MOE_TILE = 512
META_I1, META_I2, META_W1, META_W2, META_R1, META_R2 = range(6)


def _router_kernel(x_ref, g_ref, r_ref, xn_ref, meta_ref, cnt_ref, carry_ref):
    @pl.when(pl.program_id(0) == 0)
    def _():
        carry_ref[...] = jnp.zeros_like(carry_ref)

    xn = _rms(x_ref[...], g_ref[...])
    xn_ref[...] = xn
    x_hi, x_mid, _ = _split3(xn)
    r_hi, r_mid, _ = _split3(r_ref[...])
    logits = _dot(x_hi, r_hi) + (_dot(x_mid, r_hi) + _dot(x_hi, r_mid))
    lane = lax.broadcasted_iota(jnp.int32, logits.shape, 1)
    neg = -jnp.inf
    l1 = jnp.where(lane < N_EXPERTS, logits, neg)
    m1 = jnp.max(l1, axis=-1, keepdims=True)
    i1 = jnp.min(jnp.where(l1 == m1, lane, LANES), axis=-1, keepdims=True)
    l2 = jnp.where(lane == i1, neg, l1)
    m2 = jnp.max(l2, axis=-1, keepdims=True)
    i2 = jnp.min(jnp.where(l2 == m2, lane, LANES), axis=-1, keepdims=True)
    e = jnp.exp(m2 - m1)
    den = 1.0 + e
    sel = jnp.where(lane == i1, 1.0, jnp.where(lane == i2, 1.0, 0.0))
    tm = sel.shape[0]
    row = lax.broadcasted_iota(jnp.int32, (tm, tm), 0)
    col = lax.broadcasted_iota(jnp.int32, (tm, tm), 1)
    incl = _dot(jnp.where(col <= row, 1.0, 0.0).astype(BF16), sel.astype(BF16))
    excl = incl - sel + carry_ref[0:1, :]
    rank1 = jnp.sum(jnp.where(lane == i1, excl, 0.0), axis=-1, keepdims=True)
    rank2 = jnp.sum(jnp.where(lane == i2, excl, 0.0), axis=-1, keepdims=True)
    carry_ref[...] = carry_ref[...] + incl[tm - 1:tm, :]
    cnt_ref[...] = carry_ref[...]
    fields = {META_I1: i1.astype(F32), META_I2: i2.astype(F32), META_W1: 1.0 / den, META_W2: e / den,
              META_R1: rank1, META_R2: rank2}
    meta = jnp.zeros(logits.shape, F32)
    for k, val in fields.items():
        meta = jnp.where(lane == k, val, meta)
    meta_ref[...] = meta


def router(x, g, r, *, tm):
    m, d = x.shape
    r_pad = jnp.zeros((d, LANES), F32).at[:, :N_EXPERTS].set(r)
    return pl.pallas_call(
        _router_kernel,
        grid=(m // tm,),
        in_specs=[pl.BlockSpec((tm, d), lambda i: (i, 0)),
                  pl.BlockSpec((1, d), lambda i: (0, 0)),
                  pl.BlockSpec((d, LANES), lambda i: (0, 0))],
        out_specs=[pl.BlockSpec((tm, d), lambda i: (i, 0)),
                   pl.BlockSpec((tm, LANES), lambda i: (i, 0)),
                   pl.BlockSpec((SUBLANES, LANES), lambda i: (0, 0))],
        out_shape=[jax.ShapeDtypeStruct((m, d), F32),
                   jax.ShapeDtypeStruct((m, LANES), F32),
                   jax.ShapeDtypeStruct((SUBLANES, LANES), F32)],
        scratch_shapes=[pltpu.VMEM((SUBLANES, LANES), F32)],
        compiler_params=_params("arbitrary"),
        name="router",
    )(x, g.reshape(1, d), r_pad)


def _dispatch_kernel(p1_ref, p2_ref, xn_ref, zeros_ref, xs_ref, sem, *, td):
    del zeros_ref
    base = pl.program_id(0) * td

    def copies(r):
        src = xn_ref.at[pl.ds(r, 1), :]
        return [pltpu.make_async_copy(src, xs_ref.at[pl.ds(p_ref[base + r], 1), :], sem)
                for p_ref in (p1_ref, p2_ref)]

    def issue(r, carry):
        for c in copies(r):
            c.start()
        return carry

    def drain(r, carry):
        for c in copies(r):
            c.wait()
        return carry

    lax.fori_loop(0, td, issue, 0)
    lax.fori_loop(0, td, drain, 0)


def moe_dispatch(xn, pos1, pos2, *, rows, td):
    m, d = xn.shape
    return pl.pallas_call(
        functools.partial(_dispatch_kernel, td=td),
        grid_spec=pltpu.PrefetchScalarGridSpec(
            num_scalar_prefetch=2,
            grid=(m // td,),
            in_specs=[pl.BlockSpec((td, d), lambda i, p1, p2: (i, 0)),
                      pl.BlockSpec(memory_space=pl.ANY)],
            out_specs=pl.BlockSpec(memory_space=pl.ANY),
            scratch_shapes=[pltpu.SemaphoreType.DMA(())]),
        out_shape=jax.ShapeDtypeStruct((rows, d), F32),
        input_output_aliases={3: 0},
        compiler_params=_params("arbitrary"),
        name="moe_dispatch",
    )(pos1, pos2, xn, jnp.zeros((rows, d), F32))


def _expert_kernel(te_ref, tv_ref, xs_ref, wg_ref, wu_ref, wd_ref, ys_ref, xb_ref):
    del te_ref
    f = pl.program_id(1)
    valid = tv_ref[pl.program_id(0)]

    @pl.when(valid > 0)
    def _():
        @pl.when(f == 0)
        def _():
            xb_ref[...] = xs_ref[...].astype(BF16)

        xb = xb_ref[...]
        h = jax.nn.silu(_dot(xb, wg_ref[0])) * _dot(xb, wu_ref[0])
        y = _dot(h.astype(BF16), wd_ref[0])

        @pl.when(f == 0)
        def _():
            ys_ref[...] = y

        @pl.when(f > 0)
        def _():
            ys_ref[...] += y

    @pl.when((valid == 0) & (f == 0))
    def _():
        ys_ref[...] = jnp.zeros_like(ys_ref)


def moe_experts(xs, tile_expert, tile_valid, w_gate, w_up, w_down, *, tf):
    rows, d = xs.shape
    _, _, f = w_gate.shape
    nf = f // tf
    fidx = lambda t, j, tv: jnp.where(tv[t] > 0, j, nf - 1)
    return pl.pallas_call(
        _expert_kernel,
        grid_spec=pltpu.PrefetchScalarGridSpec(
            num_scalar_prefetch=2,
            grid=(rows // MOE_TILE, nf),
            in_specs=[pl.BlockSpec((MOE_TILE, d), lambda t, j, te, tv: (t, 0)),
                      pl.BlockSpec((1, d, tf), lambda t, j, te, tv: (te[t], 0, fidx(t, j, tv))),
                      pl.BlockSpec((1, d, tf), lambda t, j, te, tv: (te[t], 0, fidx(t, j, tv))),
                      pl.BlockSpec((1, tf, d), lambda t, j, te, tv: (te[t], fidx(t, j, tv), 0))],
            out_specs=pl.BlockSpec((MOE_TILE, d), lambda t, j, te, tv: (t, 0)),
            scratch_shapes=[pltpu.VMEM((MOE_TILE, d), BF16)]),
        out_shape=jax.ShapeDtypeStruct((rows, d), F32),
        compiler_params=_params("arbitrary", "arbitrary"),
        name="moe_experts",
    )(tile_expert, tile_valid, xs, w_gate, w_up, w_down)


def _combine_kernel(p1_ref, p2_ref, x_ref, meta_ref, ys_ref, o_ref, buf_ref, sem, *, tc):
    i = pl.program_id(0)
    slot = i % 2

    def copies(step, sl, r):
        tok = step * tc + r
        return [pltpu.make_async_copy(ys_ref.at[pl.ds(p_ref[tok], 1), :],
                                      buf_ref.at[sl, k, pl.ds(r, 1), :], sem.at[sl])
                for k, p_ref in enumerate((p1_ref, p2_ref))]

    def issue(step, sl):
        def body(r, carry):
            for c in copies(step, sl, r):
                c.start()
            return carry
        lax.fori_loop(0, tc, body, 0)

    @pl.when(i == 0)
    def _():
        issue(0, 0)

    @pl.when(i + 1 < pl.num_programs(0))
    def _():
        issue(i + 1, 1 - slot)

    def drain(r, carry):
        for c in copies(i, slot, r):
            c.wait()
        return carry
    lax.fori_loop(0, tc, drain, 0)

    meta = meta_ref[...]
    w1 = meta[:, META_W1:META_W1 + 1]
    w2 = meta[:, META_W2:META_W2 + 1]
    o_ref[...] = x_ref[...] + w1 * buf_ref[slot, 0] + w2 * buf_ref[slot, 1]


def moe_combine(x, meta, ys, pos1, pos2, *, tc):
    m, d = x.shape
    return pl.pallas_call(
        functools.partial(_combine_kernel, tc=tc),
        grid_spec=pltpu.PrefetchScalarGridSpec(
            num_scalar_prefetch=2,
            grid=(m // tc,),
            in_specs=[pl.BlockSpec((tc, d), lambda i, p1, p2: (i, 0)),
                      pl.BlockSpec((tc, LANES), lambda i, p1, p2: (i, 0)),
                      pl.BlockSpec(memory_space=pl.ANY)],
            out_specs=pl.BlockSpec((tc, d), lambda i, p1, p2: (i, 0)),
            scratch_shapes=[pltpu.VMEM((2, 2, tc, d), F32), pltpu.SemaphoreType.DMA((2,))]),
        out_shape=jax.ShapeDtypeStruct((m, d), F32),
        compiler_params=_params("arbitrary"),
        name="moe_combine",
    )(pos1, pos2, x, meta, ys)


def moe_sparse(x, g, r, w_gate, w_up, w_down):
    m, d = x.shape
    tm = min(m, 512)
    tt = min(m, 256)
    xn, meta, counts = router(x, g, r, tm=tm)
    cnt = counts[0, :N_EXPERTS].astype(jnp.int32)
    tiles_e = (cnt + MOE_TILE - 1) // MOE_TILE
    tile_end = jnp.cumsum(tiles_e)
    tile_start = tile_end - tiles_e
    n_tiles = (2 * m) // MOE_TILE + N_EXPERTS
    t_all = jnp.arange(n_tiles, dtype=jnp.int32)
    t = jnp.minimum(t_all, tile_end[-1] - 1)
    tile_expert = jnp.sum(t[:, None] >= tile_end[None, :], axis=1).astype(jnp.int32)
    tile_valid = jnp.clip(cnt[tile_expert] - (t - tile_start[tile_expert]) * MOE_TILE, 0, MOE_TILE)
    tile_valid = jnp.where(t_all < tile_end[-1], tile_valid, 0).astype(jnp.int32)
    off = tile_start * MOE_TILE
    pos1 = off[meta[:, META_I1].astype(jnp.int32)] + meta[:, META_R1].astype(jnp.int32)
    pos2 = off[meta[:, META_I2].astype(jnp.int32)] + meta[:, META_R2].astype(jnp.int32)
    xs = moe_dispatch(xn, pos1, pos2, rows=n_tiles * MOE_TILE, td=tt)
    ys = moe_experts(xs, tile_expert, tile_valid, w_gate, w_up, w_down, tf=512)
    return moe_combine(x, meta, ys, pos1, pos2, tc=tt)


def _gates_kernel(ba_ref, alog_ref, dtb_ref, beta_ref, eg_ref, gcum_ref, *, tg):
    ba = ba_ref[...]
    b = ba[:, :DN_V_HEADS]
    a = ba[:, DN_V_HEADS:2 * DN_V_HEADS]
    beta_ref[...] = jax.nn.sigmoid(b)
    z = a + dtb_ref[...]
    softplus = jnp.maximum(z, 0.0) + jnp.log1p(jnp.exp(-jnp.abs(z)))
    g = -jnp.exp(alog_ref[...]) * softplus
    eg_ref[...] = jnp.exp(g)
    r = lax.broadcasted_iota(jnp.int32, (tg, tg), 0)
    c = lax.broadcasted_iota(jnp.int32, (tg, tg), 1)
    tri = jnp.where((c <= r) & (r // DN_CHUNK == c // DN_CHUNK), 1.0, 0.0).astype(BF16)
    hi, mid, lo = _split3(g)
    gcum_ref[...] = _dot(tri, hi) + (_dot(tri, mid) + _dot(tri, lo))


def dn_gates(ba, a_log, dt_bias, *, tg):
    m = ba.shape[0]
    out = jax.ShapeDtypeStruct((m, DN_V_HEADS), F32)
    spec = pl.BlockSpec((tg, DN_V_HEADS), lambda i: (i, 0))
    return pl.pallas_call(
        functools.partial(_gates_kernel, tg=tg),
        grid=(m // tg,),
        in_specs=[pl.BlockSpec((tg, LANES), lambda i: (i, 0)),
                  pl.BlockSpec((1, DN_V_HEADS), lambda i: (0, 0)),
                  pl.BlockSpec((1, DN_V_HEADS), lambda i: (0, 0))],
        out_specs=[spec, spec, spec],
        out_shape=[out, out, out],
        compiler_params=_params("parallel"),
        name="dn_gates",
    )(ba, a_log.reshape(1, DN_V_HEADS), dt_bias.reshape(1, DN_V_HEADS))


def _l2norm_heads(c, scale):
    parts = []
    for h in range(c.shape[1] // DN_K_DIM):
        t = c[:, h * DN_K_DIM:(h + 1) * DN_K_DIM]
        t = t * lax.rsqrt(jnp.sum(t * t, axis=-1, keepdims=True) + EPS)
        parts.append(t * scale if scale != 1.0 else t)
    return jnp.concatenate(parts, axis=1)


def _conv_prompt_kernel(x_ref, halo_ref, w_ref, o_ref, *, tl, tc):
    j = pl.program_id(2)
    halo = jnp.where(pl.program_id(1) > 0, halo_ref[0], 0.0)
    xcat = jnp.concatenate([halo, x_ref[0]], axis=0)
    conv = xcat[SUBLANES - 3:SUBLANES - 3 + tl] * w_ref[0:1, :]
    for i in range(1, DN_CONV_K):
        conv = conv + xcat[SUBLANES - 3 + i:SUBLANES - 3 + i + tl] * w_ref[i:i + 1, :]
    c = jax.nn.silu(conv)
    nq = DN_QK_W // tc

    @pl.when(j < nq)
    def _():
        o_ref[0] = _l2norm_heads(c, DN_K_DIM ** -0.5)

    @pl.when((j >= nq) & (j < 2 * nq))
    def _():
        o_ref[0] = _l2norm_heads(c, 1.0)

    @pl.when(j >= 2 * nq)
    def _():
        o_ref[0] = c


def dn_conv_prompt(proj, conv_w, *, tl, tc):
    b, l, _ = proj.shape
    hb = tl // SUBLANES
    return pl.pallas_call(
        functools.partial(_conv_prompt_kernel, tl=tl, tc=tc),
        grid=(b, l // tl, DN_CONV_DIM // tc),
        in_specs=[pl.BlockSpec((1, tl, tc), lambda i, t, j: (i, t, j)),
                  pl.BlockSpec((1, SUBLANES, tc), lambda i, t, j: (i, jnp.maximum(t * hb - 1, 0), j)),
                  pl.BlockSpec((DN_CONV_K, tc), lambda i, t, j: (0, j))],
        out_specs=pl.BlockSpec((1, tl, tc), lambda i, t, j: (i, t, j)),
        out_shape=jax.ShapeDtypeStruct((b, l, DN_CONV_DIM), F32),
        compiler_params=_params("parallel", "parallel", "arbitrary"),
        name="dn_conv_prompt",
    )(proj, proj, conv_w)


DN_GROUP = 4
DN_LBLOCK = 512


def _dn_chunk_kernel(q_ref, k_ref, v_ref, z_ref, gc_ref, beta_ref, gt_ref, onorm_ref,
                     o_ref, s_out_ref, s_ref, wq_ref, ak_ref, u_ref, oacc_ref, egl_ref, *, nc):
    hg = pl.program_id(1)
    lb = pl.program_id(2)
    cz = DN_CHUNK

    @pl.when(lb == 0)
    def _():
        s_ref[...] = jnp.zeros_like(s_ref)

    gc_all = gc_ref[0]
    beta_all = beta_ref[0]
    lane = lax.broadcasted_iota(jnp.int32, gc_all.shape, 1)
    ri = lax.broadcasted_iota(jnp.int32, (cz, cz), 0)
    ci = lax.broadcasted_iota(jnp.int32, (cz, cz), 1)
    incl = (ri >= ci)[None]
    strict = (ri > ci)[None]

    for g in range(DN_GROUP):
        kh = g // 2
        head = hg * DN_GROUP + g
        q = q_ref[0, :, kh * DN_K_DIM:(kh + 1) * DN_K_DIM].reshape(nc, cz, DN_K_DIM)
        k = k_ref[0, :, kh * DN_K_DIM:(kh + 1) * DN_K_DIM].reshape(nc, cz, DN_K_DIM)
        v = v_ref[0, :, g * DN_V_DIM:(g + 1) * DN_V_DIM].reshape(nc, cz, DN_V_DIM)
        gcol = jnp.sum(jnp.where(lane == head, gc_all, 0.0), axis=-1, keepdims=True).reshape(nc, cz, 1)
        bcol = jnp.sum(jnp.where(lane == head, beta_all, 0.0), axis=-1, keepdims=True).reshape(nc, cz, 1)
        grow = gt_ref[0, pl.ds(head, 1), :]
        grow = jnp.stack([grow[:, c * cz:(c + 1) * cz] for c in range(nc)])
        decay = jnp.where(incl, jnp.exp(gcol - grow), 0.0)
        kb = k * bcol
        kbf = k.astype(BF16)
        kk = jnp.einsum('cid,cjd->cij', kb.astype(BF16), kbf, preferred_element_type=F32)
        neg_l = jnp.where(strict, -(kk * decay), 0.0)
        n_acc = neg_l
        pw = neg_l
        for _ in range(5):
            pwb = pw.astype(BF16)
            pw = jnp.einsum('cij,cjk->cik', pwb, pwb, preferred_element_type=F32)
            n_acc = n_acc + pw + jnp.einsum('cij,cjk->cik', n_acc.astype(BF16), pw.astype(BF16),
                                            preferred_element_type=F32)
        egc = jnp.exp(gcol)
        rhs = jnp.concatenate([v * bcol, kb * egc], axis=-1)
        n_hi, n_mid, _ = _split3(n_acc)
        r_hi, r_mid, _ = _split3(rhs)
        bmm = lambda a, b: jnp.einsum('cij,cjd->cid', a, b, preferred_element_type=F32)
        sol = rhs + (bmm(n_hi, r_hi) + (bmm(n_mid, r_hi) + bmm(n_hi, r_mid)))
        u_ref[g] = sol[..., :DN_V_DIM]
        qk = jnp.einsum('cid,cjd->cij', q.astype(BF16), kbf, preferred_element_type=F32) * decay
        glast = gcol[:, cz - 1:cz, :]
        kd = k * jnp.exp(glast - gcol)
        wq_ref[g, :, 0:cz, :] = sol[..., DN_V_DIM:].astype(BF16)
        wq_ref[g, :, cz:2 * cz, :] = (q * egc).astype(BF16)
        ak_ref[g, :, 0:cz, :] = qk.astype(BF16)
        for c in range(nc):
            ak_ref[g, c, cz:cz + DN_K_DIM, :] = jnp.transpose(kd[c]).astype(BF16)
        egl_ref[g] = jnp.broadcast_to(jnp.exp(glast), (nc, 1, DN_V_DIM))

    def chunk(c, carry):
        for g in range(DN_GROUP):
            s = s_ref[g]
            ws_qs = _dot(wq_ref[g, c], s.astype(BF16))
            v_new = u_ref[g, c] - ws_qs[0:cz]
            av_kv = _dot(ak_ref[g, c], v_new.astype(BF16))
            oacc_ref[g, pl.ds(pl.multiple_of(c * cz, cz), cz), :] = ws_qs[cz:2 * cz] + av_kv[0:cz]
            s_ref[g] = s * egl_ref[g, c] + av_kv[cz:cz + DN_K_DIM]
        return carry

    lax.fori_loop(0, nc, chunk, 0)

    for g in range(DN_GROUP):
        o = oacc_ref[g]
        z = z_ref[0, :, g * DN_V_DIM:(g + 1) * DN_V_DIM]
        o_ref[0, :, g * DN_V_DIM:(g + 1) * DN_V_DIM] = (
            _rms(o, onorm_ref[...]) * jax.nn.silu(z)).astype(o_ref.dtype)

    @pl.when(lb == pl.num_programs(2) - 1)
    def _():
        s_out_ref[0] = s_ref[...]


def dn_chunked(qkvc, proj, gcum, beta, gcum_t, out_norm):
    b, l, _ = qkvc.shape
    lbk = min(DN_LBLOCK, l)
    nc = lbk // DN_CHUNK
    gk = DN_GROUP // 2 * DN_K_DIM
    gv = DN_GROUP * DN_V_DIM
    cz = DN_CHUNK
    return pl.pallas_call(
        functools.partial(_dn_chunk_kernel, nc=nc),
        grid=(b, DN_V_HEADS // DN_GROUP, l // lbk),
        in_specs=[pl.BlockSpec((1, lbk, gk), lambda i, h, t: (i, t, h)),
                  pl.BlockSpec((1, lbk, gk), lambda i, h, t: (i, t, DN_QK_W // gk + h)),
                  pl.BlockSpec((1, lbk, gv), lambda i, h, t: (i, t, 2 * DN_QK_W // gv + h)),
                  pl.BlockSpec((1, lbk, gv), lambda i, h, t: (i, t, DN_CONV_DIM // gv + h)),
                  pl.BlockSpec((1, lbk, DN_V_HEADS), lambda i, h, t: (i, t, 0)),
                  pl.BlockSpec((1, lbk, DN_V_HEADS), lambda i, h, t: (i, t, 0)),
                  pl.BlockSpec((1, DN_V_HEADS, lbk), lambda i, h, t: (i, 0, t)),
                  pl.BlockSpec((1, DN_V_DIM), lambda i, h, t: (0, 0))],
        out_specs=[pl.BlockSpec((1, lbk, gv), lambda i, h, t: (i, t, h)),
                   pl.BlockSpec((1, DN_GROUP, DN_K_DIM, DN_V_DIM), lambda i, h, t: (i, h, 0, 0))],
        out_shape=[jax.ShapeDtypeStruct((b, l, DN_V_W), BF16),
                   jax.ShapeDtypeStruct((b, DN_V_HEADS, DN_K_DIM, DN_V_DIM), F32)],
        scratch_shapes=[pltpu.VMEM((DN_GROUP, DN_K_DIM, DN_V_DIM), F32),
                        pltpu.VMEM((DN_GROUP, nc, 2 * cz, DN_K_DIM), BF16),
                        pltpu.VMEM((DN_GROUP, nc, cz + DN_K_DIM, cz), BF16),
                        pltpu.VMEM((DN_GROUP, nc, cz, DN_V_DIM), F32),
                        pltpu.VMEM((DN_GROUP, lbk, DN_V_DIM), F32),
                        pltpu.VMEM((DN_GROUP, nc, 1, DN_V_DIM), F32)],
        compiler_params=_params("parallel", "parallel", "arbitrary"),
        name="dn_chunked",
    )(qkvc, qkvc, qkvc, proj, gcum, beta, gcum_t, out_norm.reshape(1, DN_V_DIM))


DN_ROWS = DN_CONV_DIM // LANES
DN_QROWS = DN_K_HEADS
DN_VROW0 = 2 * DN_K_HEADS


def _dn_sample_kernel(x_ref, buf_ref, w_ref, z_ref, eg_ref, beta_ref, s_ref, onorm_ref,
                      o_ref, s_out_ref, buf_out_ref):
    b = pl.program_id(0)
    x = x_ref[0]
    buf = buf_ref[0]
    conv = buf[0] * w_ref[0]
    for i in range(1, DN_CONV_K - 1):
        conv = conv + buf[i] * w_ref[i]
    conv = conv + x * w_ref[DN_CONV_K - 1]
    buf_out_ref[0, 0:DN_CONV_K - 2] = buf[1:DN_CONV_K - 1]
    buf_out_ref[0, DN_CONV_K - 2] = x
    c = jax.nn.silu(conv)
    qk = c[0:DN_VROW0]
    qk = qk * lax.rsqrt(jnp.sum(qk * qk, axis=-1, keepdims=True) + EPS)
    q_t = jnp.transpose(qk[0:DN_QROWS] * (DN_K_DIM ** -0.5))
    k_t = jnp.transpose(qk[DN_QROWS:DN_VROW0])
    for h in range(DN_V_HEADS):
        kh = h // (DN_V_HEADS // DN_K_HEADS)
        kcol = k_t[:, kh:kh + 1]
        qcol = q_t[:, kh:kh + 1]
        v = c[DN_VROW0 + h:DN_VROW0 + h + 1]
        s = s_ref[0, h] * eg_ref[b, h]
        delta = (v - jnp.sum(kcol * s, axis=0, keepdims=True)) * beta_ref[b, h]
        s = s + kcol * delta
        s_out_ref[0, h] = s
        o = jnp.sum(qcol * s, axis=0, keepdims=True)
        o_ref[0, h:h + 1, :] = (_rms(o, onorm_ref[...]) * jax.nn.silu(z_ref[0, h:h + 1, :])
                                ).astype(o_ref.dtype)


def dn_sample(qkv_rows, conv_buf, conv_w, z_rows, eg, beta, state, out_norm):
    nb = qkv_rows.shape[0]
    smem = pl.BlockSpec(memory_space=pltpu.SMEM)
    return pl.pallas_call(
        _dn_sample_kernel,
        grid=(nb,),
        in_specs=[pl.BlockSpec((1, DN_ROWS, LANES), lambda i: (i, 0, 0)),
                  pl.BlockSpec((1, DN_CONV_K - 1, DN_ROWS, LANES), lambda i: (i, 0, 0, 0)),
                  pl.BlockSpec((DN_CONV_K, DN_ROWS, LANES), lambda i: (0, 0, 0)),
                  pl.BlockSpec((1, DN_V_HEADS, DN_V_DIM), lambda i: (i, 0, 0)),
                  smem, smem,
                  pl.BlockSpec((1, DN_V_HEADS, DN_K_DIM, DN_V_DIM), lambda i: (i, 0, 0, 0)),
                  pl.BlockSpec((1, DN_V_DIM), lambda i: (0, 0))],
        out_specs=[pl.BlockSpec((1, DN_V_HEADS, DN_V_DIM), lambda i: (i, 0, 0)),
                   pl.BlockSpec((1, DN_V_HEADS, DN_K_DIM, DN_V_DIM), lambda i: (i, 0, 0, 0)),
                   pl.BlockSpec((1, DN_CONV_K - 1, DN_ROWS, LANES), lambda i: (i, 0, 0, 0))],
        out_shape=[jax.ShapeDtypeStruct((nb, DN_V_HEADS, DN_V_DIM), BF16),
                   jax.ShapeDtypeStruct(state.shape, F32),
                   jax.ShapeDtypeStruct(conv_buf.shape, F32)],
        compiler_params=_params("arbitrary"),
        name="dn_sample",
    )(qkv_rows, conv_buf, conv_w, z_rows, eg, beta, state, out_norm.reshape(1, DN_V_DIM))


def _tile(m, cap):
    return min(m, cap)


def _even_layer(x, cache, w, *, batch, seq):
    m = x.shape[0]
    tm = _tile(m, 1024)
    qkv = norm_matmul(x, w['norm_mix'], w['w_in_qkv'], tm=tm, tn=512)
    u = norm_matmul(x, w['norm_mix'], w['w_in_u'], tm=tm, tn=512)
    if cache is None:
        attn, new_k, new_v = attn_prompt(qkv, w['q_norm'], w['k_norm'], w['sinks'], batch=batch, seq=seq)
        zeros = jnp.zeros((batch, S5_LANES), F32)
        ssm, h_re, h_im = s5_mix(u.reshape(batch, seq, S5_WIDTH), zeros, zeros, w['s5'],
                                 rows=batch, seq=seq, steps=64)
        ssm = ssm.reshape(m, S5_WIDTH)
    else:
        k_win, v_win, h0_re, h0_im = cache
        attn, new_k, new_v = attn_sample(qkv, k_win.reshape(batch, WINDOW, KV_WIDTH),
                                         v_win.reshape(batch, WINDOW, KV_WIDTH),
                                         w['q_norm'], w['k_norm'], w['sinks'])
        ssm, h_re, h_im = s5_mix(u, h0_re.reshape(batch, S5_LANES), h0_im.reshape(batch, S5_LANES),
                                 w['s5'], rows=batch, seq=1, steps=1)
    x = matmul_residual(x, [attn, ssm], [w['w_out_a'], w['w_out_b']], tm=tm, tn=512)
    x = ffn(x, w['norm_ffn'], w['ffn_gate'], w['ffn_up'], w['ffn_down'], tm=tm, tf=512)
    shp = (batch, WINDOW, N_KV_HEADS, HEAD_DIM)
    st = (batch, S5_GROUPS, S5_STATE)
    return x, new_k.reshape(shp), new_v.reshape(shp), h_re.reshape(st), h_im.reshape(st)


def _odd_layer(x, cache, w, *, batch, seq):
    m = x.shape[0]
    tm = _tile(m, 1024)
    proj = norm_matmul(x, w['norm_mix'], w['w_in_main'], tm=tm, tn=1024)
    ba = norm_matmul(x, w['norm_mix'], w['w_in_ba'], tm=tm, tn=LANES)
    beta, eg, gcum = dn_gates(ba, w['a_log'], w['dt_bias'], tg=_tile(m, 512))
    if cache is None:
        proj3 = proj.reshape(batch, seq, DN_CONV_DIM + DN_V_W)
        qkvc = dn_conv_prompt(proj3, w['conv_w'], tl=256, tc=1024)
        gcum3 = gcum.reshape(batch, seq, DN_V_HEADS)
        o, s_new = dn_chunked(qkvc, proj3, gcum3, beta.reshape(batch, seq, DN_V_HEADS),
                              jnp.swapaxes(gcum3, 1, 2), w['out_norm'])
        o = o.reshape(m, DN_V_W)
        new_buf = proj3[:, seq - (DN_CONV_K - 1):, :DN_CONV_DIM]
    else:
        s0, conv_buf = cache
        o, s_new, new_buf = dn_sample(
            proj[:, :DN_CONV_DIM].reshape(batch, DN_ROWS, LANES),
            conv_buf.reshape(batch, DN_CONV_K - 1, DN_ROWS, LANES),
            w['conv_w'].reshape(DN_CONV_K, DN_ROWS, LANES),
            proj[:, DN_CONV_DIM:].reshape(batch, DN_V_HEADS, DN_V_DIM),
            eg, beta, s0, w['out_norm'])
        o = o.reshape(m, DN_V_W)
        new_buf = new_buf.reshape(batch, DN_CONV_K - 1, DN_CONV_DIM)
    x = matmul_residual(x, [o], [w['w_out']], tm=tm, tn=512)
    x = moe_sparse(x, w['norm_ffn'], w['router'], w['exp_gate'], w['exp_up'], w['exp_down'])
    return x, s_new, new_buf


def kernel(x_prompt, x_sample, cache_win_k, cache_win_v, state_s5_re, state_s5_im, state_dn, state_dn_conv,
           e_norm_mix, e_w_in, e_q_norm, e_k_norm, e_sinks,
           e_s5_a_re, e_s5_a_im, e_s5_log_dt, e_s5_b_re, e_s5_b_im, e_s5_c_re, e_s5_c_im, e_s5_d, e_s5_w_glu,
           e_w_out, e_norm_ffn, e_ffn_w_gate, e_ffn_w_up, e_ffn_w_down,
           o_norm_mix, o_w_in, o_conv_w, o_a_log, o_dt_bias, o_out_norm, o_w_out, o_norm_ffn,
           o_router, o_exp_w_gate, o_exp_w_up, o_exp_w_down):
    bp, lp, d = x_prompt.shape
    bs, ls, _ = x_sample.shape
    assert ls == 1, "the sample group advances one token per step"
    hp = x_prompt.reshape(bp * lp, d)
    hs = x_sample.reshape(bs * ls, d)
    qkv_w = ATTN_WIDTH + 2 * KV_WIDTH
    main_w = DN_CONV_DIM + DN_V_W

    j = 0
    we = dict(
        norm_mix=e_norm_mix[j], q_norm=e_q_norm[j], k_norm=e_k_norm[j], sinks=e_sinks[j],
        w_in_qkv=e_w_in[j, :, :qkv_w].astype(BF16), w_in_u=e_w_in[j, :, qkv_w:].astype(BF16),
        s5=s5_constants(e_s5_a_re[j], e_s5_a_im[j], e_s5_log_dt[j], e_s5_b_re[j], e_s5_b_im[j],
                        e_s5_c_re[j], e_s5_c_im[j], e_s5_d[j], e_s5_w_glu[j]),
        w_out_a=e_w_out[j, :ATTN_WIDTH].astype(BF16), w_out_b=e_w_out[j, ATTN_WIDTH:].astype(BF16),
        norm_ffn=e_norm_ffn[j], ffn_gate=e_ffn_w_gate[j].astype(BF16),
        ffn_up=e_ffn_w_up[j].astype(BF16), ffn_down=e_ffn_w_down[j].astype(BF16))
    ba_pad = jnp.zeros((d, LANES), F32).at[:, :2 * DN_V_HEADS].set(o_w_in[j, :, main_w:])
    wo = dict(
        norm_mix=o_norm_mix[j], w_in_main=o_w_in[j, :, :main_w].astype(BF16), w_in_ba=ba_pad.astype(BF16),
        conv_w=o_conv_w[j], a_log=o_a_log[j], dt_bias=o_dt_bias[j], out_norm=o_out_norm[j],
        w_out=o_w_out[j].astype(BF16), norm_ffn=o_norm_ffn[j], router=o_router[j],
        exp_gate=o_exp_w_gate[j].astype(BF16), exp_up=o_exp_w_up[j].astype(BF16),
        exp_down=o_exp_w_down[j].astype(BF16))

    hp, kp, vp, rp, ip = _even_layer(hp, None, we, batch=bp, seq=lp)
    hs, ks, vs, rs, is_ = _even_layer(
        hs, (cache_win_k[j], cache_win_v[j], state_s5_re[j], state_s5_im[j]), we, batch=bs, seq=1)
    hp, sp, cp = _odd_layer(hp, None, wo, batch=bp, seq=lp)
    hs, ss, cs = _odd_layer(hs, (state_dn[j], state_dn_conv[j]), wo, batch=bs, seq=1)

    one = lambda t: t[None]
    return (hp.reshape(bp, lp, d), hs.reshape(bs, ls, d),
            one(kp), one(vp), one(rp), one(ip), one(sp), one(cp),
            one(ks), one(vs), one(rs), one(is_), one(ss), one(cs))
```

```python
import functools

import jax
import jax.numpy as jnp
from jax import lax
from jax.experimental import pallas as pl
from jax.experimental.pallas import tpu as pltpu

F32 = jnp.float32
BF16 = jnp.bfloat16

D_MODEL = 2048
N_HEADS = 16
N_KV_HEADS = 4
HEAD_DIM = 64
Q_PER_KV = N_HEADS // N_KV_HEADS
WINDOW = 128
ATTN_WIDTH = N_HEADS * HEAD_DIM
KV_WIDTH = N_KV_HEADS * HEAD_DIM
ATTN_SCALE = HEAD_DIM ** -0.5
S5_WIDTH = D_MODEL // 2
S5_GROUP_CH = 16
S5_GROUPS = S5_WIDTH // S5_GROUP_CH
S5_STATE = 64
S5_LANES = S5_GROUPS * S5_STATE
DN_K_HEADS = 16
DN_V_HEADS = 32
DN_K_DIM = 128
DN_V_DIM = 128
DN_CONV_K = 4
DN_CHUNK = 64
DN_QK_W = DN_K_HEADS * DN_K_DIM
DN_V_W = DN_V_HEADS * DN_V_DIM
DN_CONV_DIM = 2 * DN_QK_W + DN_V_W
D_FF = 5632
N_EXPERTS = 8
EPS = 1e-6
NEG_INF = -1e30

LANES = 128
SUBLANES = 8
VMEM_LIMIT = 56 * 1024 * 1024


def _params(*sem):
    return pltpu.CompilerParams(dimension_semantics=sem, vmem_limit_bytes=VMEM_LIMIT)


def _rms(x, g):
    return x * lax.rsqrt(jnp.mean(x * x, axis=-1, keepdims=True) + EPS) * g


def _dot(a, b):
    return jnp.dot(a, b, preferred_element_type=F32)


def _dot_nt(a, b):
    return lax.dot_general(a, b, (((1,), (1,)), ((), ())), preferred_element_type=F32)


def _split3(x):
    hi = x.astype(BF16)
    r = x - hi.astype(F32)
    mid = r.astype(BF16)
    lo = (r - mid.astype(F32)).astype(BF16)
    return hi, mid, lo


def _norm_matmul_kernel(x_ref, g_ref, w_ref, o_ref, xn_ref):
    @pl.when(pl.program_id(1) == 0)
    def _():
        xn_ref[...] = _rms(x_ref[...], g_ref[...]).astype(BF16)

    o_ref[...] = _dot(xn_ref[...], w_ref[...]).astype(o_ref.dtype)


def norm_matmul(x, g, w, *, tm, tn, out_dtype=F32):
    m, d = x.shape
    n = w.shape[1]
    return pl.pallas_call(
        _norm_matmul_kernel,
        grid=(m // tm, n // tn),
        in_specs=[pl.BlockSpec((tm, d), lambda i, j: (i, 0)),
                  pl.BlockSpec((1, d), lambda i, j: (0, 0)),
                  pl.BlockSpec((d, tn), lambda i, j: (0, j))],
        out_specs=pl.BlockSpec((tm, tn), lambda i, j: (i, j)),
        out_shape=jax.ShapeDtypeStruct((m, n), out_dtype),
        scratch_shapes=[pltpu.VMEM((tm, d), BF16)],
        compiler_params=_params("parallel", "arbitrary"),
        name="norm_matmul",
    )(x, g.reshape(1, d), w)


def _matmul_residual_kernel(*refs, n_pairs):
    x_ref = refs[0]
    o_ref = refs[-1]
    acc = x_ref[...]
    for a_ref, w_ref in zip(refs[1:1 + n_pairs], refs[1 + n_pairs:1 + 2 * n_pairs]):
        acc = acc + _dot(a_ref[...], w_ref[...])
    o_ref[...] = acc


def matmul_residual(x, a_list, w_list, *, tm, tn):
    m, n = x.shape
    in_specs = [pl.BlockSpec((tm, tn), lambda i, j: (i, j))]
    in_specs += [pl.BlockSpec((tm, a.shape[1]), lambda i, j: (i, 0)) for a in a_list]
    in_specs += [pl.BlockSpec((w.shape[0], tn), lambda i, j: (0, j)) for w in w_list]
    return pl.pallas_call(
        functools.partial(_matmul_residual_kernel, n_pairs=len(a_list)),
        grid=(m // tm, n // tn),
        in_specs=in_specs,
        out_specs=pl.BlockSpec((tm, tn), lambda i, j: (i, j)),
        out_shape=jax.ShapeDtypeStruct((m, n), F32),
        compiler_params=_params("parallel", "arbitrary"),
        name="matmul_residual",
    )(x, *a_list, *w_list)


HEADS_PER_TILE = LANES // HEAD_DIM


def _head_rms(x, seg, g):
    hi, mid, lo = _split3(x * x)
    parts = []
    for a in range(x.shape[1] // LANES):
        sl = slice(a * LANES, (a + 1) * LANES)
        parts.append(_dot(hi[:, sl], seg) + (_dot(mid[:, sl], seg) + _dot(lo[:, sl], seg)))
    ms = jnp.concatenate(parts, axis=1) * (1.0 / HEAD_DIM)
    return x * lax.rsqrt(ms + EPS) * g


def _attn_prompt_kernel(q_ref, kc_ref, kp_ref, vc_ref, vp_ref, qn_ref, kn_ref, sink_ref,
                        o_ref, wk_ref, wv_ref, s_ref, p_ref):
    has_prev = pl.program_id(1) > 0
    r = lax.broadcasted_iota(jnp.int32, (LANES, LANES), 0)
    c = lax.broadcasted_iota(jnp.int32, (LANES, LANES), 1)
    seg = jnp.where(r // HEAD_DIM == c // HEAD_DIM, 1.0, 0.0).astype(BF16)
    low = lax.broadcasted_iota(jnp.int32, (1, LANES), 1) < HEAD_DIM

    qn = _head_rms(q_ref[...], seg, qn_ref[...]) * ATTN_SCALE
    kc = _head_rms(kc_ref[...], seg, kn_ref[...])
    kp = _head_rms(kp_ref[...], seg, kn_ref[...])
    vc = vc_ref[...]
    wk_ref[0] = kc
    wv_ref[0] = vc
    kband = jnp.concatenate([kp, kc], axis=0)
    vband = jnp.concatenate([vp_ref[...], vc], axis=0)

    v_halves = []
    for kvh in range(N_KV_HEADS):
        tl = slice(kvh // HEADS_PER_TILE * LANES, (kvh // HEADS_PER_TILE + 1) * LANES)

        def both_halves(t):
            rolled = pltpu.roll(t, HEAD_DIM, axis=1)
            return jnp.where(low, t, rolled) if kvh % HEADS_PER_TILE == 0 else jnp.where(low, rolled, t)

        kdup = both_halves(kband[:, tl]).astype(BF16)
        vdup = both_halves(vband[:, tl])
        v_halves.append((jnp.where(low, vdup, 0.0).astype(BF16), jnp.where(low, 0.0, vdup).astype(BF16)))
        for pr in range(Q_PER_KV // HEADS_PER_TILE):
            a = kvh * Q_PER_KV // HEADS_PER_TILE + pr
            qt = qn[:, a * LANES:(a + 1) * LANES]
            s_ref[HEADS_PER_TILE * a] = _dot_nt(jnp.where(low, qt, 0.0).astype(BF16), kdup)
            s_ref[HEADS_PER_TILE * a + 1] = _dot_nt(jnp.where(low, 0.0, qt).astype(BF16), kdup)

    row = lax.broadcasted_iota(jnp.int32, (WINDOW, 2 * WINDOW), 0)
    col = lax.broadcasted_iota(jnp.int32, (WINDOW, 2 * WINDOW), 1)
    rel = (WINDOW + row) - col
    mask = (rel >= 0) & (rel <= WINDOW) & ((col >= WINDOW) | has_prev)
    for h in range(N_HEADS):
        s = jnp.where(mask, s_ref[h], NEG_INF)
        sk = sink_ref[h:h + 1, :]
        m = jnp.maximum(jnp.max(s, axis=-1, keepdims=True), sk)
        p = jnp.exp(s - m)
        p = p / (jnp.sum(p, axis=-1, keepdims=True) + jnp.exp(sk - m))
        p_ref[h] = p.astype(BF16)

    for a in range(N_HEADS // HEADS_PER_TILE):
        v_lo, v_hi = v_halves[a * HEADS_PER_TILE // Q_PER_KV]
        o = _dot(p_ref[HEADS_PER_TILE * a], v_lo) + _dot(p_ref[HEADS_PER_TILE * a + 1], v_hi)
        o_ref[:, a * LANES:(a + 1) * LANES] = o.astype(o_ref.dtype)


def attn_prompt(qkv, q_norm, k_norm, sinks, *, batch, seq):
    nb = seq // WINDOW
    kcol = ATTN_WIDTH // KV_WIDTH
    cur = lambda c: (lambda b, n: (b * nb + n, c))
    prev = lambda c: (lambda b, n: (jnp.maximum(b * nb + n - 1, 0), c))
    return pl.pallas_call(
        _attn_prompt_kernel,
        grid=(batch, nb),
        in_specs=[pl.BlockSpec((WINDOW, ATTN_WIDTH), cur(0)),
                  pl.BlockSpec((WINDOW, KV_WIDTH), cur(kcol)),
                  pl.BlockSpec((WINDOW, KV_WIDTH), prev(kcol)),
                  pl.BlockSpec((WINDOW, KV_WIDTH), cur(kcol + 1)),
                  pl.BlockSpec((WINDOW, KV_WIDTH), prev(kcol + 1)),
                  pl.BlockSpec((1, ATTN_WIDTH), lambda b, n: (0, 0)),
                  pl.BlockSpec((1, KV_WIDTH), lambda b, n: (0, 0)),
                  pl.BlockSpec((N_HEADS, 1), lambda b, n: (0, 0))],
        out_specs=[pl.BlockSpec((WINDOW, ATTN_WIDTH), lambda b, n: (b * nb + n, 0)),
                   pl.BlockSpec((1, WINDOW, KV_WIDTH), lambda b, n: (b, 0, 0)),
                   pl.BlockSpec((1, WINDOW, KV_WIDTH), lambda b, n: (b, 0, 0))],
        out_shape=[jax.ShapeDtypeStruct((batch * seq, ATTN_WIDTH), BF16),
                   jax.ShapeDtypeStruct((batch, WINDOW, KV_WIDTH), F32),
                   jax.ShapeDtypeStruct((batch, WINDOW, KV_WIDTH), F32)],
        scratch_shapes=[pltpu.VMEM((N_HEADS, WINDOW, 2 * WINDOW), F32),
                        pltpu.VMEM((N_HEADS, WINDOW, 2 * WINDOW), BF16)],
        compiler_params=_params("parallel", "arbitrary"),
        name="attn_prompt",
    )(qkv, qkv, qkv, qkv, qkv, jnp.tile(q_norm, N_HEADS).reshape(1, ATTN_WIDTH),
      jnp.tile(k_norm, N_KV_HEADS).reshape(1, KV_WIDTH), sinks.reshape(N_HEADS, 1))


def _attn_sample_kernel(qkv_ref, ck_ref, cv_ref, qn_ref, kn_ref, sink_ref,
                        o_ref, wk_ref, wv_ref, *, bt):
    qg = qn_ref[...]
    kg = kn_ref[...]
    for b in range(bt):
        wk_ref[b, 0:WINDOW - 1, :] = ck_ref[b, 1:WINDOW, :]
        wv_ref[b, 0:WINDOW - 1, :] = cv_ref[b, 1:WINDOW, :]
        wv_ref[b, WINDOW - 1:WINDOW, :] = qkv_ref[b:b + 1, ATTN_WIDTH + KV_WIDTH:ATTN_WIDTH + 2 * KV_WIDTH]
        for kvh in range(N_KV_HEADS):
            sl = slice(kvh * HEAD_DIM, (kvh + 1) * HEAD_DIM)
            kn = _rms(qkv_ref[b:b + 1, ATTN_WIDTH + kvh * HEAD_DIM:ATTN_WIDTH + (kvh + 1) * HEAD_DIM], kg)
            vn = qkv_ref[b:b + 1, ATTN_WIDTH + KV_WIDTH + kvh * HEAD_DIM:
                         ATTN_WIDTH + KV_WIDTH + (kvh + 1) * HEAD_DIM]
            wk_ref[b, WINDOW - 1:WINDOW, sl] = kn
            q4 = jnp.concatenate(
                [qkv_ref[b:b + 1, (kvh * Q_PER_KV + g) * HEAD_DIM:(kvh * Q_PER_KV + g + 1) * HEAD_DIM]
                 for g in range(Q_PER_KV)], axis=0)
            q4 = _rms(q4, qg)
            kwin = ck_ref[b, :, sl].astype(BF16)
            vwin = cv_ref[b, :, sl].astype(BF16)
            s_c = _dot_nt(q4.astype(BF16), kwin) * ATTN_SCALE
            s_n = jnp.sum(q4 * kn, axis=-1, keepdims=True) * ATTN_SCALE
            sk = sink_ref[kvh * Q_PER_KV:(kvh + 1) * Q_PER_KV, :]
            m = jnp.maximum(jnp.maximum(jnp.max(s_c, axis=-1, keepdims=True), s_n), sk)
            p_c = jnp.exp(s_c - m)
            p_n = jnp.exp(s_n - m)
            den = jnp.sum(p_c, axis=-1, keepdims=True) + p_n + jnp.exp(sk - m)
            o4 = _dot((p_c / den).astype(BF16), vwin) + (p_n / den) * vn
            for g in range(Q_PER_KV):
                h = kvh * Q_PER_KV + g
                o_ref[b:b + 1, h * HEAD_DIM:(h + 1) * HEAD_DIM] = o4[g:g + 1, :].astype(o_ref.dtype)


def attn_sample(qkv, cache_k, cache_v, q_norm, k_norm, sinks, *, bt=8):
    nb = qkv.shape[0]
    width = qkv.shape[1]
    win = pl.BlockSpec((bt, WINDOW, KV_WIDTH), lambda i: (i, 0, 0))
    return pl.pallas_call(
        functools.partial(_attn_sample_kernel, bt=bt),
        grid=(nb // bt,),
        in_specs=[pl.BlockSpec((bt, width), lambda i: (i, 0)), win, win,
                  pl.BlockSpec((1, HEAD_DIM), lambda i: (0, 0)),
                  pl.BlockSpec((1, HEAD_DIM), lambda i: (0, 0)),
                  pl.BlockSpec((N_HEADS, 1), lambda i: (0, 0))],
        out_specs=[pl.BlockSpec((bt, ATTN_WIDTH), lambda i: (i, 0)), win, win],
        out_shape=[jax.ShapeDtypeStruct((nb, ATTN_WIDTH), BF16),
                   jax.ShapeDtypeStruct((nb, WINDOW, KV_WIDTH), F32),
                   jax.ShapeDtypeStruct((nb, WINDOW, KV_WIDTH), F32)],
        compiler_params=_params("parallel"),
        name="attn_sample",
    )(qkv, cache_k, cache_v, q_norm.reshape(1, HEAD_DIM), k_norm.reshape(1, HEAD_DIM),
      sinks.reshape(N_HEADS, 1))


S5_KCH = S5_WIDTH // LANES
S5_CHUNK_STATES = S5_LANES // S5_KCH
S5_SCAN_TILES = 4


def _s5_kernel(u_ref, h0re_ref, h0im_ref, are_ref, aim_ref, wbu_ref, wc_ref, d_ref, wglu_ref,
               y_ref, hre_ref, him_ref, hs_ref, st_ref, *, rows, steps):
    rt = rows * steps
    u = u_ref[...].reshape(rt, S5_WIDTH)

    @pl.when(pl.program_id(0) == 0)
    def _():
        st_ref[:, :S5_LANES] = h0re_ref[...]
        st_ref[:, S5_LANES:] = h0im_ref[...]

    if steps > 1:
        assert rows == SUBLANES
        r = lax.broadcasted_iota(jnp.int32, (rt, rt), 0)
        c = lax.broadcasted_iota(jnp.int32, (rt, rt), 1)
        to_time_major = jnp.where(c == (r % rows) * steps + r // rows, 1.0, 0.0).astype(BF16)
        to_seq_major = jnp.where(c == (r % steps) * rows + r // steps, 1.0, 0.0).astype(BF16)
        hi, mid, lo = _split3(u)
        u_hi = _dot(to_time_major, hi)
        u = u_hi + (_dot(to_time_major, mid) + _dot(to_time_major, lo))
        ub = u_hi.astype(BF16)
    else:
        ub = u.astype(BF16)
    nre = S5_LANES // LANES
    tpk = S5_CHUNK_STATES // LANES
    for k in range(S5_KCH):
        r = _dot(ub[:, k * LANES:(k + 1) * LANES], wbu_ref[k])
        for a in range(tpk):
            hs_ref[k * tpk + a] = r[:, a * LANES:(a + 1) * LANES]
            hs_ref[nre + k * tpk + a] = r[:, (tpk + a) * LANES:(tpk + a + 1) * LANES]

    nt = S5_SCAN_TILES
    for j in range(nre // nt):
        tiles = range(j * nt, (j + 1) * nt)
        ar = [jnp.broadcast_to(are_ref[:, a * LANES:(a + 1) * LANES], (SUBLANES, LANES)) for a in tiles]
        ai = [jnp.broadcast_to(aim_ref[:, a * LANES:(a + 1) * LANES], (SUBLANES, LANES)) for a in tiles]

        def advance(r8, hr, hi):
            nr, ni = [], []
            for n, a in enumerate(tiles):
                nr.append(ar[n] * hr[n] - ai[n] * hi[n] + hs_ref[a, r8, :])
                ni.append(ar[n] * hi[n] + ai[n] * hr[n] + hs_ref[nre + a, r8, :])
                hs_ref[a, r8, :] = nr[n]
                hs_ref[nre + a, r8, :] = ni[n]
            return nr, ni

        if steps == 1:
            def group(rg, carry):
                r8 = pl.ds(pl.multiple_of(rg * SUBLANES, SUBLANES), SUBLANES)
                hr = [st_ref[r8, a * LANES:(a + 1) * LANES] for a in tiles]
                hi = [st_ref[r8, S5_LANES + a * LANES:S5_LANES + (a + 1) * LANES] for a in tiles]
                nr, ni = advance(r8, hr, hi)
                for n, a in enumerate(tiles):
                    st_ref[r8, a * LANES:(a + 1) * LANES] = nr[n]
                    st_ref[r8, S5_LANES + a * LANES:S5_LANES + (a + 1) * LANES] = ni[n]
                return carry
            lax.fori_loop(0, rows // SUBLANES, group, 0)
        else:
            def step(t, carry):
                nr, ni = advance(pl.ds(pl.multiple_of(t * SUBLANES, SUBLANES), SUBLANES), *carry)
                return tuple(nr), tuple(ni)
            hr0 = tuple(st_ref[:, a * LANES:(a + 1) * LANES] for a in tiles)
            hi0 = tuple(st_ref[:, S5_LANES + a * LANES:S5_LANES + (a + 1) * LANES] for a in tiles)
            hr, hi = lax.fori_loop(0, steps, step, (hr0, hi0))
            for n, a in enumerate(tiles):
                st_ref[:, a * LANES:(a + 1) * LANES] = hr[n]
                st_ref[:, S5_LANES + a * LANES:S5_LANES + (a + 1) * LANES] = hi[n]

    ys = []
    for k in range(S5_KCH):
        hre = jnp.concatenate([hs_ref[k * tpk + a] for a in range(tpk)], axis=1).astype(BF16)
        him = jnp.concatenate([hs_ref[nre + k * tpk + a] for a in range(tpk)], axis=1).astype(BF16)
        yk = _dot(hre, wc_ref[0, k]) + _dot(him, wc_ref[1, k])
        ys.append(yk + d_ref[:, k * LANES:(k + 1) * LANES] * u[:, k * LANES:(k + 1) * LANES])
    y = jax.nn.gelu(jnp.concatenate(ys, axis=1))
    out = (y * jax.nn.sigmoid(_dot(y.astype(BF16), wglu_ref[...]))).astype(y_ref.dtype)
    if steps > 1:
        out = _dot(to_seq_major, out).astype(y_ref.dtype)
    y_ref[...] = out.reshape(y_ref.shape)

    @pl.when(pl.program_id(0) == pl.num_programs(0) - 1)
    def _():
        hre_ref[...] = st_ref[:, :S5_LANES]
        him_ref[...] = st_ref[:, S5_LANES:]


def s5_mix(u, h0_re, h0_im, consts, *, rows, seq, steps):
    a_re, a_im, w_bu, w_c, d_skip, w_glu = consts
    if seq > 1:
        u_spec = pl.BlockSpec((rows, steps, S5_WIDTH), lambda c: (0, c, 0))
    else:
        u_spec = pl.BlockSpec((rows, S5_WIDTH), lambda c: (0, 0))
    full = lambda shape: pl.BlockSpec(shape, lambda c: (0,) * len(shape))
    return pl.pallas_call(
        functools.partial(_s5_kernel, rows=rows, steps=steps),
        grid=(seq // steps,),
        in_specs=[u_spec, full((rows, S5_LANES)), full((rows, S5_LANES)),
                  full((1, S5_LANES)), full((1, S5_LANES)),
                  full(w_bu.shape), full(w_c.shape), full((1, S5_WIDTH)), full(w_glu.shape)],
        out_specs=[u_spec, full((rows, S5_LANES)), full((rows, S5_LANES))],
        out_shape=[jax.ShapeDtypeStruct(u.shape, BF16),
                   jax.ShapeDtypeStruct((rows, S5_LANES), F32),
                   jax.ShapeDtypeStruct((rows, S5_LANES), F32)],
        scratch_shapes=[pltpu.VMEM((2 * S5_LANES // LANES, rows * steps, LANES), F32),
                        pltpu.VMEM((rows, 2 * S5_LANES), F32)],
        compiler_params=_params("arbitrary"),
        name="s5_mix",
    )(u, h0_re, h0_im, a_re, a_im, w_bu, w_c, d_skip, w_glu)


def s5_constants(a_re, a_im, log_dt, b_re, b_im, c_re, c_im, d_skip, w_glu):
    lr, li = a_re, a_im
    dt = jnp.exp(log_dt)[:, None]
    mag = jnp.exp(lr * dt)
    ab_re, ab_im = mag * jnp.cos(li * dt), mag * jnp.sin(li * dt)
    den = lr * lr + li * li
    f_re = ((ab_re - 1.0) * lr + ab_im * li) / den
    f_im = (ab_im * lr - (ab_re - 1.0) * li) / den
    bb_re = f_re[..., None] * b_re - f_im[..., None] * b_im
    bb_im = f_re[..., None] * b_im + f_im[..., None] * b_re
    gpc = LANES // S5_GROUP_CH
    eye = jnp.eye(gpc, dtype=F32)

    def bu_blocks(bb):
        t = bb.reshape(S5_KCH, gpc, S5_STATE, S5_GROUP_CH)
        return jnp.einsum('kgpc,gh->kgchp', t, eye).reshape(S5_KCH, LANES, gpc * S5_STATE)

    def c_blocks(c):
        t = c.reshape(S5_KCH, gpc, S5_GROUP_CH, S5_STATE)
        return jnp.einsum('kgcp,gh->kgphc', t, eye).reshape(S5_KCH, gpc * S5_STATE, LANES)

    w_bu = jnp.concatenate([bu_blocks(bb_re), bu_blocks(bb_im)], axis=-1).astype(BF16)
    w_c = jnp.stack([c_blocks(c_re), -c_blocks(c_im)]).astype(BF16)
    return (ab_re.reshape(1, S5_LANES), ab_im.reshape(1, S5_LANES), w_bu, w_c,
            d_skip.reshape(1, S5_WIDTH), w_glu.astype(BF16))


def _ffn_kernel(x_ref, g_ref, wg_ref, wu_ref, wd_ref, o_ref, xn_ref):
    @pl.when(pl.program_id(1) == 0)
    def _():
        x = x_ref[...]
        xn_ref[...] = _rms(x, g_ref[...]).astype(BF16)
        o_ref[...] = x

    xn = xn_ref[...]
    h = jax.nn.silu(_dot(xn, wg_ref[...])) * _dot(xn, wu_ref[...])
    o_ref[...] += _dot(h.astype(BF16), wd_ref[...])


def ffn(x, g, w_gate, w_up, w_down, *, tm, tf):
    m, d = x.shape
    f = w_gate.shape[1]
    return pl.pallas_call(
        _ffn_kernel,
        grid=(m // tm, f // tf),
        in_specs=[pl.BlockSpec((tm, d), lambda i, j: (i, 0), pipeline_mode=pl.Buffered(1)),
                  pl.BlockSpec((1, d), lambda i, j: (0, 0)),
                  pl.BlockSpec((d, tf), lambda i, j: (0, j)),
                  pl.BlockSpec((d, tf), lambda i, j: (0, j)),
                  pl.BlockSpec((tf, d), lambda i, j: (j, 0))],
        out_specs=pl.BlockSpec((tm, d), lambda i, j: (i, 0)),
        out_shape=jax.ShapeDtypeStruct((m, d), F32),
        scratch_shapes=[pltpu.VMEM((tm, d), BF16)],
        compiler_params=_params("parallel", "arbitrary"),
        name="ffn",
    )(x, g.reshape(1, d), w_gate, w_up, w_down)


MOE_TILE = 512
META_I1, META_I2, META_W1, META_W2, META_R1, META_R2 = range(6)


def _router_kernel(x_ref, g_ref, r_ref, xn_ref, meta_ref, cnt_ref, carry_ref):
    @pl.when(pl.program_id(0) == 0)
    def _():
        carry_ref[...] = jnp.zeros_like(carry_ref)

    xn = _rms(x_ref[...], g_ref[...])
    xn_ref[...] = xn
    x_hi, x_mid, _ = _split3(xn)
    r_hi, r_mid, _ = _split3(r_ref[...])
    logits = _dot(x_hi, r_hi) + (_dot(x_mid, r_hi) + _dot(x_hi, r_mid))
    lane = lax.broadcasted_iota(jnp.int32, logits.shape, 1)
    neg = -jnp.inf
    l1 = jnp.where(lane < N_EXPERTS, logits, neg)
    m1 = jnp.max(l1, axis=-1, keepdims=True)
    i1 = jnp.min(jnp.where(l1 == m1, lane, LANES), axis=-1, keepdims=True)
    l2 = jnp.where(lane == i1, neg, l1)
    m2 = jnp.max(l2, axis=-1, keepdims=True)
    i2 = jnp.min(jnp.where(l2 == m2, lane, LANES), axis=-1, keepdims=True)
    e = jnp.exp(m2 - m1)
    den = 1.0 + e
    sel = jnp.where(lane == i1, 1.0, jnp.where(lane == i2, 1.0, 0.0))
    tm = sel.shape[0]
    row = lax.broadcasted_iota(jnp.int32, (tm, tm), 0)
    col = lax.broadcasted_iota(jnp.int32, (tm, tm), 1)
    incl = _dot(jnp.where(col <= row, 1.0, 0.0).astype(BF16), sel.astype(BF16))
    excl = incl - sel + carry_ref[0:1, :]
    rank1 = jnp.sum(jnp.where(lane == i1, excl, 0.0), axis=-1, keepdims=True)
    rank2 = jnp.sum(jnp.where(lane == i2, excl, 0.0), axis=-1, keepdims=True)
    carry_ref[...] = carry_ref[...] + incl[tm - 1:tm, :]
    cnt_ref[...] = carry_ref[...]
    fields = {META_I1: i1.astype(F32), META_I2: i2.astype(F32), META_W1: 1.0 / den, META_W2: e / den,
              META_R1: rank1, META_R2: rank2}
    meta = jnp.zeros(logits.shape, F32)
    for k, val in fields.items():
        meta = jnp.where(lane == k, val, meta)
    meta_ref[...] = meta


def router(x, g, r, *, tm):
    m, d = x.shape
    r_pad = jnp.zeros((d, LANES), F32).at[:, :N_EXPERTS].set(r)
    return pl.pallas_call(
        _router_kernel,
        grid=(m // tm,),
        in_specs=[pl.BlockSpec((tm, d), lambda i: (i, 0)),
                  pl.BlockSpec((1, d), lambda i: (0, 0)),
                  pl.BlockSpec((d, LANES), lambda i: (0, 0))],
        out_specs=[pl.BlockSpec((tm, d), lambda i: (i, 0)),
                   pl.BlockSpec((tm, LANES), lambda i: (i, 0)),
                   pl.BlockSpec((SUBLANES, LANES), lambda i: (0, 0))],
        out_shape=[jax.ShapeDtypeStruct((m, d), F32),
                   jax.ShapeDtypeStruct((m, LANES), F32),
                   jax.ShapeDtypeStruct((SUBLANES, LANES), F32)],
        scratch_shapes=[pltpu.VMEM((SUBLANES, LANES), F32)],
        compiler_params=_params("arbitrary"),
        name="router",
    )(x, g.reshape(1, d), r_pad)


def _dispatch_kernel(p1_ref, p2_ref, xn_ref, zeros_ref, xs_ref, sem, *, td):
    del zeros_ref
    base = pl.program_id(0) * td

    def copies(r):
        src = xn_ref.at[pl.ds(r, 1), :]
        return [pltpu.make_async_copy(src, xs_ref.at[pl.ds(p_ref[base + r], 1), :], sem)
                for p_ref in (p1_ref, p2_ref)]

    def issue(r, carry):
        for c in copies(r):
            c.start()
        return carry

    def drain(r, carry):
        for c in copies(r):
            c.wait()
        return carry

    lax.fori_loop(0, td, issue, 0)
    lax.fori_loop(0, td, drain, 0)


def moe_dispatch(xn, pos1, pos2, *, rows, td):
    m, d = xn.shape
    return pl.pallas_call(
        functools.partial(_dispatch_kernel, td=td),
        grid_spec=pltpu.PrefetchScalarGridSpec(
            num_scalar_prefetch=2,
            grid=(m // td,),
            in_specs=[pl.BlockSpec((td, d), lambda i, p1, p2: (i, 0)),
                      pl.BlockSpec(memory_space=pl.ANY)],
            out_specs=pl.BlockSpec(memory_space=pl.ANY),
            scratch_shapes=[pltpu.SemaphoreType.DMA(())]),
        out_shape=jax.ShapeDtypeStruct((rows, d), F32),
        input_output_aliases={3: 0},
        compiler_params=_params("arbitrary"),
        name="moe_dispatch",
    )(pos1, pos2, xn, jnp.zeros((rows, d), F32))


def _expert_kernel(te_ref, tv_ref, xs_ref, wg_ref, wu_ref, wd_ref, ys_ref, xb_ref):
    del te_ref
    f = pl.program_id(1)
    valid = tv_ref[pl.program_id(0)]

    @pl.when(valid > 0)
    def _():
        @pl.when(f == 0)
        def _():
            xb_ref[...] = xs_ref[...].astype(BF16)

        xb = xb_ref[...]
        h = jax.nn.silu(_dot(xb, wg_ref[0])) * _dot(xb, wu_ref[0])
        y = _dot(h.astype(BF16), wd_ref[0])

        @pl.when(f == 0)
        def _():
            ys_ref[...] = y

        @pl.when(f > 0)
        def _():
            ys_ref[...] += y

    @pl.when((valid == 0) & (f == 0))
    def _():
        ys_ref[...] = jnp.zeros_like(ys_ref)


def moe_experts(xs, tile_expert, tile_valid, w_gate, w_up, w_down, *, tile, tf):
    rows, d = xs.shape
    _, _, f = w_gate.shape
    nf = f // tf
    fidx = lambda t, j, tv: jnp.where(tv[t] > 0, j, nf - 1)
    return pl.pallas_call(
        _expert_kernel,
        grid_spec=pltpu.PrefetchScalarGridSpec(
            num_scalar_prefetch=2,
            grid=(rows // tile, nf),
            in_specs=[pl.BlockSpec((tile, d), lambda t, j, te, tv: (t, 0)),
                      pl.BlockSpec((1, d, tf), lambda t, j, te, tv: (te[t], 0, fidx(t, j, tv))),
                      pl.BlockSpec((1, d, tf), lambda t, j, te, tv: (te[t], 0, fidx(t, j, tv))),
                      pl.BlockSpec((1, tf, d), lambda t, j, te, tv: (te[t], fidx(t, j, tv), 0))],
            out_specs=pl.BlockSpec((tile, d), lambda t, j, te, tv: (t, 0)),
            scratch_shapes=[pltpu.VMEM((tile, d), BF16)]),
        out_shape=jax.ShapeDtypeStruct((rows, d), F32),
        compiler_params=_params("arbitrary", "arbitrary"),
        name="moe_experts",
    )(tile_expert, tile_valid, xs, w_gate, w_up, w_down)


def _combine_kernel(p1_ref, p2_ref, x_ref, meta_ref, ys_ref, o_ref, buf_ref, sem, *, tc):
    i = pl.program_id(0)
    slot = i % 2

    def copies(step, sl, r):
        tok = step * tc + r
        return [pltpu.make_async_copy(ys_ref.at[pl.ds(p_ref[tok], 1), :],
                                      buf_ref.at[sl, k, pl.ds(r, 1), :], sem.at[sl])
                for k, p_ref in enumerate((p1_ref, p2_ref))]

    def issue(step, sl):
        def body(r, carry):
            for c in copies(step, sl, r):
                c.start()
            return carry
        lax.fori_loop(0, tc, body, 0)

    @pl.when(i == 0)
    def _():
        issue(0, 0)

    @pl.when(i + 1 < pl.num_programs(0))
    def _():
        issue(i + 1, 1 - slot)

    def drain(r, carry):
        for c in copies(i, slot, r):
            c.wait()
        return carry
    lax.fori_loop(0, tc, drain, 0)

    meta = meta_ref[...]
    w1 = meta[:, META_W1:META_W1 + 1]
    w2 = meta[:, META_W2:META_W2 + 1]
    o_ref[...] = x_ref[...] + w1 * buf_ref[slot, 0] + w2 * buf_ref[slot, 1]


def moe_combine(x, meta, ys, pos1, pos2, *, tc):
    m, d = x.shape
    return pl.pallas_call(
        functools.partial(_combine_kernel, tc=tc),
        grid_spec=pltpu.PrefetchScalarGridSpec(
            num_scalar_prefetch=2,
            grid=(m // tc,),
            in_specs=[pl.BlockSpec((tc, d), lambda i, p1, p2: (i, 0)),
                      pl.BlockSpec((tc, LANES), lambda i, p1, p2: (i, 0)),
                      pl.BlockSpec(memory_space=pl.ANY)],
            out_specs=pl.BlockSpec((tc, d), lambda i, p1, p2: (i, 0)),
            scratch_shapes=[pltpu.VMEM((2, 2, tc, d), F32), pltpu.SemaphoreType.DMA((2,))]),
        out_shape=jax.ShapeDtypeStruct((m, d), F32),
        compiler_params=_params("arbitrary"),
        name="moe_combine",
    )(pos1, pos2, x, meta, ys)


def moe_sparse(x, g, r, w_gate, w_up, w_down):
    m, d = x.shape
    tm = min(m, 512)
    tt = min(m, 256)
    tile = min(m, MOE_TILE)
    xn, meta, counts = router(x, g, r, tm=tm)
    cnt = counts[0, :N_EXPERTS].astype(jnp.int32)
    tiles_e = (cnt + tile - 1) // tile
    tile_end = jnp.cumsum(tiles_e)
    tile_start = tile_end - tiles_e
    n_tiles = (2 * m) // tile + N_EXPERTS
    t_all = jnp.arange(n_tiles, dtype=jnp.int32)
    t = jnp.minimum(t_all, tile_end[-1] - 1)
    tile_expert = jnp.sum(t[:, None] >= tile_end[None, :], axis=1).astype(jnp.int32)
    tile_valid = jnp.clip(cnt[tile_expert] - (t - tile_start[tile_expert]) * tile, 0, tile)
    tile_valid = jnp.where(t_all < tile_end[-1], tile_valid, 0).astype(jnp.int32)
    off = tile_start * tile
    pos1 = off[meta[:, META_I1].astype(jnp.int32)] + meta[:, META_R1].astype(jnp.int32)
    pos2 = off[meta[:, META_I2].astype(jnp.int32)] + meta[:, META_R2].astype(jnp.int32)
    xs = moe_dispatch(xn, pos1, pos2, rows=n_tiles * tile, td=tt)
    ys = moe_experts(xs, tile_expert, tile_valid, w_gate, w_up, w_down, tile=tile, tf=512)
    return moe_combine(x, meta, ys, pos1, pos2, tc=tt)


def _gates_kernel(ba_ref, alog_ref, dtb_ref, beta_ref, eg_ref, gcum_ref, *, tg):
    ba = ba_ref[...]
    b = ba[:, :DN_V_HEADS]
    a = ba[:, DN_V_HEADS:2 * DN_V_HEADS]
    beta_ref[...] = jax.nn.sigmoid(b)
    z = a + dtb_ref[...]
    softplus = jnp.maximum(z, 0.0) + jnp.log1p(jnp.exp(-jnp.abs(z)))
    g = -jnp.exp(alog_ref[...]) * softplus
    eg_ref[...] = jnp.exp(g)
    r = lax.broadcasted_iota(jnp.int32, (tg, tg), 0)
    c = lax.broadcasted_iota(jnp.int32, (tg, tg), 1)
    tri = jnp.where((c <= r) & (r // DN_CHUNK == c // DN_CHUNK), 1.0, 0.0).astype(BF16)
    hi, mid, lo = _split3(g)
    gcum_ref[...] = _dot(tri, hi) + (_dot(tri, mid) + _dot(tri, lo))


def dn_gates(ba, a_log, dt_bias, *, tg):
    m = ba.shape[0]
    out = jax.ShapeDtypeStruct((m, DN_V_HEADS), F32)
    spec = pl.BlockSpec((tg, DN_V_HEADS), lambda i: (i, 0))
    return pl.pallas_call(
        functools.partial(_gates_kernel, tg=tg),
        grid=(m // tg,),
        in_specs=[pl.BlockSpec((tg, LANES), lambda i: (i, 0)),
                  pl.BlockSpec((1, DN_V_HEADS), lambda i: (0, 0)),
                  pl.BlockSpec((1, DN_V_HEADS), lambda i: (0, 0))],
        out_specs=[spec, spec, spec],
        out_shape=[out, out, out],
        compiler_params=_params("parallel"),
        name="dn_gates",
    )(ba, a_log.reshape(1, DN_V_HEADS), dt_bias.reshape(1, DN_V_HEADS))


def _l2norm_heads(c, scale):
    parts = []
    for h in range(c.shape[1] // DN_K_DIM):
        t = c[:, h * DN_K_DIM:(h + 1) * DN_K_DIM]
        t = t * lax.rsqrt(jnp.sum(t * t, axis=-1, keepdims=True) + EPS)
        parts.append(t * scale if scale != 1.0 else t)
    return jnp.concatenate(parts, axis=1)


def _conv_prompt_kernel(x_ref, halo_ref, w_ref, o_ref, *, tl, tc):
    j = pl.program_id(2)
    halo = jnp.where(pl.program_id(1) > 0, halo_ref[0], 0.0)
    xcat = jnp.concatenate([halo, x_ref[0]], axis=0)
    conv = xcat[SUBLANES - 3:SUBLANES - 3 + tl] * w_ref[0:1, :]
    for i in range(1, DN_CONV_K):
        conv = conv + xcat[SUBLANES - 3 + i:SUBLANES - 3 + i + tl] * w_ref[i:i + 1, :]
    c = jax.nn.silu(conv)
    nq = DN_QK_W // tc

    @pl.when(j < nq)
    def _():
        o_ref[0] = _l2norm_heads(c, DN_K_DIM ** -0.5)

    @pl.when((j >= nq) & (j < 2 * nq))
    def _():
        o_ref[0] = _l2norm_heads(c, 1.0)

    @pl.when(j >= 2 * nq)
    def _():
        o_ref[0] = c


def dn_conv_prompt(proj, conv_w, *, tl, tc):
    b, l, _ = proj.shape
    hb = tl // SUBLANES
    return pl.pallas_call(
        functools.partial(_conv_prompt_kernel, tl=tl, tc=tc),
        grid=(b, l // tl, DN_CONV_DIM // tc),
        in_specs=[pl.BlockSpec((1, tl, tc), lambda i, t, j: (i, t, j)),
                  pl.BlockSpec((1, SUBLANES, tc), lambda i, t, j: (i, jnp.maximum(t * hb - 1, 0), j)),
                  pl.BlockSpec((DN_CONV_K, tc), lambda i, t, j: (0, j))],
        out_specs=pl.BlockSpec((1, tl, tc), lambda i, t, j: (i, t, j)),
        out_shape=jax.ShapeDtypeStruct((b, l, DN_CONV_DIM), F32),
        compiler_params=_params("parallel", "parallel", "arbitrary"),
        name="dn_conv_prompt",
    )(proj, proj, conv_w)


DN_GROUP = 4
DN_LBLOCK = 512


def _dn_chunk_kernel(q_ref, k_ref, v_ref, z_ref, gc_ref, beta_ref, gt_ref, onorm_ref,
                     o_ref, s_out_ref, s_ref, wq_ref, ak_ref, u_ref, oacc_ref, egl_ref, *, nc):
    hg = pl.program_id(1)
    lb = pl.program_id(2)
    cz = DN_CHUNK

    @pl.when(lb == 0)
    def _():
        s_ref[...] = jnp.zeros_like(s_ref)

    gc_all = gc_ref[0]
    beta_all = beta_ref[0]
    lane = lax.broadcasted_iota(jnp.int32, gc_all.shape, 1)
    ri = lax.broadcasted_iota(jnp.int32, (cz, cz), 0)
    ci = lax.broadcasted_iota(jnp.int32, (cz, cz), 1)
    incl = (ri >= ci)[None]
    strict = (ri > ci)[None]

    for g in range(DN_GROUP):
        kh = g // 2
        head = hg * DN_GROUP + g
        q = q_ref[0, :, kh * DN_K_DIM:(kh + 1) * DN_K_DIM].reshape(nc, cz, DN_K_DIM)
        k = k_ref[0, :, kh * DN_K_DIM:(kh + 1) * DN_K_DIM].reshape(nc, cz, DN_K_DIM)
        v = v_ref[0, :, g * DN_V_DIM:(g + 1) * DN_V_DIM].reshape(nc, cz, DN_V_DIM)
        gcol = jnp.sum(jnp.where(lane == head, gc_all, 0.0), axis=-1, keepdims=True).reshape(nc, cz, 1)
        bcol = jnp.sum(jnp.where(lane == head, beta_all, 0.0), axis=-1, keepdims=True).reshape(nc, cz, 1)
        grow = gt_ref[0, pl.ds(head, 1), :]
        grow = jnp.stack([grow[:, c * cz:(c + 1) * cz] for c in range(nc)])
        decay = jnp.where(incl, jnp.exp(gcol - grow), 0.0)
        kb = k * bcol
        kbf = k.astype(BF16)
        kk = jnp.einsum('cid,cjd->cij', kb.astype(BF16), kbf, preferred_element_type=F32)
        neg_l = jnp.where(strict, -(kk * decay), 0.0)
        n_acc = neg_l
        pw = neg_l
        for _ in range(5):
            pwb = pw.astype(BF16)
            pw = jnp.einsum('cij,cjk->cik', pwb, pwb, preferred_element_type=F32)
            n_acc = n_acc + pw + jnp.einsum('cij,cjk->cik', n_acc.astype(BF16), pw.astype(BF16),
                                            preferred_element_type=F32)
        egc = jnp.exp(gcol)
        rhs = jnp.concatenate([v * bcol, kb * egc], axis=-1)
        n_hi, n_mid, _ = _split3(n_acc)
        r_hi, r_mid, _ = _split3(rhs)
        bmm = lambda a, b: jnp.einsum('cij,cjd->cid', a, b, preferred_element_type=F32)
        sol = rhs + (bmm(n_hi, r_hi) + (bmm(n_mid, r_hi) + bmm(n_hi, r_mid)))
        u_ref[g] = sol[..., :DN_V_DIM]
        qk = jnp.einsum('cid,cjd->cij', q.astype(BF16), kbf, preferred_element_type=F32) * decay
        glast = gcol[:, cz - 1:cz, :]
        kd = k * jnp.exp(glast - gcol)
        wq_ref[g, :, 0:cz, :] = sol[..., DN_V_DIM:].astype(BF16)
        wq_ref[g, :, cz:2 * cz, :] = (q * egc).astype(BF16)
        ak_ref[g, :, 0:cz, :] = qk.astype(BF16)
        for c in range(nc):
            ak_ref[g, c, cz:cz + DN_K_DIM, :] = jnp.transpose(kd[c]).astype(BF16)
        egl_ref[g] = jnp.broadcast_to(jnp.exp(glast), (nc, 1, DN_V_DIM))

    def chunk(c, carry):
        for g in range(DN_GROUP):
            s = s_ref[g]
            ws_qs = _dot(wq_ref[g, c], s.astype(BF16))
            v_new = u_ref[g, c] - ws_qs[0:cz]
            av_kv = _dot(ak_ref[g, c], v_new.astype(BF16))
            oacc_ref[g, pl.ds(pl.multiple_of(c * cz, cz), cz), :] = ws_qs[cz:2 * cz] + av_kv[0:cz]
            s_ref[g] = s * egl_ref[g, c] + av_kv[cz:cz + DN_K_DIM]
        return carry

    lax.fori_loop(0, nc, chunk, 0)

    for g in range(DN_GROUP):
        o = oacc_ref[g]
        z = z_ref[0, :, g * DN_V_DIM:(g + 1) * DN_V_DIM]
        o_ref[0, :, g * DN_V_DIM:(g + 1) * DN_V_DIM] = (
            _rms(o, onorm_ref[...]) * jax.nn.silu(z)).astype(o_ref.dtype)

    @pl.when(lb == pl.num_programs(2) - 1)
    def _():
        s_out_ref[0] = s_ref[...]


def dn_chunked(qkvc, proj, gcum, beta, gcum_t, out_norm):
    b, l, _ = qkvc.shape
    lbk = min(DN_LBLOCK, l)
    nc = lbk // DN_CHUNK
    gk = DN_GROUP // 2 * DN_K_DIM
    gv = DN_GROUP * DN_V_DIM
    cz = DN_CHUNK
    return pl.pallas_call(
        functools.partial(_dn_chunk_kernel, nc=nc),
        grid=(b, DN_V_HEADS // DN_GROUP, l // lbk),
        in_specs=[pl.BlockSpec((1, lbk, gk), lambda i, h, t: (i, t, h)),
                  pl.BlockSpec((1, lbk, gk), lambda i, h, t: (i, t, DN_QK_W // gk + h)),
                  pl.BlockSpec((1, lbk, gv), lambda i, h, t: (i, t, 2 * DN_QK_W // gv + h)),
                  pl.BlockSpec((1, lbk, gv), lambda i, h, t: (i, t, DN_CONV_DIM // gv + h)),
                  pl.BlockSpec((1, lbk, DN_V_HEADS), lambda i, h, t: (i, t, 0)),
                  pl.BlockSpec((1, lbk, DN_V_HEADS), lambda i, h, t: (i, t, 0)),
                  pl.BlockSpec((1, DN_V_HEADS, lbk), lambda i, h, t: (i, 0, t)),
                  pl.BlockSpec((1, DN_V_DIM), lambda i, h, t: (0, 0))],
        out_specs=[pl.BlockSpec((1, lbk, gv), lambda i, h, t: (i, t, h)),
                   pl.BlockSpec((1, DN_GROUP, DN_K_DIM, DN_V_DIM), lambda i, h, t: (i, h, 0, 0))],
        out_shape=[jax.ShapeDtypeStruct((b, l, DN_V_W), BF16),
                   jax.ShapeDtypeStruct((b, DN_V_HEADS, DN_K_DIM, DN_V_DIM), F32)],
        scratch_shapes=[pltpu.VMEM((DN_GROUP, DN_K_DIM, DN_V_DIM), F32),
                        pltpu.VMEM((DN_GROUP, nc, 2 * cz, DN_K_DIM), BF16),
                        pltpu.VMEM((DN_GROUP, nc, cz + DN_K_DIM, cz), BF16),
                        pltpu.VMEM((DN_GROUP, nc, cz, DN_V_DIM), F32),
                        pltpu.VMEM((DN_GROUP, lbk, DN_V_DIM), F32),
                        pltpu.VMEM((DN_GROUP, nc, 1, DN_V_DIM), F32)],
        compiler_params=_params("parallel", "parallel", "arbitrary"),
        name="dn_chunked",
    )(qkvc, qkvc, qkvc, proj, gcum, beta, gcum_t, out_norm.reshape(1, DN_V_DIM))


DN_ROWS = DN_CONV_DIM // LANES
DN_QROWS = DN_K_HEADS
DN_VROW0 = 2 * DN_K_HEADS


def _dn_sample_kernel(x_ref, buf_ref, w_ref, z_ref, eg_ref, beta_ref, s_ref, onorm_ref,
                      o_ref, s_out_ref, buf_out_ref):
    b = pl.program_id(0)
    x = x_ref[0]
    buf = buf_ref[0]
    conv = buf[0] * w_ref[0]
    for i in range(1, DN_CONV_K - 1):
        conv = conv + buf[i] * w_ref[i]
    conv = conv + x * w_ref[DN_CONV_K - 1]
    buf_out_ref[0, 0:DN_CONV_K - 2] = buf[1:DN_CONV_K - 1]
    buf_out_ref[0, DN_CONV_K - 2] = x
    c = jax.nn.silu(conv)
    qk = c[0:DN_VROW0]
    qk = qk * lax.rsqrt(jnp.sum(qk * qk, axis=-1, keepdims=True) + EPS)
    q_t = jnp.transpose(qk[0:DN_QROWS] * (DN_K_DIM ** -0.5))
    k_t = jnp.transpose(qk[DN_QROWS:DN_VROW0])
    for h in range(DN_V_HEADS):
        kh = h // (DN_V_HEADS // DN_K_HEADS)
        kcol = k_t[:, kh:kh + 1]
        qcol = q_t[:, kh:kh + 1]
        v = c[DN_VROW0 + h:DN_VROW0 + h + 1]
        s = s_ref[0, h] * eg_ref[b, h]
        delta = (v - jnp.sum(kcol * s, axis=0, keepdims=True)) * beta_ref[b, h]
        s = s + kcol * delta
        s_out_ref[0, h] = s
        o = jnp.sum(qcol * s, axis=0, keepdims=True)
        o_ref[0, h:h + 1, :] = (_rms(o, onorm_ref[...]) * jax.nn.silu(z_ref[0, h:h + 1, :])
                                ).astype(o_ref.dtype)


def dn_sample(qkv_rows, conv_buf, conv_w, z_rows, eg, beta, state, out_norm):
    nb = qkv_rows.shape[0]
    smem = pl.BlockSpec(memory_space=pltpu.SMEM)
    return pl.pallas_call(
        _dn_sample_kernel,
        grid=(nb,),
        in_specs=[pl.BlockSpec((1, DN_ROWS, LANES), lambda i: (i, 0, 0)),
                  pl.BlockSpec((1, DN_CONV_K - 1, DN_ROWS, LANES), lambda i: (i, 0, 0, 0)),
                  pl.BlockSpec((DN_CONV_K, DN_ROWS, LANES), lambda i: (0, 0, 0)),
                  pl.BlockSpec((1, DN_V_HEADS, DN_V_DIM), lambda i: (i, 0, 0)),
                  smem, smem,
                  pl.BlockSpec((1, DN_V_HEADS, DN_K_DIM, DN_V_DIM), lambda i: (i, 0, 0, 0)),
                  pl.BlockSpec((1, DN_V_DIM), lambda i: (0, 0))],
        out_specs=[pl.BlockSpec((1, DN_V_HEADS, DN_V_DIM), lambda i: (i, 0, 0)),
                   pl.BlockSpec((1, DN_V_HEADS, DN_K_DIM, DN_V_DIM), lambda i: (i, 0, 0, 0)),
                   pl.BlockSpec((1, DN_CONV_K - 1, DN_ROWS, LANES), lambda i: (i, 0, 0, 0))],
        out_shape=[jax.ShapeDtypeStruct((nb, DN_V_HEADS, DN_V_DIM), BF16),
                   jax.ShapeDtypeStruct(state.shape, F32),
                   jax.ShapeDtypeStruct(conv_buf.shape, F32)],
        compiler_params=_params("arbitrary"),
        name="dn_sample",
    )(qkv_rows, conv_buf, conv_w, z_rows, eg, beta, state, out_norm.reshape(1, DN_V_DIM))


def _tile(m, cap):
    return min(m, cap)


def _even_layer(x, cache, w, *, batch, seq):
    m = x.shape[0]
    tm = _tile(m, 1024)
    qkv = norm_matmul(x, w['norm_mix'], w['w_in_qkv'], tm=tm, tn=512)
    u = norm_matmul(x, w['norm_mix'], w['w_in_u'], tm=tm, tn=512)
    if cache is None:
        attn, new_k, new_v = attn_prompt(qkv, w['q_norm'], w['k_norm'], w['sinks'], batch=batch, seq=seq)
        zeros = jnp.zeros((batch, S5_LANES), F32)
        ssm, h_re, h_im = s5_mix(u.reshape(batch, seq, S5_WIDTH), zeros, zeros, w['s5'],
                                 rows=batch, seq=seq, steps=64)
        ssm = ssm.reshape(m, S5_WIDTH)
    else:
        k_win, v_win, h0_re, h0_im = cache
        attn, new_k, new_v = attn_sample(qkv, k_win.reshape(batch, WINDOW, KV_WIDTH),
                                         v_win.reshape(batch, WINDOW, KV_WIDTH),
                                         w['q_norm'], w['k_norm'], w['sinks'])
        ssm, h_re, h_im = s5_mix(u, h0_re.reshape(batch, S5_LANES), h0_im.reshape(batch, S5_LANES),
                                 w['s5'], rows=batch, seq=1, steps=1)
    x = matmul_residual(x, [attn, ssm], [w['w_out_a'], w['w_out_b']], tm=tm, tn=512)
    x = ffn(x, w['norm_ffn'], w['ffn_gate'], w['ffn_up'], w['ffn_down'], tm=tm, tf=512)
    shp = (batch, WINDOW, N_KV_HEADS, HEAD_DIM)
    st = (batch, S5_GROUPS, S5_STATE)
    return x, new_k.reshape(shp), new_v.reshape(shp), h_re.reshape(st), h_im.reshape(st)


def _odd_layer(x, cache, w, *, batch, seq):
    m = x.shape[0]
    tm = _tile(m, 1024)
    proj = norm_matmul(x, w['norm_mix'], w['w_in_main'], tm=tm, tn=1024)
    ba = norm_matmul(x, w['norm_mix'], w['w_in_ba'], tm=tm, tn=LANES)
    beta, eg, gcum = dn_gates(ba, w['a_log'], w['dt_bias'], tg=_tile(m, 512))
    if cache is None:
        proj3 = proj.reshape(batch, seq, DN_CONV_DIM + DN_V_W)
        qkvc = dn_conv_prompt(proj3, w['conv_w'], tl=256, tc=1024)
        gcum3 = gcum.reshape(batch, seq, DN_V_HEADS)
        o, s_new = dn_chunked(qkvc, proj3, gcum3, beta.reshape(batch, seq, DN_V_HEADS),
                              jnp.swapaxes(gcum3, 1, 2), w['out_norm'])
        o = o.reshape(m, DN_V_W)
        new_buf = proj3[:, seq - (DN_CONV_K - 1):, :DN_CONV_DIM]
    else:
        s0, conv_buf = cache
        o, s_new, new_buf = dn_sample(
            proj[:, :DN_CONV_DIM].reshape(batch, DN_ROWS, LANES),
            conv_buf.reshape(batch, DN_CONV_K - 1, DN_ROWS, LANES),
            w['conv_w'].reshape(DN_CONV_K, DN_ROWS, LANES),
            proj[:, DN_CONV_DIM:].reshape(batch, DN_V_HEADS, DN_V_DIM),
            eg, beta, s0, w['out_norm'])
        o = o.reshape(m, DN_V_W)
        new_buf = new_buf.reshape(batch, DN_CONV_K - 1, DN_CONV_DIM)
    x = matmul_residual(x, [o], [w['w_out']], tm=tm, tn=512)
    x = moe_sparse(x, w['norm_ffn'], w['router'], w['exp_gate'], w['exp_up'], w['exp_down'])
    return x, s_new, new_buf


def kernel(x_prompt, x_sample, cache_win_k, cache_win_v, state_s5_re, state_s5_im, state_dn, state_dn_conv,
           e_norm_mix, e_w_in, e_q_norm, e_k_norm, e_sinks,
           e_s5_a_re, e_s5_a_im, e_s5_log_dt, e_s5_b_re, e_s5_b_im, e_s5_c_re, e_s5_c_im, e_s5_d, e_s5_w_glu,
           e_w_out, e_norm_ffn, e_ffn_w_gate, e_ffn_w_up, e_ffn_w_down,
           o_norm_mix, o_w_in, o_conv_w, o_a_log, o_dt_bias, o_out_norm, o_w_out, o_norm_ffn,
           o_router, o_exp_w_gate, o_exp_w_up, o_exp_w_down):
    bp, lp, d = x_prompt.shape
    bs, ls, _ = x_sample.shape
    assert ls == 1, "the sample group advances one token per step"
    hp = x_prompt.reshape(bp * lp, d)
    hs = x_sample.reshape(bs * ls, d)
    qkv_w = ATTN_WIDTH + 2 * KV_WIDTH
    main_w = DN_CONV_DIM + DN_V_W

    j = 0
    we = dict(
        norm_mix=e_norm_mix[j], q_norm=e_q_norm[j], k_norm=e_k_norm[j], sinks=e_sinks[j],
        w_in_qkv=e_w_in[j, :, :qkv_w].astype(BF16), w_in_u=e_w_in[j, :, qkv_w:].astype(BF16),
        s5=s5_constants(e_s5_a_re[j], e_s5_a_im[j], e_s5_log_dt[j], e_s5_b_re[j], e_s5_b_im[j],
                        e_s5_c_re[j], e_s5_c_im[j], e_s5_d[j], e_s5_w_glu[j]),
        w_out_a=e_w_out[j, :ATTN_WIDTH].astype(BF16), w_out_b=e_w_out[j, ATTN_WIDTH:].astype(BF16),
        norm_ffn=e_norm_ffn[j], ffn_gate=e_ffn_w_gate[j].astype(BF16),
        ffn_up=e_ffn_w_up[j].astype(BF16), ffn_down=e_ffn_w_down[j].astype(BF16))
    ba_pad = jnp.zeros((d, LANES), F32).at[:, :2 * DN_V_HEADS].set(o_w_in[j, :, main_w:])
    wo = dict(
        norm_mix=o_norm_mix[j], w_in_main=o_w_in[j, :, :main_w].astype(BF16), w_in_ba=ba_pad.astype(BF16),
        conv_w=o_conv_w[j], a_log=o_a_log[j], dt_bias=o_dt_bias[j], out_norm=o_out_norm[j],
        w_out=o_w_out[j].astype(BF16), norm_ffn=o_norm_ffn[j], router=o_router[j],
        exp_gate=o_exp_w_gate[j].astype(BF16), exp_up=o_exp_w_up[j].astype(BF16),
        exp_down=o_exp_w_down[j].astype(BF16))

    hp, kp, vp, rp, ip = _even_layer(hp, None, we, batch=bp, seq=lp)
    hs, ks, vs, rs, is_ = _even_layer(
        hs, (cache_win_k[j], cache_win_v[j], state_s5_re[j], state_s5_im[j]), we, batch=bs, seq=1)
    hp, sp, cp = _odd_layer(hp, None, wo, batch=bp, seq=lp)
    hs, ss, cs = _odd_layer(hs, (state_dn[j], state_dn_conv[j]), wo, batch=bs, seq=1)

    one = lambda t: t[None]
    return (hp.reshape(bp, lp, d), hs.reshape(bs, ls, d),
            one(kp), one(vp), one(rp), one(ip), one(sp), one(cp),
            one(ks), one(vs), one(rs), one(is_), one(ss), one(cs))
```

```python
import functools

import jax
import jax.numpy as jnp
from jax import lax
from jax.experimental import pallas as pl
from jax.experimental.pallas import tpu as pltpu

F32 = jnp.float32
BF16 = jnp.bfloat16

D_MODEL = 2048
N_HEADS = 16
N_KV_HEADS = 4
HEAD_DIM = 64
Q_PER_KV = N_HEADS // N_KV_HEADS
WINDOW = 128
ATTN_WIDTH = N_HEADS * HEAD_DIM
KV_WIDTH = N_KV_HEADS * HEAD_DIM
ATTN_SCALE = HEAD_DIM ** -0.5
S5_WIDTH = D_MODEL // 2
S5_GROUP_CH = 16
S5_GROUPS = S5_WIDTH // S5_GROUP_CH
S5_STATE = 64
S5_LANES = S5_GROUPS * S5_STATE
DN_K_HEADS = 16
DN_V_HEADS = 32
DN_K_DIM = 128
DN_V_DIM = 128
DN_CONV_K = 4
DN_CHUNK = 64
DN_QK_W = DN_K_HEADS * DN_K_DIM
DN_V_W = DN_V_HEADS * DN_V_DIM
DN_CONV_DIM = 2 * DN_QK_W + DN_V_W
D_FF = 5632
N_EXPERTS = 8
EPS = 1e-6
NEG_INF = -1e30

LANES = 128
SUBLANES = 8
VMEM_LIMIT = 56 * 1024 * 1024


def _params(*sem):
    return pltpu.CompilerParams(dimension_semantics=sem, vmem_limit_bytes=VMEM_LIMIT)


def _rms(x, g):
    return x * lax.rsqrt(jnp.mean(x * x, axis=-1, keepdims=True) + EPS) * g


def _dot(a, b):
    return jnp.dot(a, b, preferred_element_type=F32)


def _dot_nt(a, b):
    return lax.dot_general(a, b, (((1,), (1,)), ((), ())), preferred_element_type=F32)


def _split3(x):
    hi = x.astype(BF16)
    r = x - hi.astype(F32)
    mid = r.astype(BF16)
    lo = (r - mid.astype(F32)).astype(BF16)
    return hi, mid, lo


def _norm_matmul_kernel(x_ref, g_ref, w_ref, o_ref, xn_ref):
    @pl.when(pl.program_id(1) == 0)
    def _():
        xn_ref[...] = _rms(x_ref[...], g_ref[...]).astype(BF16)

    o_ref[...] = _dot(xn_ref[...], w_ref[...]).astype(o_ref.dtype)


def norm_matmul(x, g, w, *, tm, tn, out_dtype=F32):
    m, d = x.shape
    n = w.shape[1]
    return pl.pallas_call(
        _norm_matmul_kernel,
        grid=(m // tm, n // tn),
        in_specs=[pl.BlockSpec((tm, d), lambda i, j: (i, 0)),
                  pl.BlockSpec((1, d), lambda i, j: (0, 0)),
                  pl.BlockSpec((d, tn), lambda i, j: (0, j))],
        out_specs=pl.BlockSpec((tm, tn), lambda i, j: (i, j)),
        out_shape=jax.ShapeDtypeStruct((m, n), out_dtype),
        scratch_shapes=[pltpu.VMEM((tm, d), BF16)],
        compiler_params=_params("parallel", "arbitrary"),
        name="norm_matmul",
    )(x, g.reshape(1, d), w)


def _matmul_residual_kernel(*refs, n_pairs):
    x_ref = refs[0]
    o_ref = refs[-1]
    acc = x_ref[...]
    for a_ref, w_ref in zip(refs[1:1 + n_pairs], refs[1 + n_pairs:1 + 2 * n_pairs]):
        acc = acc + _dot(a_ref[...], w_ref[...])
    o_ref[...] = acc


def matmul_residual(x, a_list, w_list, *, tm, tn):
    m, n = x.shape
    in_specs = [pl.BlockSpec((tm, tn), lambda i, j: (i, j))]
    in_specs += [pl.BlockSpec((tm, a.shape[1]), lambda i, j: (i, 0)) for a in a_list]
    in_specs += [pl.BlockSpec((w.shape[0], tn), lambda i, j: (0, j)) for w in w_list]
    return pl.pallas_call(
        functools.partial(_matmul_residual_kernel, n_pairs=len(a_list)),
        grid=(m // tm, n // tn),
        in_specs=in_specs,
        out_specs=pl.BlockSpec((tm, tn), lambda i, j: (i, j)),
        out_shape=jax.ShapeDtypeStruct((m, n), F32),
        compiler_params=_params("parallel", "arbitrary"),
        name="matmul_residual",
    )(x, *a_list, *w_list)


HEADS_PER_TILE = LANES // HEAD_DIM


def _head_rms(x, seg, g):
    hi, mid, lo = _split3(x * x)
    parts = []
    for a in range(x.shape[1] // LANES):
        sl = slice(a * LANES, (a + 1) * LANES)
        parts.append(_dot(hi[:, sl], seg) + (_dot(mid[:, sl], seg) + _dot(lo[:, sl], seg)))
    ms = jnp.concatenate(parts, axis=1) * (1.0 / HEAD_DIM)
    return x * lax.rsqrt(ms + EPS) * g


def _attn_prompt_kernel(q_ref, kc_ref, kp_ref, vc_ref, vp_ref, qn_ref, kn_ref, sink_ref,
                        o_ref, wk_ref, wv_ref, s_ref, p_ref):
    has_prev = pl.program_id(1) > 0
    r = lax.broadcasted_iota(jnp.int32, (LANES, LANES), 0)
    c = lax.broadcasted_iota(jnp.int32, (LANES, LANES), 1)
    seg = jnp.where(r // HEAD_DIM == c // HEAD_DIM, 1.0, 0.0).astype(BF16)
    low = lax.broadcasted_iota(jnp.int32, (1, LANES), 1) < HEAD_DIM

    qn = _head_rms(q_ref[...], seg, qn_ref[...]) * ATTN_SCALE
    kc = _head_rms(kc_ref[...], seg, kn_ref[...])
    kp = _head_rms(kp_ref[...], seg, kn_ref[...])
    vc = vc_ref[...]
    wk_ref[0] = kc
    wv_ref[0] = vc
    kband = jnp.concatenate([kp, kc], axis=0)
    vband = jnp.concatenate([vp_ref[...], vc], axis=0)

    v_halves = []
    for kvh in range(N_KV_HEADS):
        tl = slice(kvh // HEADS_PER_TILE * LANES, (kvh // HEADS_PER_TILE + 1) * LANES)

        def both_halves(t):
            rolled = pltpu.roll(t, HEAD_DIM, axis=1)
            return jnp.where(low, t, rolled) if kvh % HEADS_PER_TILE == 0 else jnp.where(low, rolled, t)

        kdup = both_halves(kband[:, tl]).astype(BF16)
        vdup = both_halves(vband[:, tl])
        v_halves.append((jnp.where(low, vdup, 0.0).astype(BF16), jnp.where(low, 0.0, vdup).astype(BF16)))
        for pr in range(Q_PER_KV // HEADS_PER_TILE):
            a = kvh * Q_PER_KV // HEADS_PER_TILE + pr
            qt = qn[:, a * LANES:(a + 1) * LANES]
            s_ref[HEADS_PER_TILE * a] = _dot_nt(jnp.where(low, qt, 0.0).astype(BF16), kdup)
            s_ref[HEADS_PER_TILE * a + 1] = _dot_nt(jnp.where(low, 0.0, qt).astype(BF16), kdup)

    row = lax.broadcasted_iota(jnp.int32, (WINDOW, 2 * WINDOW), 0)
    col = lax.broadcasted_iota(jnp.int32, (WINDOW, 2 * WINDOW), 1)
    rel = (WINDOW + row) - col
    mask = (rel >= 0) & (rel <= WINDOW) & ((col >= WINDOW) | has_prev)
    for h in range(N_HEADS):
        s = jnp.where(mask, s_ref[h], NEG_INF)
        sk = sink_ref[h:h + 1, :]
        m = jnp.maximum(jnp.max(s, axis=-1, keepdims=True), sk)
        p = jnp.exp(s - m)
        p = p / (jnp.sum(p, axis=-1, keepdims=True) + jnp.exp(sk - m))
        p_ref[h] = p.astype(BF16)

    for a in range(N_HEADS // HEADS_PER_TILE):
        v_lo, v_hi = v_halves[a * HEADS_PER_TILE // Q_PER_KV]
        o = _dot(p_ref[HEADS_PER_TILE * a], v_lo) + _dot(p_ref[HEADS_PER_TILE * a + 1], v_hi)
        o_ref[:, a * LANES:(a + 1) * LANES] = o.astype(o_ref.dtype)


def attn_prompt(qkv, q_norm, k_norm, sinks, *, batch, seq):
    nb = seq // WINDOW
    kcol = ATTN_WIDTH // KV_WIDTH
    cur = lambda c: (lambda b, n: (b * nb + n, c))
    prev = lambda c: (lambda b, n: (jnp.maximum(b * nb + n - 1, 0), c))
    return pl.pallas_call(
        _attn_prompt_kernel,
        grid=(batch, nb),
        in_specs=[pl.BlockSpec((WINDOW, ATTN_WIDTH), cur(0)),
                  pl.BlockSpec((WINDOW, KV_WIDTH), cur(kcol)),
                  pl.BlockSpec((WINDOW, KV_WIDTH), prev(kcol)),
                  pl.BlockSpec((WINDOW, KV_WIDTH), cur(kcol + 1)),
                  pl.BlockSpec((WINDOW, KV_WIDTH), prev(kcol + 1)),
                  pl.BlockSpec((1, ATTN_WIDTH), lambda b, n: (0, 0)),
                  pl.BlockSpec((1, KV_WIDTH), lambda b, n: (0, 0)),
                  pl.BlockSpec((N_HEADS, 1), lambda b, n: (0, 0))],
        out_specs=[pl.BlockSpec((WINDOW, ATTN_WIDTH), lambda b, n: (b * nb + n, 0)),
                   pl.BlockSpec((1, WINDOW, KV_WIDTH), lambda b, n: (b, 0, 0)),
                   pl.BlockSpec((1, WINDOW, KV_WIDTH), lambda b, n: (b, 0, 0))],
        out_shape=[jax.ShapeDtypeStruct((batch * seq, ATTN_WIDTH), BF16),
                   jax.ShapeDtypeStruct((batch, WINDOW, KV_WIDTH), F32),
                   jax.ShapeDtypeStruct((batch, WINDOW, KV_WIDTH), F32)],
        scratch_shapes=[pltpu.VMEM((N_HEADS, WINDOW, 2 * WINDOW), F32),
                        pltpu.VMEM((N_HEADS, WINDOW, 2 * WINDOW), BF16)],
        compiler_params=_params("parallel", "arbitrary"),
        name="attn_prompt",
    )(qkv, qkv, qkv, qkv, qkv, jnp.tile(q_norm, N_HEADS).reshape(1, ATTN_WIDTH),
      jnp.tile(k_norm, N_KV_HEADS).reshape(1, KV_WIDTH), sinks.reshape(N_HEADS, 1))


def _attn_sample_kernel(qkv_ref, ck_ref, cv_ref, qn_ref, kn_ref, sink_ref,
                        o_ref, wk_ref, wv_ref, *, bt):
    qg = qn_ref[...]
    kg = kn_ref[...]
    for b in range(bt):
        wk_ref[b, 0:WINDOW - 1, :] = ck_ref[b, 1:WINDOW, :]
        wv_ref[b, 0:WINDOW - 1, :] = cv_ref[b, 1:WINDOW, :]
        wv_ref[b, WINDOW - 1:WINDOW, :] = qkv_ref[b:b + 1, ATTN_WIDTH + KV_WIDTH:ATTN_WIDTH + 2 * KV_WIDTH]
        for kvh in range(N_KV_HEADS):
            sl = slice(kvh * HEAD_DIM, (kvh + 1) * HEAD_DIM)
            kn = _rms(qkv_ref[b:b + 1, ATTN_WIDTH + kvh * HEAD_DIM:ATTN_WIDTH + (kvh + 1) * HEAD_DIM], kg)
            vn = qkv_ref[b:b + 1, ATTN_WIDTH + KV_WIDTH + kvh * HEAD_DIM:
                         ATTN_WIDTH + KV_WIDTH + (kvh + 1) * HEAD_DIM]
            wk_ref[b, WINDOW - 1:WINDOW, sl] = kn
            q4 = jnp.concatenate(
                [qkv_ref[b:b + 1, (kvh * Q_PER_KV + g) * HEAD_DIM:(kvh * Q_PER_KV + g + 1) * HEAD_DIM]
                 for g in range(Q_PER_KV)], axis=0)
            q4 = _rms(q4, qg)
            kwin = ck_ref[b, :, sl].astype(BF16)
            vwin = cv_ref[b, :, sl].astype(BF16)
            s_c = _dot_nt(q4.astype(BF16), kwin) * ATTN_SCALE
            s_n = jnp.sum(q4 * kn, axis=-1, keepdims=True) * ATTN_SCALE
            sk = sink_ref[kvh * Q_PER_KV:(kvh + 1) * Q_PER_KV, :]
            m = jnp.maximum(jnp.maximum(jnp.max(s_c, axis=-1, keepdims=True), s_n), sk)
            p_c = jnp.exp(s_c - m)
            p_n = jnp.exp(s_n - m)
            den = jnp.sum(p_c, axis=-1, keepdims=True) + p_n + jnp.exp(sk - m)
            o4 = _dot((p_c / den).astype(BF16), vwin) + (p_n / den) * vn
            for g in range(Q_PER_KV):
                h = kvh * Q_PER_KV + g
                o_ref[b:b + 1, h * HEAD_DIM:(h + 1) * HEAD_DIM] = o4[g:g + 1, :].astype(o_ref.dtype)


def attn_sample(qkv, cache_k, cache_v, q_norm, k_norm, sinks, *, bt=8):
    nb = qkv.shape[0]
    width = qkv.shape[1]
    win = pl.BlockSpec((bt, WINDOW, KV_WIDTH), lambda i: (i, 0, 0))
    return pl.pallas_call(
        functools.partial(_attn_sample_kernel, bt=bt),
        grid=(nb // bt,),
        in_specs=[pl.BlockSpec((bt, width), lambda i: (i, 0)), win, win,
                  pl.BlockSpec((1, HEAD_DIM), lambda i: (0, 0)),
                  pl.BlockSpec((1, HEAD_DIM), lambda i: (0, 0)),
                  pl.BlockSpec((N_HEADS, 1), lambda i: (0, 0))],
        out_specs=[pl.BlockSpec((bt, ATTN_WIDTH), lambda i: (i, 0)), win, win],
        out_shape=[jax.ShapeDtypeStruct((nb, ATTN_WIDTH), BF16),
                   jax.ShapeDtypeStruct((nb, WINDOW, KV_WIDTH), F32),
                   jax.ShapeDtypeStruct((nb, WINDOW, KV_WIDTH), F32)],
        compiler_params=_params("parallel"),
        name="attn_sample",
    )(qkv, cache_k, cache_v, q_norm.reshape(1, HEAD_DIM), k_norm.reshape(1, HEAD_DIM),
      sinks.reshape(N_HEADS, 1))


S5_KCH = S5_WIDTH // LANES
S5_CHUNK_STATES = S5_LANES // S5_KCH
S5_SCAN_TILES = 4


def _s5_kernel(u_ref, h0re_ref, h0im_ref, are_ref, aim_ref, wbu_ref, wc_ref, d_ref, wglu_ref,
               y_ref, hre_ref, him_ref, hs_ref, st_ref, *, rows, steps):
    rt = rows * steps
    u = u_ref[...].reshape(rt, S5_WIDTH)

    @pl.when(pl.program_id(0) == 0)
    def _():
        st_ref[:, :S5_LANES] = h0re_ref[...]
        st_ref[:, S5_LANES:] = h0im_ref[...]

    if steps > 1:
        assert rows == SUBLANES
        r = lax.broadcasted_iota(jnp.int32, (rt, rt), 0)
        c = lax.broadcasted_iota(jnp.int32, (rt, rt), 1)
        to_time_major = jnp.where(c == (r % rows) * steps + r // rows, 1.0, 0.0).astype(BF16)
        to_seq_major = jnp.where(c == (r % steps) * rows + r // steps, 1.0, 0.0).astype(BF16)
        hi, mid, lo = _split3(u)
        u_hi = _dot(to_time_major, hi)
        u = u_hi + (_dot(to_time_major, mid) + _dot(to_time_major, lo))
        ub = u_hi.astype(BF16)
    else:
        ub = u.astype(BF16)
    nre = S5_LANES // LANES
    tpk = S5_CHUNK_STATES // LANES
    for k in range(S5_KCH):
        r = _dot(ub[:, k * LANES:(k + 1) * LANES], wbu_ref[k])
        for a in range(tpk):
            hs_ref[k * tpk + a] = r[:, a * LANES:(a + 1) * LANES]
            hs_ref[nre + k * tpk + a] = r[:, (tpk + a) * LANES:(tpk + a + 1) * LANES]

    nt = S5_SCAN_TILES
    for j in range(nre // nt):
        tiles = range(j * nt, (j + 1) * nt)
        ar = [jnp.broadcast_to(are_ref[:, a * LANES:(a + 1) * LANES], (SUBLANES, LANES)) for a in tiles]
        ai = [jnp.broadcast_to(aim_ref[:, a * LANES:(a + 1) * LANES], (SUBLANES, LANES)) for a in tiles]

        def advance(r8, hr, hi):
            nr, ni = [], []
            for n, a in enumerate(tiles):
                nr.append(ar[n] * hr[n] - ai[n] * hi[n] + hs_ref[a, r8, :])
                ni.append(ar[n] * hi[n] + ai[n] * hr[n] + hs_ref[nre + a, r8, :])
                hs_ref[a, r8, :] = nr[n]
                hs_ref[nre + a, r8, :] = ni[n]
            return nr, ni

        if steps == 1:
            def group(rg, carry):
                r8 = pl.ds(pl.multiple_of(rg * SUBLANES, SUBLANES), SUBLANES)
                hr = [st_ref[r8, a * LANES:(a + 1) * LANES] for a in tiles]
                hi = [st_ref[r8, S5_LANES + a * LANES:S5_LANES + (a + 1) * LANES] for a in tiles]
                nr, ni = advance(r8, hr, hi)
                for n, a in enumerate(tiles):
                    st_ref[r8, a * LANES:(a + 1) * LANES] = nr[n]
                    st_ref[r8, S5_LANES + a * LANES:S5_LANES + (a + 1) * LANES] = ni[n]
                return carry
            lax.fori_loop(0, rows // SUBLANES, group, 0)
        else:
            def step(t, carry):
                nr, ni = advance(pl.ds(pl.multiple_of(t * SUBLANES, SUBLANES), SUBLANES), *carry)
                return tuple(nr), tuple(ni)
            hr0 = tuple(st_ref[:, a * LANES:(a + 1) * LANES] for a in tiles)
            hi0 = tuple(st_ref[:, S5_LANES + a * LANES:S5_LANES + (a + 1) * LANES] for a in tiles)
            hr, hi = lax.fori_loop(0, steps, step, (hr0, hi0))
            for n, a in enumerate(tiles):
                st_ref[:, a * LANES:(a + 1) * LANES] = hr[n]
                st_ref[:, S5_LANES + a * LANES:S5_LANES + (a + 1) * LANES] = hi[n]

    ys = []
    for k in range(S5_KCH):
        hre = jnp.concatenate([hs_ref[k * tpk + a] for a in range(tpk)], axis=1).astype(BF16)
        him = jnp.concatenate([hs_ref[nre + k * tpk + a] for a in range(tpk)], axis=1).astype(BF16)
        yk = _dot(hre, wc_ref[0, k]) + _dot(him, wc_ref[1, k])
        ys.append(yk + d_ref[:, k * LANES:(k + 1) * LANES] * u[:, k * LANES:(k + 1) * LANES])
    y = jax.nn.gelu(jnp.concatenate(ys, axis=1))
    out = (y * jax.nn.sigmoid(_dot(y.astype(BF16), wglu_ref[...]))).astype(y_ref.dtype)
    if steps > 1:
        out = _dot(to_seq_major, out).astype(y_ref.dtype)
    y_ref[...] = out.reshape(y_ref.shape)

    @pl.when(pl.program_id(0) == pl.num_programs(0) - 1)
    def _():
        hre_ref[...] = st_ref[:, :S5_LANES]
        him_ref[...] = st_ref[:, S5_LANES:]


def s5_mix(u, h0_re, h0_im, consts, *, rows, seq, steps):
    a_re, a_im, w_bu, w_c, d_skip, w_glu = consts
    if seq > 1:
        u_spec = pl.BlockSpec((rows, steps, S5_WIDTH), lambda c: (0, c, 0))
    else:
        u_spec = pl.BlockSpec((rows, S5_WIDTH), lambda c: (0, 0))
    full = lambda shape: pl.BlockSpec(shape, lambda c: (0,) * len(shape))
    return pl.pallas_call(
        functools.partial(_s5_kernel, rows=rows, steps=steps),
        grid=(seq // steps,),
        in_specs=[u_spec, full((rows, S5_LANES)), full((rows, S5_LANES)),
                  full((1, S5_LANES)), full((1, S5_LANES)),
                  full(w_bu.shape), full(w_c.shape), full((1, S5_WIDTH)), full(w_glu.shape)],
        out_specs=[u_spec, full((rows, S5_LANES)), full((rows, S5_LANES))],
        out_shape=[jax.ShapeDtypeStruct(u.shape, BF16),
                   jax.ShapeDtypeStruct((rows, S5_LANES), F32),
                   jax.ShapeDtypeStruct((rows, S5_LANES), F32)],
        scratch_shapes=[pltpu.VMEM((2 * S5_LANES // LANES, rows * steps, LANES), F32),
                        pltpu.VMEM((rows, 2 * S5_LANES), F32)],
        compiler_params=_params("arbitrary"),
        name="s5_mix",
    )(u, h0_re, h0_im, a_re, a_im, w_bu, w_c, d_skip, w_glu)


def s5_constants(a_re, a_im, log_dt, b_re, b_im, c_re, c_im, d_skip, w_glu):
    lr, li = a_re, a_im
    dt = jnp.exp(log_dt)[:, None]
    mag = jnp.exp(lr * dt)
    ab_re, ab_im = mag * jnp.cos(li * dt), mag * jnp.sin(li * dt)
    den = lr * lr + li * li
    f_re = ((ab_re - 1.0) * lr + ab_im * li) / den
    f_im = (ab_im * lr - (ab_re - 1.0) * li) / den
    bb_re = f_re[..., None] * b_re - f_im[..., None] * b_im
    bb_im = f_re[..., None] * b_im + f_im[..., None] * b_re
    gpc = LANES // S5_GROUP_CH
    eye = jnp.eye(gpc, dtype=F32)

    def bu_blocks(bb):
        t = bb.reshape(S5_KCH, gpc, S5_STATE, S5_GROUP_CH)
        return jnp.einsum('kgpc,gh->kgchp', t, eye).reshape(S5_KCH, LANES, gpc * S5_STATE)

    def c_blocks(c):
        t = c.reshape(S5_KCH, gpc, S5_GROUP_CH, S5_STATE)
        return jnp.einsum('kgcp,gh->kgphc', t, eye).reshape(S5_KCH, gpc * S5_STATE, LANES)

    w_bu = jnp.concatenate([bu_blocks(bb_re), bu_blocks(bb_im)], axis=-1).astype(BF16)
    w_c = jnp.stack([c_blocks(c_re), -c_blocks(c_im)]).astype(BF16)
    return (ab_re.reshape(1, S5_LANES), ab_im.reshape(1, S5_LANES), w_bu, w_c,
            d_skip.reshape(1, S5_WIDTH), w_glu.astype(BF16))


def _ffn_kernel(x_ref, g_ref, wg_ref, wu_ref, wd_ref, o_ref, xn_ref):
    @pl.when(pl.program_id(1) == 0)
    def _():
        x = x_ref[...]
        xn_ref[...] = _rms(x, g_ref[...]).astype(BF16)
        o_ref[...] = x

    xn = xn_ref[...]
    h = jax.nn.silu(_dot(xn, wg_ref[...])) * _dot(xn, wu_ref[...])
    o_ref[...] += _dot(h.astype(BF16), wd_ref[...])


def ffn(x, g, w_gate, w_up, w_down, *, tm, tf):
    m, d = x.shape
    f = w_gate.shape[1]
    return pl.pallas_call(
        _ffn_kernel,
        grid=(m // tm, f // tf),
        in_specs=[pl.BlockSpec((tm, d), lambda i, j: (i, 0), pipeline_mode=pl.Buffered(1)),
                  pl.BlockSpec((1, d), lambda i, j: (0, 0)),
                  pl.BlockSpec((d, tf), lambda i, j: (0, j)),
                  pl.BlockSpec((d, tf), lambda i, j: (0, j)),
                  pl.BlockSpec((tf, d), lambda i, j: (j, 0))],
        out_specs=pl.BlockSpec((tm, d), lambda i, j: (i, 0)),
        out_shape=jax.ShapeDtypeStruct((m, d), F32),
        scratch_shapes=[pltpu.VMEM((tm, d), BF16)],
        compiler_params=_params("parallel", "arbitrary"),
        name="ffn",
    )(x, g.reshape(1, d), w_gate, w_up, w_down)


MOE_TILE = 512
META_I1, META_I2, META_W1, META_W2, META_R1, META_R2 = range(6)


def _router_kernel(x_ref, g_ref, r_ref, xn_ref, meta_ref, cnt_ref, carry_ref):
    @pl.when(pl.program_id(0) == 0)
    def _():
        carry_ref[...] = jnp.zeros_like(carry_ref)

    xn = _rms(x_ref[...], g_ref[...])
    xn_ref[...] = xn
    x_hi, x_mid, _ = _split3(xn)
    r_hi, r_mid, _ = _split3(r_ref[...])
    logits = _dot(x_hi, r_hi) + (_dot(x_mid, r_hi) + _dot(x_hi, r_mid))
    lane = lax.broadcasted_iota(jnp.int32, logits.shape, 1)
    neg = -jnp.inf
    l1 = jnp.where(lane < N_EXPERTS, logits, neg)
    m1 = jnp.max(l1, axis=-1, keepdims=True)
    i1 = jnp.min(jnp.where(l1 == m1, lane, LANES), axis=-1, keepdims=True)
    l2 = jnp.where(lane == i1, neg, l1)
    m2 = jnp.max(l2, axis=-1, keepdims=True)
    i2 = jnp.min(jnp.where(l2 == m2, lane, LANES), axis=-1, keepdims=True)
    e = jnp.exp(m2 - m1)
    den = 1.0 + e
    sel = jnp.where(lane == i1, 1.0, jnp.where(lane == i2, 1.0, 0.0))
    tm = sel.shape[0]
    row = lax.broadcasted_iota(jnp.int32, (tm, tm), 0)
    col = lax.broadcasted_iota(jnp.int32, (tm, tm), 1)
    incl = _dot(jnp.where(col <= row, 1.0, 0.0).astype(BF16), sel.astype(BF16))
    excl = incl - sel + carry_ref[0:1, :]
    rank1 = jnp.sum(jnp.where(lane == i1, excl, 0.0), axis=-1, keepdims=True)
    rank2 = jnp.sum(jnp.where(lane == i2, excl, 0.0), axis=-1, keepdims=True)
    carry_ref[...] = carry_ref[...] + incl[tm - 1:tm, :]
    cnt_ref[...] = carry_ref[...]
    fields = {META_I1: i1.astype(F32), META_I2: i2.astype(F32), META_W1: 1.0 / den, META_W2: e / den,
              META_R1: rank1, META_R2: rank2}
    meta = jnp.zeros(logits.shape, F32)
    for k, val in fields.items():
        meta = jnp.where(lane == k, val, meta)
    meta_ref[...] = meta


def router(x, g, r, *, tm):
    m, d = x.shape
    r_pad = jnp.zeros((d, LANES), F32).at[:, :N_EXPERTS].set(r)
    return pl.pallas_call(
        _router_kernel,
        grid=(m // tm,),
        in_specs=[pl.BlockSpec((tm, d), lambda i: (i, 0)),
                  pl.BlockSpec((1, d), lambda i: (0, 0)),
                  pl.BlockSpec((d, LANES), lambda i: (0, 0))],
        out_specs=[pl.BlockSpec((tm, d), lambda i: (i, 0)),
                   pl.BlockSpec((tm, LANES), lambda i: (i, 0)),
                   pl.BlockSpec((SUBLANES, LANES), lambda i: (0, 0))],
        out_shape=[jax.ShapeDtypeStruct((m, d), F32),
                   jax.ShapeDtypeStruct((m, LANES), F32),
                   jax.ShapeDtypeStruct((SUBLANES, LANES), F32)],
        scratch_shapes=[pltpu.VMEM((SUBLANES, LANES), F32)],
        compiler_params=_params("arbitrary"),
        name="router",
    )(x, g.reshape(1, d), r_pad)


def _dispatch_kernel(p1_ref, p2_ref, xn_ref, zeros_ref, xs_ref, sem, *, td):
    del zeros_ref
    base = pl.program_id(0) * td

    def copies(r):
        src = xn_ref.at[pl.ds(r, 1), :]
        return [pltpu.make_async_copy(src, xs_ref.at[pl.ds(p_ref[base + r], 1), :], sem)
                for p_ref in (p1_ref, p2_ref)]

    def issue(r, carry):
        for c in copies(r):
            c.start()
        return carry

    def drain(r, carry):
        for c in copies(r):
            c.wait()
        return carry

    lax.fori_loop(0, td, issue, 0)
    lax.fori_loop(0, td, drain, 0)


def moe_dispatch(xn, pos1, pos2, *, rows, td):
    m, d = xn.shape
    return pl.pallas_call(
        functools.partial(_dispatch_kernel, td=td),
        grid_spec=pltpu.PrefetchScalarGridSpec(
            num_scalar_prefetch=2,
            grid=(m // td,),
            in_specs=[pl.BlockSpec((td, d), lambda i, p1, p2: (i, 0)),
                      pl.BlockSpec(memory_space=pl.ANY)],
            out_specs=pl.BlockSpec(memory_space=pl.ANY),
            scratch_shapes=[pltpu.SemaphoreType.DMA(())]),
        out_shape=jax.ShapeDtypeStruct((rows, d), F32),
        input_output_aliases={3: 0},
        compiler_params=_params("arbitrary"),
        name="moe_dispatch",
    )(pos1, pos2, xn, jnp.zeros((rows, d), F32))


def _expert_kernel(te_ref, tv_ref, xs_ref, wg_ref, wu_ref, wd_ref, ys_ref, xb_ref):
    del te_ref
    f = pl.program_id(1)
    valid = tv_ref[pl.program_id(0)]

    @pl.when(valid > 0)
    def _():
        @pl.when(f == 0)
        def _():
            xb_ref[...] = xs_ref[...].astype(BF16)

        xb = xb_ref[...]
        h = jax.nn.silu(_dot(xb, wg_ref[0])) * _dot(xb, wu_ref[0])
        y = _dot(h.astype(BF16), wd_ref[0])

        @pl.when(f == 0)
        def _():
            ys_ref[...] = y

        @pl.when(f > 0)
        def _():
            ys_ref[...] += y

    @pl.when((valid == 0) & (f == 0))
    def _():
        ys_ref[...] = jnp.zeros_like(ys_ref)


def moe_experts(xs, tile_expert, tile_valid, w_gate, w_up, w_down, *, tile, tf):
    rows, d = xs.shape
    _, _, f = w_gate.shape
    nf = f // tf
    fidx = lambda t, j, tv: jnp.where(tv[t] > 0, j, nf - 1)
    return pl.pallas_call(
        _expert_kernel,
        grid_spec=pltpu.PrefetchScalarGridSpec(
            num_scalar_prefetch=2,
            grid=(rows // tile, nf),
            in_specs=[pl.BlockSpec((tile, d), lambda t, j, te, tv: (t, 0)),
                      pl.BlockSpec((1, d, tf), lambda t, j, te, tv: (te[t], 0, fidx(t, j, tv))),
                      pl.BlockSpec((1, d, tf), lambda t, j, te, tv: (te[t], 0, fidx(t, j, tv))),
                      pl.BlockSpec((1, tf, d), lambda t, j, te, tv: (te[t], fidx(t, j, tv), 0))],
            out_specs=pl.BlockSpec((tile, d), lambda t, j, te, tv: (t, 0)),
            scratch_shapes=[pltpu.VMEM((tile, d), BF16)]),
        out_shape=jax.ShapeDtypeStruct((rows, d), F32),
        compiler_params=_params("arbitrary", "arbitrary"),
        name="moe_experts",
    )(tile_expert, tile_valid, xs, w_gate, w_up, w_down)


def _combine_kernel(p1_ref, p2_ref, x_ref, meta_ref, ys_ref, o_ref, buf_ref, sem, *, tc):
    i = pl.program_id(0)
    slot = i % 2

    def copies(step, sl, r):
        tok = step * tc + r
        return [pltpu.make_async_copy(ys_ref.at[pl.ds(p_ref[tok], 1), :],
                                      buf_ref.at[sl, k, pl.ds(r, 1), :], sem.at[sl])
                for k, p_ref in enumerate((p1_ref, p2_ref))]

    def issue(step, sl):
        def body(r, carry):
            for c in copies(step, sl, r):
                c.start()
            return carry
        lax.fori_loop(0, tc, body, 0)

    @pl.when(i == 0)
    def _():
        issue(0, 0)

    @pl.when(i + 1 < pl.num_programs(0))
    def _():
        issue(i + 1, 1 - slot)

    def drain(r, carry):
        for c in copies(i, slot, r):
            c.wait()
        return carry
    lax.fori_loop(0, tc, drain, 0)

    meta = meta_ref[...]
    w1 = meta[:, META_W1:META_W1 + 1]
    w2 = meta[:, META_W2:META_W2 + 1]
    o_ref[...] = x_ref[...] + w1 * buf_ref[slot, 0] + w2 * buf_ref[slot, 1]


def moe_combine(x, meta, ys, pos1, pos2, *, tc):
    m, d = x.shape
    return pl.pallas_call(
        functools.partial(_combine_kernel, tc=tc),
        grid_spec=pltpu.PrefetchScalarGridSpec(
            num_scalar_prefetch=2,
            grid=(m // tc,),
            in_specs=[pl.BlockSpec((tc, d), lambda i, p1, p2: (i, 0)),
                      pl.BlockSpec((tc, LANES), lambda i, p1, p2: (i, 0)),
                      pl.BlockSpec(memory_space=pl.ANY)],
            out_specs=pl.BlockSpec((tc, d), lambda i, p1, p2: (i, 0)),
            scratch_shapes=[pltpu.VMEM((2, 2, tc, d), F32), pltpu.SemaphoreType.DMA((2,))]),
        out_shape=jax.ShapeDtypeStruct((m, d), F32),
        compiler_params=_params("arbitrary"),
        name="moe_combine",
    )(pos1, pos2, x, meta, ys)


def moe_sparse(x, g, r, w_gate, w_up, w_down):
    m, d = x.shape
    tm = min(m, 512)
    tt = min(m, 256)
    tile = min(m, MOE_TILE)
    xn, meta, counts = router(x, g, r, tm=tm)
    cnt = counts[0, :N_EXPERTS].astype(jnp.int32)
    tiles_e = (cnt + tile - 1) // tile
    tile_end = jnp.cumsum(tiles_e)
    tile_start = tile_end - tiles_e
    n_tiles = (2 * m) // tile + N_EXPERTS
    t_all = jnp.arange(n_tiles, dtype=jnp.int32)
    t = jnp.minimum(t_all, tile_end[-1] - 1)
    tile_expert = jnp.sum(t[:, None] >= tile_end[None, :], axis=1).astype(jnp.int32)
    tile_valid = jnp.clip(cnt[tile_expert] - (t - tile_start[tile_expert]) * tile, 0, tile)
    tile_valid = jnp.where(t_all < tile_end[-1], tile_valid, 0).astype(jnp.int32)
    off = tile_start * tile
    pos1 = off[meta[:, META_I1].astype(jnp.int32)] + meta[:, META_R1].astype(jnp.int32)
    pos2 = off[meta[:, META_I2].astype(jnp.int32)] + meta[:, META_R2].astype(jnp.int32)
    xs = moe_dispatch(xn, pos1, pos2, rows=n_tiles * tile, td=tt)
    ys = moe_experts(xs, tile_expert, tile_valid, w_gate, w_up, w_down, tile=tile, tf=512)
    return moe_combine(x, meta, ys, pos1, pos2, tc=tt)


def _gates_kernel(ba_ref, alog_ref, dtb_ref, beta_ref, eg_ref, gcum_ref, *, tg):
    ba = ba_ref[...]
    b = ba[:, :DN_V_HEADS]
    a = ba[:, DN_V_HEADS:2 * DN_V_HEADS]
    beta_ref[...] = jax.nn.sigmoid(b)
    z = a + dtb_ref[...]
    softplus = jnp.maximum(z, 0.0) + jnp.log1p(jnp.exp(-jnp.abs(z)))
    g = -jnp.exp(alog_ref[...]) * softplus
    eg_ref[...] = jnp.exp(g)
    r = lax.broadcasted_iota(jnp.int32, (tg, tg), 0)
    c = lax.broadcasted_iota(jnp.int32, (tg, tg), 1)
    tri = jnp.where((c <= r) & (r // DN_CHUNK == c // DN_CHUNK), 1.0, 0.0).astype(BF16)
    hi, mid, lo = _split3(g)
    gcum_ref[...] = _dot(tri, hi) + (_dot(tri, mid) + _dot(tri, lo))


def dn_gates(ba, a_log, dt_bias, *, tg):
    m = ba.shape[0]
    out = jax.ShapeDtypeStruct((m, DN_V_HEADS), F32)
    spec = pl.BlockSpec((tg, DN_V_HEADS), lambda i: (i, 0))
    return pl.pallas_call(
        functools.partial(_gates_kernel, tg=tg),
        grid=(m // tg,),
        in_specs=[pl.BlockSpec((tg, LANES), lambda i: (i, 0)),
                  pl.BlockSpec((1, DN_V_HEADS), lambda i: (0, 0)),
                  pl.BlockSpec((1, DN_V_HEADS), lambda i: (0, 0))],
        out_specs=[spec, spec, spec],
        out_shape=[out, out, out],
        compiler_params=_params("parallel"),
        name="dn_gates",
    )(ba, a_log.reshape(1, DN_V_HEADS), dt_bias.reshape(1, DN_V_HEADS))


def _l2norm_heads(c, scale):
    parts = []
    for h in range(c.shape[1] // DN_K_DIM):
        t = c[:, h * DN_K_DIM:(h + 1) * DN_K_DIM]
        t = t * lax.rsqrt(jnp.sum(t * t, axis=-1, keepdims=True) + EPS)
        parts.append(t * scale if scale != 1.0 else t)
    return jnp.concatenate(parts, axis=1)


def _conv_prompt_kernel(x_ref, halo_ref, w_ref, o_ref, *, tl, tc):
    j = pl.program_id(2)
    halo = jnp.where(pl.program_id(1) > 0, halo_ref[0], 0.0)
    xcat = jnp.concatenate([halo, x_ref[0]], axis=0)
    conv = xcat[SUBLANES - 3:SUBLANES - 3 + tl] * w_ref[0:1, :]
    for i in range(1, DN_CONV_K):
        conv = conv + xcat[SUBLANES - 3 + i:SUBLANES - 3 + i + tl] * w_ref[i:i + 1, :]
    c = jax.nn.silu(conv)
    nq = DN_QK_W // tc

    @pl.when(j < nq)
    def _():
        o_ref[0] = _l2norm_heads(c, DN_K_DIM ** -0.5)

    @pl.when((j >= nq) & (j < 2 * nq))
    def _():
        o_ref[0] = _l2norm_heads(c, 1.0)

    @pl.when(j >= 2 * nq)
    def _():
        o_ref[0] = c


def dn_conv_prompt(proj, conv_w, *, tl, tc):
    b, l, _ = proj.shape
    hb = tl // SUBLANES
    return pl.pallas_call(
        functools.partial(_conv_prompt_kernel, tl=tl, tc=tc),
        grid=(b, l // tl, DN_CONV_DIM // tc),
        in_specs=[pl.BlockSpec((1, tl, tc), lambda i, t, j: (i, t, j)),
                  pl.BlockSpec((1, SUBLANES, tc), lambda i, t, j: (i, jnp.maximum(t * hb - 1, 0), j)),
                  pl.BlockSpec((DN_CONV_K, tc), lambda i, t, j: (0, j))],
        out_specs=pl.BlockSpec((1, tl, tc), lambda i, t, j: (i, t, j)),
        out_shape=jax.ShapeDtypeStruct((b, l, DN_CONV_DIM), F32),
        compiler_params=_params("parallel", "parallel", "arbitrary"),
        name="dn_conv_prompt",
    )(proj, proj, conv_w)


DN_GROUP = 4
DN_LBLOCK = 512
DN_BATCH = 8
DN_HBATCH = 4


def _dn_chunk_kernel(q_ref, k_ref, v_ref, z_ref, gc_ref, beta_ref, gt_ref, onorm_ref,
                     o_ref, s_out_ref, s_ref, xs_ref, add_ref, oacc_ref, egl_ref, *, nc):
    hg = pl.program_id(1)
    lb = pl.program_id(2)
    cz = DN_CHUNK

    @pl.when(lb == 0)
    def _():
        s_ref[...] = jnp.zeros_like(s_ref)

    gc_all = gc_ref[0]
    beta_all = beta_ref[0]
    lane = lax.broadcasted_iota(jnp.int32, gc_all.shape, 1)
    ri = lax.broadcasted_iota(jnp.int32, (cz, cz), 0)
    ci = lax.broadcasted_iota(jnp.int32, (cz, cz), 1)
    incl = (ri >= ci)[None]
    strict = (ri > ci)[None]

    rep = DN_V_HEADS // DN_K_HEADS
    for g0 in range(0, DN_GROUP, DN_HBATCH):
        gs = list(range(g0, g0 + DN_HBATCH))
        gcol_all, bcol_all, grow_all = {}, {}, {}
        for g in gs:
            head = hg * DN_GROUP + g
            gcol_all[g] = jnp.sum(jnp.where(lane == head, gc_all, 0.0), axis=-1, keepdims=True)
            bcol_all[g] = jnp.sum(jnp.where(lane == head, beta_all, 0.0), axis=-1, keepdims=True)
            grow_all[g] = gt_ref[0, pl.ds(head, 1), :]
        for c0 in range(0, nc, DN_BATCH):
            cb = min(DN_BATCH, nc - c0)
            rows = slice(c0 * cz, (c0 + cb) * cz)
            stack = lambda fn: jnp.concatenate([fn(g) for g in gs], axis=0)
            kcols = lambda g: slice(g // rep * DN_K_DIM, (g // rep + 1) * DN_K_DIM)
            q = stack(lambda g: q_ref[0, rows, kcols(g)].reshape(cb, cz, DN_K_DIM))
            k = stack(lambda g: k_ref[0, rows, kcols(g)].reshape(cb, cz, DN_K_DIM))
            v = stack(lambda g: v_ref[0, rows, g * DN_V_DIM:(g + 1) * DN_V_DIM].reshape(cb, cz, DN_V_DIM))
            gcol = stack(lambda g: gcol_all[g][rows].reshape(cb, cz, 1))
            bcol = stack(lambda g: bcol_all[g][rows].reshape(cb, cz, 1))
            grow = stack(lambda g: jnp.stack([grow_all[g][:, c * cz:(c + 1) * cz]
                                              for c in range(c0, c0 + cb)]))
            decay = jnp.where(incl, jnp.exp(gcol - grow), 0.0)
            kb = k * bcol
            kbf = k.astype(BF16)
            kk = jnp.einsum('cid,cjd->cij', kb.astype(BF16), kbf, preferred_element_type=F32)
            neg_l = jnp.where(strict, -(kk * decay), 0.0)
            n_acc = neg_l
            pw = neg_l
            for _ in range(5):
                pwb = pw.astype(BF16)
                pw = jnp.einsum('cij,cjk->cik', pwb, pwb, preferred_element_type=F32)
                n_acc = n_acc + pw + jnp.einsum('cij,cjk->cik', n_acc.astype(BF16), pw.astype(BF16),
                                                preferred_element_type=F32)
            egc = jnp.exp(gcol)
            rhs = jnp.concatenate([kb * egc, v * bcol], axis=-1)
            bmm = lambda x, y: jnp.einsum('cij,cjd->cid', x, y, preferred_element_type=F32)
            wu = (rhs + bmm(n_acc.astype(BF16), rhs.astype(BF16))).astype(BF16)
            qk = jnp.einsum('cid,cjd->cij', q.astype(BF16), kbf, preferred_element_type=F32) * decay
            a_wu = bmm(qk.astype(BF16), wu)
            glast = gcol[:, cz - 1:cz, :]
            kd = k * jnp.exp(glast - gcol)
            qa = (q * egc - a_wu[..., :DN_K_DIM]).astype(BF16)
            egl = jnp.broadcast_to(jnp.exp(glast), (len(gs) * cb, 1, DN_V_DIM))
            for n, g in enumerate(gs):
                for c in range(cb):
                    kd_wu = _dot(jnp.transpose(kd[n * cb + c]).astype(BF16), wu[n * cb + c])
                    xs_ref[g, c0 + c, 0:DN_K_DIM, :] = (-kd_wu[:, :DN_K_DIM]).astype(BF16)
                    add_ref[g, c0 + c, 0:DN_K_DIM, :] = kd_wu[:, DN_K_DIM:]
                xs_ref[g, c0:c0 + cb, DN_K_DIM:DN_K_DIM + cz, :] = qa[n * cb:(n + 1) * cb]
                add_ref[g, c0:c0 + cb, DN_K_DIM:DN_K_DIM + cz, :] = a_wu[n * cb:(n + 1) * cb, :, DN_K_DIM:]
                egl_ref[g, c0:c0 + cb] = egl[n * cb:(n + 1) * cb]

    def chunk(c, carry):
        for g in range(DN_GROUP):
            s = s_ref[g]
            r = _dot(xs_ref[g, c], s.astype(BF16)) + add_ref[g, c]
            oacc_ref[g, pl.ds(pl.multiple_of(c * cz, cz), cz), :] = r[DN_K_DIM:DN_K_DIM + cz]
            s_ref[g] = s * egl_ref[g, c] + r[0:DN_K_DIM]
        return carry

    lax.fori_loop(0, nc, chunk, 0)

    for g in range(DN_GROUP):
        o = oacc_ref[g]
        z = z_ref[0, :, g * DN_V_DIM:(g + 1) * DN_V_DIM]
        o_ref[0, :, g * DN_V_DIM:(g + 1) * DN_V_DIM] = (
            _rms(o, onorm_ref[...]) * jax.nn.silu(z)).astype(o_ref.dtype)

    @pl.when(lb == pl.num_programs(2) - 1)
    def _():
        s_out_ref[0] = s_ref[...]


def dn_chunked(qkvc, proj, gcum, beta, gcum_t, out_norm):
    b, l, _ = qkvc.shape
    lbk = min(DN_LBLOCK, l)
    nc = lbk // DN_CHUNK
    gk = DN_GROUP // 2 * DN_K_DIM
    gv = DN_GROUP * DN_V_DIM
    cz = DN_CHUNK
    return pl.pallas_call(
        functools.partial(_dn_chunk_kernel, nc=nc),
        grid=(b, DN_V_HEADS // DN_GROUP, l // lbk),
        in_specs=[pl.BlockSpec((1, lbk, gk), lambda i, h, t: (i, t, h)),
                  pl.BlockSpec((1, lbk, gk), lambda i, h, t: (i, t, DN_QK_W // gk + h)),
                  pl.BlockSpec((1, lbk, gv), lambda i, h, t: (i, t, 2 * DN_QK_W // gv + h)),
                  pl.BlockSpec((1, lbk, gv), lambda i, h, t: (i, t, DN_CONV_DIM // gv + h)),
                  pl.BlockSpec((1, lbk, DN_V_HEADS), lambda i, h, t: (i, t, 0)),
                  pl.BlockSpec((1, lbk, DN_V_HEADS), lambda i, h, t: (i, t, 0)),
                  pl.BlockSpec((1, DN_V_HEADS, lbk), lambda i, h, t: (i, 0, t)),
                  pl.BlockSpec((1, DN_V_DIM), lambda i, h, t: (0, 0))],
        out_specs=[pl.BlockSpec((1, lbk, gv), lambda i, h, t: (i, t, h)),
                   pl.BlockSpec((1, DN_GROUP, DN_K_DIM, DN_V_DIM), lambda i, h, t: (i, h, 0, 0))],
        out_shape=[jax.ShapeDtypeStruct((b, l, DN_V_W), BF16),
                   jax.ShapeDtypeStruct((b, DN_V_HEADS, DN_K_DIM, DN_V_DIM), F32)],
        scratch_shapes=[pltpu.VMEM((DN_GROUP, DN_K_DIM, DN_V_DIM), F32),
                        pltpu.VMEM((DN_GROUP, nc, DN_K_DIM + cz, DN_K_DIM), BF16),
                        pltpu.VMEM((DN_GROUP, nc, DN_K_DIM + cz, DN_V_DIM), F32),
                        pltpu.VMEM((DN_GROUP, lbk, DN_V_DIM), F32),
                        pltpu.VMEM((DN_GROUP, nc, 1, DN_V_DIM), F32)],
        compiler_params=_params("parallel", "parallel", "arbitrary"),
        name="dn_chunked",
    )(qkvc, qkvc, qkvc, proj, gcum, beta, gcum_t, out_norm.reshape(1, DN_V_DIM))


DN_ROWS = DN_CONV_DIM // LANES
DN_QROWS = DN_K_HEADS
DN_VROW0 = 2 * DN_K_HEADS


def _dn_sample_kernel(x_ref, buf_ref, w_ref, z_ref, eg_ref, beta_ref, s_ref, onorm_ref,
                      o_ref, s_out_ref, buf_out_ref):
    b = pl.program_id(0)
    x = x_ref[0]
    buf = buf_ref[0]
    conv = buf[0] * w_ref[0]
    for i in range(1, DN_CONV_K - 1):
        conv = conv + buf[i] * w_ref[i]
    conv = conv + x * w_ref[DN_CONV_K - 1]
    buf_out_ref[0, 0:DN_CONV_K - 2] = buf[1:DN_CONV_K - 1]
    buf_out_ref[0, DN_CONV_K - 2] = x
    c = jax.nn.silu(conv)
    qk = c[0:DN_VROW0]
    qk = qk * lax.rsqrt(jnp.sum(qk * qk, axis=-1, keepdims=True) + EPS)
    q_t = jnp.transpose(qk[0:DN_QROWS] * (DN_K_DIM ** -0.5))
    k_t = jnp.transpose(qk[DN_QROWS:DN_VROW0])
    for h in range(DN_V_HEADS):
        kh = h // (DN_V_HEADS // DN_K_HEADS)
        kcol = k_t[:, kh:kh + 1]
        qcol = q_t[:, kh:kh + 1]
        v = c[DN_VROW0 + h:DN_VROW0 + h + 1]
        s = s_ref[0, h] * eg_ref[b, h]
        delta = (v - jnp.sum(kcol * s, axis=0, keepdims=True)) * beta_ref[b, h]
        s = s + kcol * delta
        s_out_ref[0, h] = s
        o = jnp.sum(qcol * s, axis=0, keepdims=True)
        o_ref[0, h:h + 1, :] = (_rms(o, onorm_ref[...]) * jax.nn.silu(z_ref[0, h:h + 1, :])
                                ).astype(o_ref.dtype)


def dn_sample(qkv_rows, conv_buf, conv_w, z_rows, eg, beta, state, out_norm):
    nb = qkv_rows.shape[0]
    smem = pl.BlockSpec(memory_space=pltpu.SMEM)
    return pl.pallas_call(
        _dn_sample_kernel,
        grid=(nb,),
        in_specs=[pl.BlockSpec((1, DN_ROWS, LANES), lambda i: (i, 0, 0)),
                  pl.BlockSpec((1, DN_CONV_K - 1, DN_ROWS, LANES), lambda i: (i, 0, 0, 0)),
                  pl.BlockSpec((DN_CONV_K, DN_ROWS, LANES), lambda i: (0, 0, 0)),
                  pl.BlockSpec((1, DN_V_HEADS, DN_V_DIM), lambda i: (i, 0, 0)),
                  smem, smem,
                  pl.BlockSpec((1, DN_V_HEADS, DN_K_DIM, DN_V_DIM), lambda i: (i, 0, 0, 0)),
                  pl.BlockSpec((1, DN_V_DIM), lambda i: (0, 0))],
        out_specs=[pl.BlockSpec((1, DN_V_HEADS, DN_V_DIM), lambda i: (i, 0, 0)),
                   pl.BlockSpec((1, DN_V_HEADS, DN_K_DIM, DN_V_DIM), lambda i: (i, 0, 0, 0)),
                   pl.BlockSpec((1, DN_CONV_K - 1, DN_ROWS, LANES), lambda i: (i, 0, 0, 0))],
        out_shape=[jax.ShapeDtypeStruct((nb, DN_V_HEADS, DN_V_DIM), BF16),
                   jax.ShapeDtypeStruct(state.shape, F32),
                   jax.ShapeDtypeStruct(conv_buf.shape, F32)],
        compiler_params=_params("arbitrary"),
        name="dn_sample",
    )(qkv_rows, conv_buf, conv_w, z_rows, eg, beta, state, out_norm.reshape(1, DN_V_DIM))


def _tile(m, cap):
    return min(m, cap)


def _even_layer(x, cache, w, *, batch, seq):
    m = x.shape[0]
    tm = _tile(m, 1024)
    qkv = norm_matmul(x, w['norm_mix'], w['w_in_qkv'], tm=tm, tn=512)
    u = norm_matmul(x, w['norm_mix'], w['w_in_u'], tm=tm, tn=512)
    if cache is None:
        attn, new_k, new_v = attn_prompt(qkv, w['q_norm'], w['k_norm'], w['sinks'], batch=batch, seq=seq)
        zeros = jnp.zeros((batch, S5_LANES), F32)
        ssm, h_re, h_im = s5_mix(u.reshape(batch, seq, S5_WIDTH), zeros, zeros, w['s5'],
                                 rows=batch, seq=seq, steps=64)
        ssm = ssm.reshape(m, S5_WIDTH)
    else:
        k_win, v_win, h0_re, h0_im = cache
        attn, new_k, new_v = attn_sample(qkv, k_win.reshape(batch, WINDOW, KV_WIDTH),
                                         v_win.reshape(batch, WINDOW, KV_WIDTH),
                                         w['q_norm'], w['k_norm'], w['sinks'])
        ssm, h_re, h_im = s5_mix(u, h0_re.reshape(batch, S5_LANES), h0_im.reshape(batch, S5_LANES),
                                 w['s5'], rows=batch, seq=1, steps=1)
    x = matmul_residual(x, [attn, ssm], [w['w_out_a'], w['w_out_b']], tm=tm, tn=512)
    x = ffn(x, w['norm_ffn'], w['ffn_gate'], w['ffn_up'], w['ffn_down'], tm=tm, tf=512)
    shp = (batch, WINDOW, N_KV_HEADS, HEAD_DIM)
    st = (batch, S5_GROUPS, S5_STATE)
    return x, new_k.reshape(shp), new_v.reshape(shp), h_re.reshape(st), h_im.reshape(st)


def _odd_layer(x, cache, w, *, batch, seq):
    m = x.shape[0]
    tm = _tile(m, 1024)
    proj = norm_matmul(x, w['norm_mix'], w['w_in_main'], tm=tm, tn=1024)
    ba = norm_matmul(x, w['norm_mix'], w['w_in_ba'], tm=tm, tn=LANES)
    beta, eg, gcum = dn_gates(ba, w['a_log'], w['dt_bias'], tg=_tile(m, 512))
    if cache is None:
        proj3 = proj.reshape(batch, seq, DN_CONV_DIM + DN_V_W)
        qkvc = dn_conv_prompt(proj3, w['conv_w'], tl=256, tc=1024)
        gcum3 = gcum.reshape(batch, seq, DN_V_HEADS)
        o, s_new = dn_chunked(qkvc, proj3, gcum3, beta.reshape(batch, seq, DN_V_HEADS),
                              jnp.swapaxes(gcum3, 1, 2), w['out_norm'])
        o = o.reshape(m, DN_V_W)
        new_buf = proj3[:, seq - (DN_CONV_K - 1):, :DN_CONV_DIM]
    else:
        s0, conv_buf = cache
        o, s_new, new_buf = dn_sample(
            proj[:, :DN_CONV_DIM].reshape(batch, DN_ROWS, LANES),
            conv_buf.reshape(batch, DN_CONV_K - 1, DN_ROWS, LANES),
            w['conv_w'].reshape(DN_CONV_K, DN_ROWS, LANES),
            proj[:, DN_CONV_DIM:].reshape(batch, DN_V_HEADS, DN_V_DIM),
            eg, beta, s0, w['out_norm'])
        o = o.reshape(m, DN_V_W)
        new_buf = new_buf.reshape(batch, DN_CONV_K - 1, DN_CONV_DIM)
    x = matmul_residual(x, [o], [w['w_out']], tm=tm, tn=512)
    x = moe_sparse(x, w['norm_ffn'], w['router'], w['exp_gate'], w['exp_up'], w['exp_down'])
    return x, s_new, new_buf


def kernel(x_prompt, x_sample, cache_win_k, cache_win_v, state_s5_re, state_s5_im, state_dn, state_dn_conv,
           e_norm_mix, e_w_in, e_q_norm, e_k_norm, e_sinks,
           e_s5_a_re, e_s5_a_im, e_s5_log_dt, e_s5_b_re, e_s5_b_im, e_s5_c_re, e_s5_c_im, e_s5_d, e_s5_w_glu,
           e_w_out, e_norm_ffn, e_ffn_w_gate, e_ffn_w_up, e_ffn_w_down,
           o_norm_mix, o_w_in, o_conv_w, o_a_log, o_dt_bias, o_out_norm, o_w_out, o_norm_ffn,
           o_router, o_exp_w_gate, o_exp_w_up, o_exp_w_down):
    bp, lp, d = x_prompt.shape
    bs, ls, _ = x_sample.shape
    assert ls == 1, "the sample group advances one token per step"
    hp = x_prompt.reshape(bp * lp, d)
    hs = x_sample.reshape(bs * ls, d)
    qkv_w = ATTN_WIDTH + 2 * KV_WIDTH
    main_w = DN_CONV_DIM + DN_V_W

    j = 0
    we = dict(
        norm_mix=e_norm_mix[j], q_norm=e_q_norm[j], k_norm=e_k_norm[j], sinks=e_sinks[j],
        w_in_qkv=e_w_in[j, :, :qkv_w].astype(BF16), w_in_u=e_w_in[j, :, qkv_w:].astype(BF16),
        s5=s5_constants(e_s5_a_re[j], e_s5_a_im[j], e_s5_log_dt[j], e_s5_b_re[j], e_s5_b_im[j],
                        e_s5_c_re[j], e_s5_c_im[j], e_s5_d[j], e_s5_w_glu[j]),
        w_out_a=e_w_out[j, :ATTN_WIDTH].astype(BF16), w_out_b=e_w_out[j, ATTN_WIDTH:].astype(BF16),
        norm_ffn=e_norm_ffn[j], ffn_gate=e_ffn_w_gate[j].astype(BF16),
        ffn_up=e_ffn_w_up[j].astype(BF16), ffn_down=e_ffn_w_down[j].astype(BF16))
    ba_pad = jnp.zeros((d, LANES), F32).at[:, :2 * DN_V_HEADS].set(o_w_in[j, :, main_w:])
    wo = dict(
        norm_mix=o_norm_mix[j], w_in_main=o_w_in[j, :, :main_w].astype(BF16), w_in_ba=ba_pad.astype(BF16),
        conv_w=o_conv_w[j], a_log=o_a_log[j], dt_bias=o_dt_bias[j], out_norm=o_out_norm[j],
        w_out=o_w_out[j].astype(BF16), norm_ffn=o_norm_ffn[j], router=o_router[j],
        exp_gate=o_exp_w_gate[j].astype(BF16), exp_up=o_exp_w_up[j].astype(BF16),
        exp_down=o_exp_w_down[j].astype(BF16))

    hp, kp, vp, rp, ip = _even_layer(hp, None, we, batch=bp, seq=lp)
    hs, ks, vs, rs, is_ = _even_layer(
        hs, (cache_win_k[j], cache_win_v[j], state_s5_re[j], state_s5_im[j]), we, batch=bs, seq=1)
    hp, sp, cp = _odd_layer(hp, None, wo, batch=bp, seq=lp)
    hs, ss, cs = _odd_layer(hs, (state_dn[j], state_dn_conv[j]), wo, batch=bs, seq=1)

    one = lambda t: t[None]
    return (hp.reshape(bp, lp, d), hs.reshape(bs, ls, d),
            one(kp), one(vp), one(rp), one(ip), one(sp), one(cp),
            one(ks), one(vs), one(rs), one(is_), one(ss), one(cs))
```

```python
import functools

import jax
import jax.numpy as jnp
from jax import lax
from jax.experimental import pallas as pl
from jax.experimental.pallas import tpu as pltpu

F32 = jnp.float32
BF16 = jnp.bfloat16

D_MODEL = 2048
N_HEADS = 16
N_KV_HEADS = 4
HEAD_DIM = 64
Q_PER_KV = N_HEADS // N_KV_HEADS
WINDOW = 128
ATTN_WIDTH = N_HEADS * HEAD_DIM
KV_WIDTH = N_KV_HEADS * HEAD_DIM
ATTN_SCALE = HEAD_DIM ** -0.5
S5_WIDTH = D_MODEL // 2
S5_GROUP_CH = 16
S5_GROUPS = S5_WIDTH // S5_GROUP_CH
S5_STATE = 64
S5_LANES = S5_GROUPS * S5_STATE
DN_K_HEADS = 16
DN_V_HEADS = 32
DN_K_DIM = 128
DN_V_DIM = 128
DN_CONV_K = 4
DN_CHUNK = 64
DN_QK_W = DN_K_HEADS * DN_K_DIM
DN_V_W = DN_V_HEADS * DN_V_DIM
DN_CONV_DIM = 2 * DN_QK_W + DN_V_W
D_FF = 5632
N_EXPERTS = 8
EPS = 1e-6
NEG_INF = -1e30

LANES = 128
SUBLANES = 8
VMEM_LIMIT = 56 * 1024 * 1024


def _params(*sem):
    return pltpu.CompilerParams(dimension_semantics=sem, vmem_limit_bytes=VMEM_LIMIT)


def _rms(x, g):
    return x * lax.rsqrt(jnp.mean(x * x, axis=-1, keepdims=True) + EPS) * g


def _dot(a, b):
    return jnp.dot(a, b, preferred_element_type=F32)


def _dot_nt(a, b):
    return lax.dot_general(a, b, (((1,), (1,)), ((), ())), preferred_element_type=F32)


def _split3(x):
    hi = x.astype(BF16)
    r = x - hi.astype(F32)
    mid = r.astype(BF16)
    lo = (r - mid.astype(F32)).astype(BF16)
    return hi, mid, lo


def _norm_matmul_kernel(x_ref, g_ref, w_ref, o_ref, xn_ref):
    @pl.when(pl.program_id(1) == 0)
    def _():
        xn_ref[...] = _rms(x_ref[...], g_ref[...]).astype(BF16)

    o_ref[...] = _dot(xn_ref[...], w_ref[...]).astype(o_ref.dtype)


def norm_matmul(x, g, w, *, tm, tn, out_dtype=F32):
    m, d = x.shape
    n = w.shape[1]
    return pl.pallas_call(
        _norm_matmul_kernel,
        grid=(m // tm, n // tn),
        in_specs=[pl.BlockSpec((tm, d), lambda i, j: (i, 0)),
                  pl.BlockSpec((1, d), lambda i, j: (0, 0)),
                  pl.BlockSpec((d, tn), lambda i, j: (0, j))],
        out_specs=pl.BlockSpec((tm, tn), lambda i, j: (i, j)),
        out_shape=jax.ShapeDtypeStruct((m, n), out_dtype),
        scratch_shapes=[pltpu.VMEM((tm, d), BF16)],
        compiler_params=_params("parallel", "arbitrary"),
        name="norm_matmul",
    )(x, g.reshape(1, d), w)


def _matmul_residual_kernel(*refs, n_pairs):
    x_ref = refs[0]
    o_ref = refs[-1]
    acc = x_ref[...]
    for a_ref, w_ref in zip(refs[1:1 + n_pairs], refs[1 + n_pairs:1 + 2 * n_pairs]):
        acc = acc + _dot(a_ref[...], w_ref[...])
    o_ref[...] = acc


def matmul_residual(x, a_list, w_list, *, tm, tn):
    m, n = x.shape
    in_specs = [pl.BlockSpec((tm, tn), lambda i, j: (i, j))]
    in_specs += [pl.BlockSpec((tm, a.shape[1]), lambda i, j: (i, 0)) for a in a_list]
    in_specs += [pl.BlockSpec((w.shape[0], tn), lambda i, j: (0, j)) for w in w_list]
    return pl.pallas_call(
        functools.partial(_matmul_residual_kernel, n_pairs=len(a_list)),
        grid=(m // tm, n // tn),
        in_specs=in_specs,
        out_specs=pl.BlockSpec((tm, tn), lambda i, j: (i, j)),
        out_shape=jax.ShapeDtypeStruct((m, n), F32),
        compiler_params=_params("parallel", "arbitrary"),
        name="matmul_residual",
    )(x, *a_list, *w_list)


HEADS_PER_TILE = LANES // HEAD_DIM


def _head_rms(x, seg, g):
    hi, mid, lo = _split3(x * x)
    parts = []
    for a in range(x.shape[1] // LANES):
        sl = slice(a * LANES, (a + 1) * LANES)
        parts.append(_dot(hi[:, sl], seg) + (_dot(mid[:, sl], seg) + _dot(lo[:, sl], seg)))
    ms = jnp.concatenate(parts, axis=1) * (1.0 / HEAD_DIM)
    return x * lax.rsqrt(ms + EPS) * g


def _attn_prompt_kernel(q_ref, kc_ref, kp_ref, vc_ref, vp_ref, qn_ref, kn_ref, sink_ref,
                        o_ref, wk_ref, wv_ref, s_ref, p_ref):
    has_prev = pl.program_id(1) > 0
    r = lax.broadcasted_iota(jnp.int32, (LANES, LANES), 0)
    c = lax.broadcasted_iota(jnp.int32, (LANES, LANES), 1)
    seg = jnp.where(r // HEAD_DIM == c // HEAD_DIM, 1.0, 0.0).astype(BF16)
    low = lax.broadcasted_iota(jnp.int32, (1, LANES), 1) < HEAD_DIM

    qn = _head_rms(q_ref[...], seg, qn_ref[...]) * ATTN_SCALE
    kc = _head_rms(kc_ref[...], seg, kn_ref[...])
    kp = _head_rms(kp_ref[...], seg, kn_ref[...])
    vc = vc_ref[...]
    wk_ref[0] = kc
    wv_ref[0] = vc
    kband = jnp.concatenate([kp, kc], axis=0)
    vband = jnp.concatenate([vp_ref[...], vc], axis=0)

    v_halves = []
    for kvh in range(N_KV_HEADS):
        tl = slice(kvh // HEADS_PER_TILE * LANES, (kvh // HEADS_PER_TILE + 1) * LANES)

        def both_halves(t):
            rolled = pltpu.roll(t, HEAD_DIM, axis=1)
            return jnp.where(low, t, rolled) if kvh % HEADS_PER_TILE == 0 else jnp.where(low, rolled, t)

        kdup = both_halves(kband[:, tl]).astype(BF16)
        vdup = both_halves(vband[:, tl])
        v_halves.append((jnp.where(low, vdup, 0.0).astype(BF16), jnp.where(low, 0.0, vdup).astype(BF16)))
        for pr in range(Q_PER_KV // HEADS_PER_TILE):
            a = kvh * Q_PER_KV // HEADS_PER_TILE + pr
            qt = qn[:, a * LANES:(a + 1) * LANES]
            s_ref[HEADS_PER_TILE * a] = _dot_nt(jnp.where(low, qt, 0.0).astype(BF16), kdup)
            s_ref[HEADS_PER_TILE * a + 1] = _dot_nt(jnp.where(low, 0.0, qt).astype(BF16), kdup)

    row = lax.broadcasted_iota(jnp.int32, (WINDOW, 2 * WINDOW), 0)
    col = lax.broadcasted_iota(jnp.int32, (WINDOW, 2 * WINDOW), 1)
    rel = (WINDOW + row) - col
    mask = (rel >= 0) & (rel <= WINDOW) & ((col >= WINDOW) | has_prev)
    for h in range(N_HEADS):
        s = jnp.where(mask, s_ref[h], NEG_INF)
        sk = sink_ref[h:h + 1, :]
        m = jnp.maximum(jnp.max(s, axis=-1, keepdims=True), sk)
        p = jnp.exp(s - m)
        p = p / (jnp.sum(p, axis=-1, keepdims=True) + jnp.exp(sk - m))
        p_ref[h] = p.astype(BF16)

    for a in range(N_HEADS // HEADS_PER_TILE):
        v_lo, v_hi = v_halves[a * HEADS_PER_TILE // Q_PER_KV]
        o = _dot(p_ref[HEADS_PER_TILE * a], v_lo) + _dot(p_ref[HEADS_PER_TILE * a + 1], v_hi)
        o_ref[:, a * LANES:(a + 1) * LANES] = o.astype(o_ref.dtype)


def attn_prompt(qkv, q_norm, k_norm, sinks, *, batch, seq):
    nb = seq // WINDOW
    kcol = ATTN_WIDTH // KV_WIDTH
    cur = lambda c: (lambda b, n: (b * nb + n, c))
    prev = lambda c: (lambda b, n: (jnp.maximum(b * nb + n - 1, 0), c))
    return pl.pallas_call(
        _attn_prompt_kernel,
        grid=(batch, nb),
        in_specs=[pl.BlockSpec((WINDOW, ATTN_WIDTH), cur(0)),
                  pl.BlockSpec((WINDOW, KV_WIDTH), cur(kcol)),
                  pl.BlockSpec((WINDOW, KV_WIDTH), prev(kcol)),
                  pl.BlockSpec((WINDOW, KV_WIDTH), cur(kcol + 1)),
                  pl.BlockSpec((WINDOW, KV_WIDTH), prev(kcol + 1)),
                  pl.BlockSpec((1, ATTN_WIDTH), lambda b, n: (0, 0)),
                  pl.BlockSpec((1, KV_WIDTH), lambda b, n: (0, 0)),
                  pl.BlockSpec((N_HEADS, 1), lambda b, n: (0, 0))],
        out_specs=[pl.BlockSpec((WINDOW, ATTN_WIDTH), lambda b, n: (b * nb + n, 0)),
                   pl.BlockSpec((1, WINDOW, KV_WIDTH), lambda b, n: (b, 0, 0)),
                   pl.BlockSpec((1, WINDOW, KV_WIDTH), lambda b, n: (b, 0, 0))],
        out_shape=[jax.ShapeDtypeStruct((batch * seq, ATTN_WIDTH), BF16),
                   jax.ShapeDtypeStruct((batch, WINDOW, KV_WIDTH), F32),
                   jax.ShapeDtypeStruct((batch, WINDOW, KV_WIDTH), F32)],
        scratch_shapes=[pltpu.VMEM((N_HEADS, WINDOW, 2 * WINDOW), F32),
                        pltpu.VMEM((N_HEADS, WINDOW, 2 * WINDOW), BF16)],
        compiler_params=_params("parallel", "arbitrary"),
        name="attn_prompt",
    )(qkv, qkv, qkv, qkv, qkv, jnp.tile(q_norm, N_HEADS).reshape(1, ATTN_WIDTH),
      jnp.tile(k_norm, N_KV_HEADS).reshape(1, KV_WIDTH), sinks.reshape(N_HEADS, 1))


def _attn_sample_kernel(qkv_ref, ck_ref, cv_ref, qn_ref, kn_ref, sink_ref,
                        o_ref, wk_ref, wv_ref, *, bt):
    qg = qn_ref[...]
    kg = kn_ref[...]
    kv0 = ATTN_WIDTH
    v0 = ATTN_WIDTH + KV_WIDTH
    wk_ref[:, 0:WINDOW - 1, :] = ck_ref[:, 1:WINDOW, :]
    wv_ref[:, 0:WINDOW - 1, :] = cv_ref[:, 1:WINDOW, :]
    wv_ref[:, WINDOW - 1:WINDOW, :] = qkv_ref[:, v0:v0 + KV_WIDTH][:, None, :]
    for kvh in range(N_KV_HEADS):
        sl = slice(kvh * HEAD_DIM, (kvh + 1) * HEAD_DIM)
        kn = _rms(qkv_ref[:, kv0 + kvh * HEAD_DIM:kv0 + (kvh + 1) * HEAD_DIM], kg)[:, None, :]
        vn = qkv_ref[:, v0 + kvh * HEAD_DIM:v0 + (kvh + 1) * HEAD_DIM][:, None, :]
        wk_ref[:, WINDOW - 1:WINDOW, sl] = kn
        q4 = jnp.concatenate(
            [_rms(qkv_ref[:, (kvh * Q_PER_KV + g) * HEAD_DIM:(kvh * Q_PER_KV + g + 1) * HEAD_DIM], qg)[:, None, :]
             for g in range(Q_PER_KV)], axis=1)
        kwin = ck_ref[:, :, sl].astype(BF16)
        vwin = cv_ref[:, :, sl].astype(BF16)
        s_c = jnp.einsum('bqd,bkd->bqk', q4.astype(BF16), kwin, preferred_element_type=F32) * ATTN_SCALE
        s_n = jnp.sum(q4 * kn, axis=-1, keepdims=True) * ATTN_SCALE
        sk = sink_ref[kvh * Q_PER_KV:(kvh + 1) * Q_PER_KV, :][None]
        m = jnp.maximum(jnp.maximum(jnp.max(s_c, axis=-1, keepdims=True), s_n), sk)
        p_c = jnp.exp(s_c - m)
        p_n = jnp.exp(s_n - m)
        den = jnp.sum(p_c, axis=-1, keepdims=True) + p_n + jnp.exp(sk - m)
        o4 = jnp.einsum('bqk,bkd->bqd', (p_c / den).astype(BF16), vwin,
                        preferred_element_type=F32) + (p_n / den) * vn
        for g in range(Q_PER_KV):
            h = kvh * Q_PER_KV + g
            o_ref[:, h * HEAD_DIM:(h + 1) * HEAD_DIM] = o4[:, g, :].astype(o_ref.dtype)


def attn_sample(qkv, cache_k, cache_v, q_norm, k_norm, sinks, *, bt=8):
    nb = qkv.shape[0]
    width = qkv.shape[1]
    win = pl.BlockSpec((bt, WINDOW, KV_WIDTH), lambda i: (i, 0, 0))
    return pl.pallas_call(
        functools.partial(_attn_sample_kernel, bt=bt),
        grid=(nb // bt,),
        in_specs=[pl.BlockSpec((bt, width), lambda i: (i, 0)), win, win,
                  pl.BlockSpec((1, HEAD_DIM), lambda i: (0, 0)),
                  pl.BlockSpec((1, HEAD_DIM), lambda i: (0, 0)),
                  pl.BlockSpec((N_HEADS, 1), lambda i: (0, 0))],
        out_specs=[pl.BlockSpec((bt, ATTN_WIDTH), lambda i: (i, 0)), win, win],
        out_shape=[jax.ShapeDtypeStruct((nb, ATTN_WIDTH), BF16),
                   jax.ShapeDtypeStruct((nb, WINDOW, KV_WIDTH), F32),
                   jax.ShapeDtypeStruct((nb, WINDOW, KV_WIDTH), F32)],
        compiler_params=_params("parallel"),
        name="attn_sample",
    )(qkv, cache_k, cache_v, q_norm.reshape(1, HEAD_DIM), k_norm.reshape(1, HEAD_DIM),
      sinks.reshape(N_HEADS, 1))


S5_KCH = S5_WIDTH // LANES
S5_CHUNK_STATES = S5_LANES // S5_KCH
S5_SCAN_TILES = 4


def _s5_kernel(u_ref, h0re_ref, h0im_ref, are_ref, aim_ref, wbu_ref, wc_ref, d_ref, wglu_ref,
               y_ref, hre_ref, him_ref, hs_ref, st_ref, *, rows, steps):
    rt = rows * steps
    u = u_ref[...].reshape(rt, S5_WIDTH)

    @pl.when(pl.program_id(0) == 0)
    def _():
        st_ref[:, :S5_LANES] = h0re_ref[...]
        st_ref[:, S5_LANES:] = h0im_ref[...]

    if steps > 1:
        assert rows == SUBLANES
        r = lax.broadcasted_iota(jnp.int32, (rt, rt), 0)
        c = lax.broadcasted_iota(jnp.int32, (rt, rt), 1)
        to_time_major = jnp.where(c == (r % rows) * steps + r // rows, 1.0, 0.0).astype(BF16)
        to_seq_major = jnp.where(c == (r % steps) * rows + r // steps, 1.0, 0.0).astype(BF16)
        hi, mid, lo = _split3(u)
        u_hi = _dot(to_time_major, hi)
        u = u_hi + (_dot(to_time_major, mid) + _dot(to_time_major, lo))
        ub = u_hi.astype(BF16)
    else:
        ub = u.astype(BF16)
    nre = S5_LANES // LANES
    tpk = S5_CHUNK_STATES // LANES
    for k in range(S5_KCH):
        r = _dot(ub[:, k * LANES:(k + 1) * LANES], wbu_ref[k])
        for a in range(tpk):
            hs_ref[k * tpk + a] = r[:, a * LANES:(a + 1) * LANES]
            hs_ref[nre + k * tpk + a] = r[:, (tpk + a) * LANES:(tpk + a + 1) * LANES]

    nt = S5_SCAN_TILES
    for j in range(nre // nt):
        tiles = range(j * nt, (j + 1) * nt)
        ar = [jnp.broadcast_to(are_ref[:, a * LANES:(a + 1) * LANES], (SUBLANES, LANES)) for a in tiles]
        ai = [jnp.broadcast_to(aim_ref[:, a * LANES:(a + 1) * LANES], (SUBLANES, LANES)) for a in tiles]

        def advance(r8, hr, hi):
            nr, ni = [], []
            for n, a in enumerate(tiles):
                nr.append(ar[n] * hr[n] - ai[n] * hi[n] + hs_ref[a, r8, :])
                ni.append(ar[n] * hi[n] + ai[n] * hr[n] + hs_ref[nre + a, r8, :])
                hs_ref[a, r8, :] = nr[n]
                hs_ref[nre + a, r8, :] = ni[n]
            return nr, ni

        if steps == 1:
            def group(rg, carry):
                r8 = pl.ds(pl.multiple_of(rg * SUBLANES, SUBLANES), SUBLANES)
                hr = [st_ref[r8, a * LANES:(a + 1) * LANES] for a in tiles]
                hi = [st_ref[r8, S5_LANES + a * LANES:S5_LANES + (a + 1) * LANES] for a in tiles]
                nr, ni = advance(r8, hr, hi)
                for n, a in enumerate(tiles):
                    st_ref[r8, a * LANES:(a + 1) * LANES] = nr[n]
                    st_ref[r8, S5_LANES + a * LANES:S5_LANES + (a + 1) * LANES] = ni[n]
                return carry
            lax.fori_loop(0, rows // SUBLANES, group, 0)
        else:
            def step(t, carry):
                nr, ni = advance(pl.ds(pl.multiple_of(t * SUBLANES, SUBLANES), SUBLANES), *carry)
                return tuple(nr), tuple(ni)
            hr0 = tuple(st_ref[:, a * LANES:(a + 1) * LANES] for a in tiles)
            hi0 = tuple(st_ref[:, S5_LANES + a * LANES:S5_LANES + (a + 1) * LANES] for a in tiles)
            hr, hi = lax.fori_loop(0, steps, step, (hr0, hi0))
            for n, a in enumerate(tiles):
                st_ref[:, a * LANES:(a + 1) * LANES] = hr[n]
                st_ref[:, S5_LANES + a * LANES:S5_LANES + (a + 1) * LANES] = hi[n]

    ys = []
    for k in range(S5_KCH):
        hre = jnp.concatenate([hs_ref[k * tpk + a] for a in range(tpk)], axis=1).astype(BF16)
        him = jnp.concatenate([hs_ref[nre + k * tpk + a] for a in range(tpk)], axis=1).astype(BF16)
        yk = _dot(hre, wc_ref[0, k]) + _dot(him, wc_ref[1, k])
        ys.append(yk + d_ref[:, k * LANES:(k + 1) * LANES] * u[:, k * LANES:(k + 1) * LANES])
    y = jax.nn.gelu(jnp.concatenate(ys, axis=1))
    out = (y * jax.nn.sigmoid(_dot(y.astype(BF16), wglu_ref[...]))).astype(y_ref.dtype)
    if steps > 1:
        out = _dot(to_seq_major, out).astype(y_ref.dtype)
    y_ref[...] = out.reshape(y_ref.shape)

    @pl.when(pl.program_id(0) == pl.num_programs(0) - 1)
    def _():
        hre_ref[...] = st_ref[:, :S5_LANES]
        him_ref[...] = st_ref[:, S5_LANES:]


def s5_mix(u, h0_re, h0_im, consts, *, rows, seq, steps):
    a_re, a_im, w_bu, w_c, d_skip, w_glu = consts
    if seq > 1:
        u_spec = pl.BlockSpec((rows, steps, S5_WIDTH), lambda c: (0, c, 0))
    else:
        u_spec = pl.BlockSpec((rows, S5_WIDTH), lambda c: (0, 0))
    full = lambda shape: pl.BlockSpec(shape, lambda c: (0,) * len(shape))
    return pl.pallas_call(
        functools.partial(_s5_kernel, rows=rows, steps=steps),
        grid=(seq // steps,),
        in_specs=[u_spec, full((rows, S5_LANES)), full((rows, S5_LANES)),
                  full((1, S5_LANES)), full((1, S5_LANES)),
                  full(w_bu.shape), full(w_c.shape), full((1, S5_WIDTH)), full(w_glu.shape)],
        out_specs=[u_spec, full((rows, S5_LANES)), full((rows, S5_LANES))],
        out_shape=[jax.ShapeDtypeStruct(u.shape, BF16),
                   jax.ShapeDtypeStruct((rows, S5_LANES), F32),
                   jax.ShapeDtypeStruct((rows, S5_LANES), F32)],
        scratch_shapes=[pltpu.VMEM((2 * S5_LANES // LANES, rows * steps, LANES), F32),
                        pltpu.VMEM((rows, 2 * S5_LANES), F32)],
        compiler_params=_params("arbitrary"),
        name="s5_mix",
    )(u, h0_re, h0_im, a_re, a_im, w_bu, w_c, d_skip, w_glu)


def s5_constants(a_re, a_im, log_dt, b_re, b_im, c_re, c_im, d_skip, w_glu):
    lr, li = a_re, a_im
    dt = jnp.exp(log_dt)[:, None]
    mag = jnp.exp(lr * dt)
    ab_re, ab_im = mag * jnp.cos(li * dt), mag * jnp.sin(li * dt)
    den = lr * lr + li * li
    f_re = ((ab_re - 1.0) * lr + ab_im * li) / den
    f_im = (ab_im * lr - (ab_re - 1.0) * li) / den
    bb_re = f_re[..., None] * b_re - f_im[..., None] * b_im
    bb_im = f_re[..., None] * b_im + f_im[..., None] * b_re
    gpc = LANES // S5_GROUP_CH
    eye = jnp.eye(gpc, dtype=F32)

    def bu_blocks(bb):
        t = bb.reshape(S5_KCH, gpc, S5_STATE, S5_GROUP_CH)
        return jnp.einsum('kgpc,gh->kgchp', t, eye).reshape(S5_KCH, LANES, gpc * S5_STATE)

    def c_blocks(c):
        t = c.reshape(S5_KCH, gpc, S5_GROUP_CH, S5_STATE)
        return jnp.einsum('kgcp,gh->kgphc', t, eye).reshape(S5_KCH, gpc * S5_STATE, LANES)

    w_bu = jnp.concatenate([bu_blocks(bb_re), bu_blocks(bb_im)], axis=-1).astype(BF16)
    w_c = jnp.stack([c_blocks(c_re), -c_blocks(c_im)]).astype(BF16)
    return (ab_re.reshape(1, S5_LANES), ab_im.reshape(1, S5_LANES), w_bu, w_c,
            d_skip.reshape(1, S5_WIDTH), w_glu.astype(BF16))


def _ffn_kernel(x_ref, g_ref, wg_ref, wu_ref, wd_ref, o_ref, xn_ref):
    @pl.when(pl.program_id(1) == 0)
    def _():
        x = x_ref[...]
        xn_ref[...] = _rms(x, g_ref[...]).astype(BF16)
        o_ref[...] = x

    xn = xn_ref[...]
    h = jax.nn.silu(_dot(xn, wg_ref[...])) * _dot(xn, wu_ref[...])
    o_ref[...] += _dot(h.astype(BF16), wd_ref[...])


def ffn(x, g, w_gate, w_up, w_down, *, tm, tf):
    m, d = x.shape
    f = w_gate.shape[1]
    return pl.pallas_call(
        _ffn_kernel,
        grid=(m // tm, f // tf),
        in_specs=[pl.BlockSpec((tm, d), lambda i, j: (i, 0), pipeline_mode=pl.Buffered(1)),
                  pl.BlockSpec((1, d), lambda i, j: (0, 0)),
                  pl.BlockSpec((d, tf), lambda i, j: (0, j)),
                  pl.BlockSpec((d, tf), lambda i, j: (0, j)),
                  pl.BlockSpec((tf, d), lambda i, j: (j, 0))],
        out_specs=pl.BlockSpec((tm, d), lambda i, j: (i, 0)),
        out_shape=jax.ShapeDtypeStruct((m, d), F32),
        scratch_shapes=[pltpu.VMEM((tm, d), BF16)],
        compiler_params=_params("parallel", "arbitrary"),
        name="ffn",
    )(x, g.reshape(1, d), w_gate, w_up, w_down)


MOE_TILE = 512
META_I1, META_I2, META_W1, META_W2, META_R1, META_R2 = range(6)


def _router_kernel(x_ref, g_ref, r_ref, xn_ref, meta_ref, cnt_ref, carry_ref):
    @pl.when(pl.program_id(0) == 0)
    def _():
        carry_ref[...] = jnp.zeros_like(carry_ref)

    xn = _rms(x_ref[...], g_ref[...])
    xn_ref[...] = xn
    x_hi, x_mid, _ = _split3(xn)
    r_hi, r_mid, _ = _split3(r_ref[...])
    logits = _dot(x_hi, r_hi) + (_dot(x_mid, r_hi) + _dot(x_hi, r_mid))
    lane = lax.broadcasted_iota(jnp.int32, logits.shape, 1)
    neg = -jnp.inf
    l1 = jnp.where(lane < N_EXPERTS, logits, neg)
    m1 = jnp.max(l1, axis=-1, keepdims=True)
    i1 = jnp.min(jnp.where(l1 == m1, lane, LANES), axis=-1, keepdims=True)
    l2 = jnp.where(lane == i1, neg, l1)
    m2 = jnp.max(l2, axis=-1, keepdims=True)
    i2 = jnp.min(jnp.where(l2 == m2, lane, LANES), axis=-1, keepdims=True)
    e = jnp.exp(m2 - m1)
    den = 1.0 + e
    sel = jnp.where(lane == i1, 1.0, jnp.where(lane == i2, 1.0, 0.0))
    tm = sel.shape[0]
    row = lax.broadcasted_iota(jnp.int32, (tm, tm), 0)
    col = lax.broadcasted_iota(jnp.int32, (tm, tm), 1)
    incl = _dot(jnp.where(col <= row, 1.0, 0.0).astype(BF16), sel.astype(BF16))
    excl = incl - sel + carry_ref[0:1, :]
    rank1 = jnp.sum(jnp.where(lane == i1, excl, 0.0), axis=-1, keepdims=True)
    rank2 = jnp.sum(jnp.where(lane == i2, excl, 0.0), axis=-1, keepdims=True)
    carry_ref[...] = carry_ref[...] + incl[tm - 1:tm, :]
    cnt_ref[...] = carry_ref[...]
    fields = {META_I1: i1.astype(F32), META_I2: i2.astype(F32), META_W1: 1.0 / den, META_W2: e / den,
              META_R1: rank1, META_R2: rank2}
    meta = jnp.zeros(logits.shape, F32)
    for k, val in fields.items():
        meta = jnp.where(lane == k, val, meta)
    meta_ref[...] = meta


def router(x, g, r, *, tm):
    m, d = x.shape
    r_pad = jnp.zeros((d, LANES), F32).at[:, :N_EXPERTS].set(r)
    return pl.pallas_call(
        _router_kernel,
        grid=(m // tm,),
        in_specs=[pl.BlockSpec((tm, d), lambda i: (i, 0)),
                  pl.BlockSpec((1, d), lambda i: (0, 0)),
                  pl.BlockSpec((d, LANES), lambda i: (0, 0))],
        out_specs=[pl.BlockSpec((tm, d), lambda i: (i, 0)),
                   pl.BlockSpec((tm, LANES), lambda i: (i, 0)),
                   pl.BlockSpec((SUBLANES, LANES), lambda i: (0, 0))],
        out_shape=[jax.ShapeDtypeStruct((m, d), F32),
                   jax.ShapeDtypeStruct((m, LANES), F32),
                   jax.ShapeDtypeStruct((SUBLANES, LANES), F32)],
        scratch_shapes=[pltpu.VMEM((SUBLANES, LANES), F32)],
        compiler_params=_params("arbitrary"),
        name="router",
    )(x, g.reshape(1, d), r_pad)


def _dispatch_kernel(p1_ref, p2_ref, xn_ref, zeros_ref, xs_ref, sem, *, td):
    del zeros_ref
    base = pl.program_id(0) * td

    def copies(r):
        src = xn_ref.at[pl.ds(r, 1), :]
        return [pltpu.make_async_copy(src, xs_ref.at[pl.ds(p_ref[base + r], 1), :], sem)
                for p_ref in (p1_ref, p2_ref)]

    def issue(r, carry):
        for c in copies(r):
            c.start()
        return carry

    def drain(r, carry):
        for c in copies(r):
            c.wait()
        return carry

    lax.fori_loop(0, td, issue, 0)
    lax.fori_loop(0, td, drain, 0)


def moe_dispatch(xn, pos1, pos2, *, rows, td):
    m, d = xn.shape
    return pl.pallas_call(
        functools.partial(_dispatch_kernel, td=td),
        grid_spec=pltpu.PrefetchScalarGridSpec(
            num_scalar_prefetch=2,
            grid=(m // td,),
            in_specs=[pl.BlockSpec((td, d), lambda i, p1, p2: (i, 0)),
                      pl.BlockSpec(memory_space=pl.ANY)],
            out_specs=pl.BlockSpec(memory_space=pl.ANY),
            scratch_shapes=[pltpu.SemaphoreType.DMA(())]),
        out_shape=jax.ShapeDtypeStruct((rows, d), F32),
        input_output_aliases={3: 0},
        compiler_params=_params("arbitrary"),
        name="moe_dispatch",
    )(pos1, pos2, xn, jnp.zeros((rows, d), F32))


def _expert_kernel(te_ref, tv_ref, xs_ref, wg_ref, wu_ref, wd_ref, ys_ref, xb_ref):
    del te_ref
    f = pl.program_id(1)
    valid = tv_ref[pl.program_id(0)]

    @pl.when(valid > 0)
    def _():
        @pl.when(f == 0)
        def _():
            xb_ref[...] = xs_ref[...].astype(BF16)

        xb = xb_ref[...]
        h = jax.nn.silu(_dot(xb, wg_ref[0])) * _dot(xb, wu_ref[0])
        y = _dot(h.astype(BF16), wd_ref[0])

        @pl.when(f == 0)
        def _():
            ys_ref[...] = y

        @pl.when(f > 0)
        def _():
            ys_ref[...] += y

    @pl.when((valid == 0) & (f == 0))
    def _():
        ys_ref[...] = jnp.zeros_like(ys_ref)


def moe_experts(xs, tile_expert, tile_valid, w_gate, w_up, w_down, *, tile, tf):
    rows, d = xs.shape
    _, _, f = w_gate.shape
    nf = f // tf
    fidx = lambda t, j, tv: jnp.where(tv[t] > 0, j, nf - 1)
    return pl.pallas_call(
        _expert_kernel,
        grid_spec=pltpu.PrefetchScalarGridSpec(
            num_scalar_prefetch=2,
            grid=(rows // tile, nf),
            in_specs=[pl.BlockSpec((tile, d), lambda t, j, te, tv: (t, 0)),
                      pl.BlockSpec((1, d, tf), lambda t, j, te, tv: (te[t], 0, fidx(t, j, tv))),
                      pl.BlockSpec((1, d, tf), lambda t, j, te, tv: (te[t], 0, fidx(t, j, tv))),
                      pl.BlockSpec((1, tf, d), lambda t, j, te, tv: (te[t], fidx(t, j, tv), 0))],
            out_specs=pl.BlockSpec((tile, d), lambda t, j, te, tv: (t, 0)),
            scratch_shapes=[pltpu.VMEM((tile, d), BF16)]),
        out_shape=jax.ShapeDtypeStruct((rows, d), F32),
        compiler_params=_params("arbitrary", "arbitrary"),
        name="moe_experts",
    )(tile_expert, tile_valid, xs, w_gate, w_up, w_down)


def _combine_kernel(p1_ref, p2_ref, x_ref, meta_ref, ys_ref, o_ref, buf_ref, sem, *, tc):
    i = pl.program_id(0)
    slot = i % 2

    def copies(step, sl, r):
        tok = step * tc + r
        return [pltpu.make_async_copy(ys_ref.at[pl.ds(p_ref[tok], 1), :],
                                      buf_ref.at[sl, k, pl.ds(r, 1), :], sem.at[sl])
                for k, p_ref in enumerate((p1_ref, p2_ref))]

    def issue(step, sl):
        def body(r, carry):
            for c in copies(step, sl, r):
                c.start()
            return carry
        lax.fori_loop(0, tc, body, 0)

    @pl.when(i == 0)
    def _():
        issue(0, 0)

    @pl.when(i + 1 < pl.num_programs(0))
    def _():
        issue(i + 1, 1 - slot)

    def drain(r, carry):
        for c in copies(i, slot, r):
            c.wait()
        return carry
    lax.fori_loop(0, tc, drain, 0)

    meta = meta_ref[...]
    w1 = meta[:, META_W1:META_W1 + 1]
    w2 = meta[:, META_W2:META_W2 + 1]
    o_ref[...] = x_ref[...] + w1 * buf_ref[slot, 0] + w2 * buf_ref[slot, 1]


def moe_combine(x, meta, ys, pos1, pos2, *, tc):
    m, d = x.shape
    return pl.pallas_call(
        functools.partial(_combine_kernel, tc=tc),
        grid_spec=pltpu.PrefetchScalarGridSpec(
            num_scalar_prefetch=2,
            grid=(m // tc,),
            in_specs=[pl.BlockSpec((tc, d), lambda i, p1, p2: (i, 0)),
                      pl.BlockSpec((tc, LANES), lambda i, p1, p2: (i, 0)),
                      pl.BlockSpec(memory_space=pl.ANY)],
            out_specs=pl.BlockSpec((tc, d), lambda i, p1, p2: (i, 0)),
            scratch_shapes=[pltpu.VMEM((2, 2, tc, d), F32), pltpu.SemaphoreType.DMA((2,))]),
        out_shape=jax.ShapeDtypeStruct((m, d), F32),
        compiler_params=_params("arbitrary"),
        name="moe_combine",
    )(pos1, pos2, x, meta, ys)


def moe_sparse(x, g, r, w_gate, w_up, w_down):
    m, d = x.shape
    tm = min(m, 512)
    tt = min(m, 256)
    tile = min(m, MOE_TILE)
    xn, meta, counts = router(x, g, r, tm=tm)
    cnt = counts[0, :N_EXPERTS].astype(jnp.int32)
    tiles_e = (cnt + tile - 1) // tile
    tile_end = jnp.cumsum(tiles_e)
    tile_start = tile_end - tiles_e
    n_tiles = (2 * m) // tile + N_EXPERTS
    t_all = jnp.arange(n_tiles, dtype=jnp.int32)
    t = jnp.minimum(t_all, tile_end[-1] - 1)
    tile_expert = jnp.sum(t[:, None] >= tile_end[None, :], axis=1).astype(jnp.int32)
    tile_valid = jnp.clip(cnt[tile_expert] - (t - tile_start[tile_expert]) * tile, 0, tile)
    tile_valid = jnp.where(t_all < tile_end[-1], tile_valid, 0).astype(jnp.int32)
    off = tile_start * tile
    pos1 = off[meta[:, META_I1].astype(jnp.int32)] + meta[:, META_R1].astype(jnp.int32)
    pos2 = off[meta[:, META_I2].astype(jnp.int32)] + meta[:, META_R2].astype(jnp.int32)
    xs = moe_dispatch(xn, pos1, pos2, rows=n_tiles * tile, td=tt)
    ys = moe_experts(xs, tile_expert, tile_valid, w_gate, w_up, w_down, tile=tile, tf=512)
    return moe_combine(x, meta, ys, pos1, pos2, tc=tt)


def _gates_kernel(ba_ref, alog_ref, dtb_ref, beta_ref, eg_ref, gcum_ref, *, tg):
    ba = ba_ref[...]
    b = ba[:, :DN_V_HEADS]
    a = ba[:, DN_V_HEADS:2 * DN_V_HEADS]
    beta_ref[...] = jax.nn.sigmoid(b)
    z = a + dtb_ref[...]
    softplus = jnp.maximum(z, 0.0) + jnp.log1p(jnp.exp(-jnp.abs(z)))
    g = -jnp.exp(alog_ref[...]) * softplus
    eg_ref[...] = jnp.exp(g)
    r = lax.broadcasted_iota(jnp.int32, (tg, tg), 0)
    c = lax.broadcasted_iota(jnp.int32, (tg, tg), 1)
    tri = jnp.where((c <= r) & (r // DN_CHUNK == c // DN_CHUNK), 1.0, 0.0).astype(BF16)
    hi, mid, lo = _split3(g)
    gcum_ref[...] = _dot(tri, hi) + (_dot(tri, mid) + _dot(tri, lo))


def dn_gates(ba, a_log, dt_bias, *, tg):
    m = ba.shape[0]
    out = jax.ShapeDtypeStruct((m, DN_V_HEADS), F32)
    spec = pl.BlockSpec((tg, DN_V_HEADS), lambda i: (i, 0))
    return pl.pallas_call(
        functools.partial(_gates_kernel, tg=tg),
        grid=(m // tg,),
        in_specs=[pl.BlockSpec((tg, LANES), lambda i: (i, 0)),
                  pl.BlockSpec((1, DN_V_HEADS), lambda i: (0, 0)),
                  pl.BlockSpec((1, DN_V_HEADS), lambda i: (0, 0))],
        out_specs=[spec, spec, spec],
        out_shape=[out, out, out],
        compiler_params=_params("parallel"),
        name="dn_gates",
    )(ba, a_log.reshape(1, DN_V_HEADS), dt_bias.reshape(1, DN_V_HEADS))


def _l2norm_heads(c, scale):
    parts = []
    for h in range(c.shape[1] // DN_K_DIM):
        t = c[:, h * DN_K_DIM:(h + 1) * DN_K_DIM]
        t = t * lax.rsqrt(jnp.sum(t * t, axis=-1, keepdims=True) + EPS)
        parts.append(t * scale if scale != 1.0 else t)
    return jnp.concatenate(parts, axis=1)


DN_GROUP = 4
DN_LBLOCK = 512
DN_BATCH = 8
DN_HBATCH = 4
DN_CONV_ROWS = 128


def _dn_chunk_kernel(q_ref, k_ref, v_ref, qh_ref, kh_ref, vh_ref, wq_ref, wk_ref, wv_ref,
                     z_ref, gc_ref, beta_ref, gt_ref, onorm_ref,
                     o_ref, s_out_ref, s_ref, xs_ref, add_ref, oacc_ref, egl_ref, qc_ref, kc_ref, vc_ref,
                     xq_ref, xk_ref, xv_ref, *, nc):
    hg = pl.program_id(1)
    lb = pl.program_id(2)
    cz = DN_CHUNK

    def conv(x_ref, halo_ref, w_ref, xcat_ref, out_ref, post):
        n = x_ref.shape[1]
        xcat_ref[0:SUBLANES] = jnp.where(lb > 0, halo_ref[0], 0.0)
        xcat_ref[SUBLANES:SUBLANES + n] = x_ref[0]
        first = SUBLANES - (DN_CONV_K - 1)
        for r0 in range(0, n, DN_CONV_ROWS):
            acc = xcat_ref[first + r0:first + r0 + DN_CONV_ROWS] * w_ref[0:1, :]
            for i in range(1, DN_CONV_K):
                acc = acc + xcat_ref[first + r0 + i:first + r0 + i + DN_CONV_ROWS] * w_ref[i:i + 1, :]
            out_ref[r0:r0 + DN_CONV_ROWS] = post(jax.nn.silu(acc))

    conv(q_ref, qh_ref, wq_ref, xq_ref, qc_ref, lambda c: _l2norm_heads(c, DN_K_DIM ** -0.5))
    conv(k_ref, kh_ref, wk_ref, xk_ref, kc_ref, lambda c: _l2norm_heads(c, 1.0))
    conv(v_ref, vh_ref, wv_ref, xv_ref, vc_ref, lambda c: c)

    @pl.when(lb == 0)
    def _():
        s_ref[...] = jnp.zeros_like(s_ref)

    gc_all = gc_ref[0]
    beta_all = beta_ref[0]
    lane = lax.broadcasted_iota(jnp.int32, gc_all.shape, 1)
    ri = lax.broadcasted_iota(jnp.int32, (cz, cz), 0)
    ci = lax.broadcasted_iota(jnp.int32, (cz, cz), 1)
    incl = (ri >= ci)[None]
    strict = (ri > ci)[None]

    rep = DN_V_HEADS // DN_K_HEADS
    for g0 in range(0, DN_GROUP, DN_HBATCH):
        gs = list(range(g0, g0 + DN_HBATCH))
        gcol_all, bcol_all, grow_all = {}, {}, {}
        for g in gs:
            head = hg * DN_GROUP + g
            gcol_all[g] = jnp.sum(jnp.where(lane == head, gc_all, 0.0), axis=-1, keepdims=True)
            bcol_all[g] = jnp.sum(jnp.where(lane == head, beta_all, 0.0), axis=-1, keepdims=True)
            grow_all[g] = gt_ref[0, pl.ds(head, 1), :]
        for c0 in range(0, nc, DN_BATCH):
            cb = min(DN_BATCH, nc - c0)
            rows = slice(c0 * cz, (c0 + cb) * cz)
            stack = lambda fn: jnp.concatenate([fn(g) for g in gs], axis=0)
            kcols = lambda g: slice(g // rep * DN_K_DIM, (g // rep + 1) * DN_K_DIM)
            q = stack(lambda g: qc_ref[rows, kcols(g)].reshape(cb, cz, DN_K_DIM))
            k = stack(lambda g: kc_ref[rows, kcols(g)].reshape(cb, cz, DN_K_DIM))
            v = stack(lambda g: vc_ref[rows, g * DN_V_DIM:(g + 1) * DN_V_DIM].reshape(cb, cz, DN_V_DIM))
            gcol = stack(lambda g: gcol_all[g][rows].reshape(cb, cz, 1))
            bcol = stack(lambda g: bcol_all[g][rows].reshape(cb, cz, 1))
            grow = stack(lambda g: jnp.stack([grow_all[g][:, c * cz:(c + 1) * cz]
                                              for c in range(c0, c0 + cb)]))
            decay = jnp.where(incl, jnp.exp(gcol - grow), 0.0)
            kb = k * bcol
            kbf = k.astype(BF16)
            kk = jnp.einsum('cid,cjd->cij', kb.astype(BF16), kbf, preferred_element_type=F32)
            neg_l = jnp.where(strict, -(kk * decay), 0.0)
            n_acc = neg_l
            pw = neg_l
            for _ in range(5):
                pwb = pw.astype(BF16)
                pw = jnp.einsum('cij,cjk->cik', pwb, pwb, preferred_element_type=F32)
                n_acc = n_acc + pw + jnp.einsum('cij,cjk->cik', n_acc.astype(BF16), pw.astype(BF16),
                                                preferred_element_type=F32)
            egc = jnp.exp(gcol)
            rhs = jnp.concatenate([kb * egc, v * bcol], axis=-1)
            bmm = lambda x, y: jnp.einsum('cij,cjd->cid', x, y, preferred_element_type=F32)
            wu = (rhs + bmm(n_acc.astype(BF16), rhs.astype(BF16))).astype(BF16)
            qk = jnp.einsum('cid,cjd->cij', q.astype(BF16), kbf, preferred_element_type=F32) * decay
            a_wu = bmm(qk.astype(BF16), wu)
            glast = gcol[:, cz - 1:cz, :]
            kd = k * jnp.exp(glast - gcol)
            qa = (q * egc - a_wu[..., :DN_K_DIM]).astype(BF16)
            egl = jnp.broadcast_to(jnp.exp(glast), (len(gs) * cb, 1, DN_V_DIM))
            for n, g in enumerate(gs):
                for c in range(cb):
                    kd_wu = _dot(jnp.transpose(kd[n * cb + c]).astype(BF16), wu[n * cb + c])
                    xs_ref[g, c0 + c, 0:DN_K_DIM, :] = (-kd_wu[:, :DN_K_DIM]).astype(BF16)
                    add_ref[g, c0 + c, 0:DN_K_DIM, :] = kd_wu[:, DN_K_DIM:]
                xs_ref[g, c0:c0 + cb, DN_K_DIM:DN_K_DIM + cz, :] = qa[n * cb:(n + 1) * cb]
                add_ref[g, c0:c0 + cb, DN_K_DIM:DN_K_DIM + cz, :] = a_wu[n * cb:(n + 1) * cb, :, DN_K_DIM:]
                egl_ref[g, c0:c0 + cb] = egl[n * cb:(n + 1) * cb]

    def chunk(c, carry):
        for g in range(DN_GROUP):
            s = s_ref[g]
            r = _dot(xs_ref[g, c], s.astype(BF16)) + add_ref[g, c]
            oacc_ref[g, pl.ds(pl.multiple_of(c * cz, cz), cz), :] = r[DN_K_DIM:DN_K_DIM + cz]
            s_ref[g] = s * egl_ref[g, c] + r[0:DN_K_DIM]
        return carry

    lax.fori_loop(0, nc, chunk, 0)

    for g in range(DN_GROUP):
        o = oacc_ref[g]
        z = z_ref[0, :, g * DN_V_DIM:(g + 1) * DN_V_DIM]
        o_ref[0, :, g * DN_V_DIM:(g + 1) * DN_V_DIM] = (
            _rms(o, onorm_ref[...]) * jax.nn.silu(z)).astype(o_ref.dtype)

    @pl.when(lb == pl.num_programs(2) - 1)
    def _():
        s_out_ref[0] = s_ref[...]


def dn_chunked(proj, conv_w, gcum, beta, gcum_t, out_norm):
    b, l, _ = proj.shape
    lbk = min(DN_LBLOCK, l)
    nc = lbk // DN_CHUNK
    gk = DN_GROUP // 2 * DN_K_DIM
    gv = DN_GROUP * DN_V_DIM
    cz = DN_CHUNK
    qcol = lambda h: h
    kcol = lambda h: DN_QK_W // gk + h
    vcol = lambda h: 2 * DN_QK_W // gv + h
    hb = lbk // SUBLANES
    cur = lambda w, col: pl.BlockSpec((1, lbk, w), lambda i, h, t: (i, t, col(h)))
    halo = lambda w, col: pl.BlockSpec((1, SUBLANES, w), lambda i, h, t: (i, jnp.maximum(t * hb - 1, 0), col(h)))
    taps = lambda w, col: pl.BlockSpec((DN_CONV_K, w), lambda i, h, t: (0, col(h)))
    return pl.pallas_call(
        functools.partial(_dn_chunk_kernel, nc=nc),
        grid=(b, DN_V_HEADS // DN_GROUP, l // lbk),
        in_specs=[cur(gk, qcol), cur(gk, kcol), cur(gv, vcol),
                  halo(gk, qcol), halo(gk, kcol), halo(gv, vcol),
                  taps(gk, qcol), taps(gk, kcol), taps(gv, vcol),
                  pl.BlockSpec((1, lbk, gv), lambda i, h, t: (i, t, DN_CONV_DIM // gv + h)),
                  pl.BlockSpec((1, lbk, DN_V_HEADS), lambda i, h, t: (i, t, 0)),
                  pl.BlockSpec((1, lbk, DN_V_HEADS), lambda i, h, t: (i, t, 0)),
                  pl.BlockSpec((1, DN_V_HEADS, lbk), lambda i, h, t: (i, 0, t)),
                  pl.BlockSpec((1, DN_V_DIM), lambda i, h, t: (0, 0))],
        out_specs=[pl.BlockSpec((1, lbk, gv), lambda i, h, t: (i, t, h)),
                   pl.BlockSpec((1, DN_GROUP, DN_K_DIM, DN_V_DIM), lambda i, h, t: (i, h, 0, 0))],
        out_shape=[jax.ShapeDtypeStruct((b, l, DN_V_W), BF16),
                   jax.ShapeDtypeStruct((b, DN_V_HEADS, DN_K_DIM, DN_V_DIM), F32)],
        scratch_shapes=[pltpu.VMEM((DN_GROUP, DN_K_DIM, DN_V_DIM), F32),
                        pltpu.VMEM((DN_GROUP, nc, DN_K_DIM + cz, DN_K_DIM), BF16),
                        pltpu.VMEM((DN_GROUP, nc, DN_K_DIM + cz, DN_V_DIM), F32),
                        pltpu.VMEM((DN_GROUP, lbk, DN_V_DIM), F32),
                        pltpu.VMEM((DN_GROUP, nc, 1, DN_V_DIM), F32),
                        pltpu.VMEM((lbk, gk), F32), pltpu.VMEM((lbk, gk), F32), pltpu.VMEM((lbk, gv), F32),
                        pltpu.VMEM((lbk + SUBLANES, gk), F32), pltpu.VMEM((lbk + SUBLANES, gk), F32),
                        pltpu.VMEM((lbk + SUBLANES, gv), F32)],
        compiler_params=_params("parallel", "parallel", "arbitrary"),
        name="dn_chunked",
    )(proj, proj, proj, proj, proj, proj, conv_w, conv_w, conv_w, proj, gcum, beta, gcum_t,
      out_norm.reshape(1, DN_V_DIM))


DN_ROWS = DN_CONV_DIM // LANES
DN_QROWS = DN_K_HEADS
DN_VROW0 = 2 * DN_K_HEADS


def _dn_sample_kernel(x_ref, buf_ref, w_ref, z_ref, eg_ref, beta_ref, s_ref, onorm_ref,
                      o_ref, s_out_ref, buf_out_ref):
    b = pl.program_id(0)
    x = x_ref[0]
    buf = buf_ref[0]
    conv = buf[0] * w_ref[0]
    for i in range(1, DN_CONV_K - 1):
        conv = conv + buf[i] * w_ref[i]
    conv = conv + x * w_ref[DN_CONV_K - 1]
    buf_out_ref[0, 0:DN_CONV_K - 2] = buf[1:DN_CONV_K - 1]
    buf_out_ref[0, DN_CONV_K - 2] = x
    c = jax.nn.silu(conv)
    qk = c[0:DN_VROW0]
    qk = qk * lax.rsqrt(jnp.sum(qk * qk, axis=-1, keepdims=True) + EPS)
    q_t = jnp.transpose(qk[0:DN_QROWS] * (DN_K_DIM ** -0.5))
    k_t = jnp.transpose(qk[DN_QROWS:DN_VROW0])
    for h in range(DN_V_HEADS):
        kh = h // (DN_V_HEADS // DN_K_HEADS)
        kcol = k_t[:, kh:kh + 1]
        qcol = q_t[:, kh:kh + 1]
        v = c[DN_VROW0 + h:DN_VROW0 + h + 1]
        s = s_ref[0, h] * eg_ref[b, h]
        delta = (v - jnp.sum(kcol * s, axis=0, keepdims=True)) * beta_ref[b, h]
        s = s + kcol * delta
        s_out_ref[0, h] = s
        o = jnp.sum(qcol * s, axis=0, keepdims=True)
        o_ref[0, h:h + 1, :] = (_rms(o, onorm_ref[...]) * jax.nn.silu(z_ref[0, h:h + 1, :])
                                ).astype(o_ref.dtype)


def dn_sample(qkv_rows, conv_buf, conv_w, z_rows, eg, beta, state, out_norm):
    nb = qkv_rows.shape[0]
    smem = pl.BlockSpec(memory_space=pltpu.SMEM)
    return pl.pallas_call(
        _dn_sample_kernel,
        grid=(nb,),
        in_specs=[pl.BlockSpec((1, DN_ROWS, LANES), lambda i: (i, 0, 0)),
                  pl.BlockSpec((1, DN_CONV_K - 1, DN_ROWS, LANES), lambda i: (i, 0, 0, 0)),
                  pl.BlockSpec((DN_CONV_K, DN_ROWS, LANES), lambda i: (0, 0, 0)),
                  pl.BlockSpec((1, DN_V_HEADS, DN_V_DIM), lambda i: (i, 0, 0)),
                  smem, smem,
                  pl.BlockSpec((1, DN_V_HEADS, DN_K_DIM, DN_V_DIM), lambda i: (i, 0, 0, 0)),
                  pl.BlockSpec((1, DN_V_DIM), lambda i: (0, 0))],
        out_specs=[pl.BlockSpec((1, DN_V_HEADS, DN_V_DIM), lambda i: (i, 0, 0)),
                   pl.BlockSpec((1, DN_V_HEADS, DN_K_DIM, DN_V_DIM), lambda i: (i, 0, 0, 0)),
                   pl.BlockSpec((1, DN_CONV_K - 1, DN_ROWS, LANES), lambda i: (i, 0, 0, 0))],
        out_shape=[jax.ShapeDtypeStruct((nb, DN_V_HEADS, DN_V_DIM), BF16),
                   jax.ShapeDtypeStruct(state.shape, F32),
                   jax.ShapeDtypeStruct(conv_buf.shape, F32)],
        compiler_params=_params("arbitrary"),
        name="dn_sample",
    )(qkv_rows, conv_buf, conv_w, z_rows, eg, beta, state, out_norm.reshape(1, DN_V_DIM))


def _tile(m, cap):
    return min(m, cap)


def _even_layer(x, cache, w, *, batch, seq):
    m = x.shape[0]
    tm = _tile(m, 1024)
    qkv = norm_matmul(x, w['norm_mix'], w['w_in_qkv'], tm=tm, tn=512)
    u = norm_matmul(x, w['norm_mix'], w['w_in_u'], tm=tm, tn=512)
    if cache is None:
        attn, new_k, new_v = attn_prompt(qkv, w['q_norm'], w['k_norm'], w['sinks'], batch=batch, seq=seq)
        zeros = jnp.zeros((batch, S5_LANES), F32)
        ssm, h_re, h_im = s5_mix(u.reshape(batch, seq, S5_WIDTH), zeros, zeros, w['s5'],
                                 rows=batch, seq=seq, steps=64)
        ssm = ssm.reshape(m, S5_WIDTH)
    else:
        k_win, v_win, h0_re, h0_im = cache
        attn, new_k, new_v = attn_sample(qkv, k_win.reshape(batch, WINDOW, KV_WIDTH),
                                         v_win.reshape(batch, WINDOW, KV_WIDTH),
                                         w['q_norm'], w['k_norm'], w['sinks'])
        ssm, h_re, h_im = s5_mix(u, h0_re.reshape(batch, S5_LANES), h0_im.reshape(batch, S5_LANES),
                                 w['s5'], rows=batch, seq=1, steps=1)
    x = matmul_residual(x, [attn, ssm], [w['w_out_a'], w['w_out_b']], tm=tm, tn=512)
    x = ffn(x, w['norm_ffn'], w['ffn_gate'], w['ffn_up'], w['ffn_down'], tm=tm, tf=512)
    shp = (batch, WINDOW, N_KV_HEADS, HEAD_DIM)
    st = (batch, S5_GROUPS, S5_STATE)
    return x, new_k.reshape(shp), new_v.reshape(shp), h_re.reshape(st), h_im.reshape(st)


def _odd_layer(x, cache, w, *, batch, seq):
    m = x.shape[0]
    tm = _tile(m, 1024)
    proj = norm_matmul(x, w['norm_mix'], w['w_in_main'], tm=tm, tn=1024)
    ba = norm_matmul(x, w['norm_mix'], w['w_in_ba'], tm=tm, tn=LANES)
    beta, eg, gcum = dn_gates(ba, w['a_log'], w['dt_bias'], tg=_tile(m, 512))
    if cache is None:
        proj3 = proj.reshape(batch, seq, DN_CONV_DIM + DN_V_W)
        gcum3 = gcum.reshape(batch, seq, DN_V_HEADS)
        o, s_new = dn_chunked(proj3, w['conv_w'], gcum3, beta.reshape(batch, seq, DN_V_HEADS),
                              jnp.swapaxes(gcum3, 1, 2), w['out_norm'])
        o = o.reshape(m, DN_V_W)
        new_buf = proj3[:, seq - (DN_CONV_K - 1):, :DN_CONV_DIM]
    else:
        s0, conv_buf = cache
        o, s_new, new_buf = dn_sample(
            proj[:, :DN_CONV_DIM].reshape(batch, DN_ROWS, LANES),
            conv_buf.reshape(batch, DN_CONV_K - 1, DN_ROWS, LANES),
            w['conv_w'].reshape(DN_CONV_K, DN_ROWS, LANES),
            proj[:, DN_CONV_DIM:].reshape(batch, DN_V_HEADS, DN_V_DIM),
            eg, beta, s0, w['out_norm'])
        o = o.reshape(m, DN_V_W)
        new_buf = new_buf.reshape(batch, DN_CONV_K - 1, DN_CONV_DIM)
    x = matmul_residual(x, [o], [w['w_out']], tm=tm, tn=512)
    x = moe_sparse(x, w['norm_ffn'], w['router'], w['exp_gate'], w['exp_up'], w['exp_down'])
    return x, s_new, new_buf


def kernel(x_prompt, x_sample, cache_win_k, cache_win_v, state_s5_re, state_s5_im, state_dn, state_dn_conv,
           e_norm_mix, e_w_in, e_q_norm, e_k_norm, e_sinks,
           e_s5_a_re, e_s5_a_im, e_s5_log_dt, e_s5_b_re, e_s5_b_im, e_s5_c_re, e_s5_c_im, e_s5_d, e_s5_w_glu,
           e_w_out, e_norm_ffn, e_ffn_w_gate, e_ffn_w_up, e_ffn_w_down,
           o_norm_mix, o_w_in, o_conv_w, o_a_log, o_dt_bias, o_out_norm, o_w_out, o_norm_ffn,
           o_router, o_exp_w_gate, o_exp_w_up, o_exp_w_down):
    bp, lp, d = x_prompt.shape
    bs, ls, _ = x_sample.shape
    assert ls == 1, "the sample group advances one token per step"
    hp = x_prompt.reshape(bp * lp, d)
    hs = x_sample.reshape(bs * ls, d)
    qkv_w = ATTN_WIDTH + 2 * KV_WIDTH
    main_w = DN_CONV_DIM + DN_V_W

    j = 0
    we = dict(
        norm_mix=e_norm_mix[j], q_norm=e_q_norm[j], k_norm=e_k_norm[j], sinks=e_sinks[j],
        w_in_qkv=e_w_in[j, :, :qkv_w].astype(BF16), w_in_u=e_w_in[j, :, qkv_w:].astype(BF16),
        s5=s5_constants(e_s5_a_re[j], e_s5_a_im[j], e_s5_log_dt[j], e_s5_b_re[j], e_s5_b_im[j],
                        e_s5_c_re[j], e_s5_c_im[j], e_s5_d[j], e_s5_w_glu[j]),
        w_out_a=e_w_out[j, :ATTN_WIDTH].astype(BF16), w_out_b=e_w_out[j, ATTN_WIDTH:].astype(BF16),
        norm_ffn=e_norm_ffn[j], ffn_gate=e_ffn_w_gate[j].astype(BF16),
        ffn_up=e_ffn_w_up[j].astype(BF16), ffn_down=e_ffn_w_down[j].astype(BF16))
    ba_pad = jnp.zeros((d, LANES), F32).at[:, :2 * DN_V_HEADS].set(o_w_in[j, :, main_w:])
    wo = dict(
        norm_mix=o_norm_mix[j], w_in_main=o_w_in[j, :, :main_w].astype(BF16), w_in_ba=ba_pad.astype(BF16),
        conv_w=o_conv_w[j], a_log=o_a_log[j], dt_bias=o_dt_bias[j], out_norm=o_out_norm[j],
        w_out=o_w_out[j].astype(BF16), norm_ffn=o_norm_ffn[j], router=o_router[j],
        exp_gate=o_exp_w_gate[j].astype(BF16), exp_up=o_exp_w_up[j].astype(BF16),
        exp_down=o_exp_w_down[j].astype(BF16))

    hp, kp, vp, rp, ip = _even_layer(hp, None, we, batch=bp, seq=lp)
    hs, ks, vs, rs, is_ = _even_layer(
        hs, (cache_win_k[j], cache_win_v[j], state_s5_re[j], state_s5_im[j]), we, batch=bs, seq=1)
    hp, sp, cp = _odd_layer(hp, None, wo, batch=bp, seq=lp)
    hs, ss, cs = _odd_layer(hs, (state_dn[j], state_dn_conv[j]), wo, batch=bs, seq=1)

    one = lambda t: t[None]
    return (hp.reshape(bp, lp, d), hs.reshape(bs, ls, d),
            one(kp), one(vp), one(rp), one(ip), one(sp), one(cp),
            one(ks), one(vs), one(rs), one(is_), one(ss), one(cs))
```

```python
import functools

import jax
import jax.numpy as jnp
from jax import lax
from jax.experimental import pallas as pl
from jax.experimental.pallas import tpu as pltpu

F32 = jnp.float32
BF16 = jnp.bfloat16

D_MODEL = 2048
N_HEADS = 16
N_KV_HEADS = 4
HEAD_DIM = 64
Q_PER_KV = N_HEADS // N_KV_HEADS
WINDOW = 128
ATTN_WIDTH = N_HEADS * HEAD_DIM
KV_WIDTH = N_KV_HEADS * HEAD_DIM
ATTN_SCALE = HEAD_DIM ** -0.5
S5_WIDTH = D_MODEL // 2
S5_GROUP_CH = 16
S5_GROUPS = S5_WIDTH // S5_GROUP_CH
S5_STATE = 64
S5_LANES = S5_GROUPS * S5_STATE
DN_K_HEADS = 16
DN_V_HEADS = 32
DN_K_DIM = 128
DN_V_DIM = 128
DN_CONV_K = 4
DN_CHUNK = 64
DN_QK_W = DN_K_HEADS * DN_K_DIM
DN_V_W = DN_V_HEADS * DN_V_DIM
DN_CONV_DIM = 2 * DN_QK_W + DN_V_W
D_FF = 5632
N_EXPERTS = 8
EPS = 1e-6
NEG_INF = -1e30

LANES = 128
SUBLANES = 8
VMEM_LIMIT = 56 * 1024 * 1024


def _params(*sem):
    return pltpu.CompilerParams(dimension_semantics=sem, vmem_limit_bytes=VMEM_LIMIT)


def _rms(x, g):
    return x * lax.rsqrt(jnp.mean(x * x, axis=-1, keepdims=True) + EPS) * g


def _dot(a, b):
    return jnp.dot(a, b, preferred_element_type=F32)


def _dot_nt(a, b):
    return lax.dot_general(a, b, (((1,), (1,)), ((), ())), preferred_element_type=F32)


def _split3(x):
    hi = x.astype(BF16)
    r = x - hi.astype(F32)
    mid = r.astype(BF16)
    lo = (r - mid.astype(F32)).astype(BF16)
    return hi, mid, lo


def _norm_matmul_kernel(x_ref, g_ref, w_ref, o_ref, xn_ref):
    @pl.when(pl.program_id(1) == 0)
    def _():
        xn_ref[...] = _rms(x_ref[...], g_ref[...]).astype(BF16)

    o_ref[...] = _dot(xn_ref[...], w_ref[...]).astype(o_ref.dtype)


def norm_matmul(x, g, w, *, tm, tn, out_dtype=F32):
    m, d = x.shape
    n = w.shape[1]
    return pl.pallas_call(
        _norm_matmul_kernel,
        grid=(m // tm, n // tn),
        in_specs=[pl.BlockSpec((tm, d), lambda i, j: (i, 0)),
                  pl.BlockSpec((1, d), lambda i, j: (0, 0)),
                  pl.BlockSpec((d, tn), lambda i, j: (0, j))],
        out_specs=pl.BlockSpec((tm, tn), lambda i, j: (i, j)),
        out_shape=jax.ShapeDtypeStruct((m, n), out_dtype),
        scratch_shapes=[pltpu.VMEM((tm, d), BF16)],
        compiler_params=_params("parallel", "arbitrary"),
        name="norm_matmul",
    )(x, g.reshape(1, d), w)


def _matmul_residual_kernel(*refs, n_pairs):
    x_ref = refs[0]
    o_ref = refs[-1]
    acc = x_ref[...]
    for a_ref, w_ref in zip(refs[1:1 + n_pairs], refs[1 + n_pairs:1 + 2 * n_pairs]):
        acc = acc + _dot(a_ref[...], w_ref[...])
    o_ref[...] = acc


def matmul_residual(x, a_list, w_list, *, tm, tn):
    m, n = x.shape
    in_specs = [pl.BlockSpec((tm, tn), lambda i, j: (i, j))]
    in_specs += [pl.BlockSpec((tm, a.shape[1]), lambda i, j: (i, 0)) for a in a_list]
    in_specs += [pl.BlockSpec((w.shape[0], tn), lambda i, j: (0, j)) for w in w_list]
    return pl.pallas_call(
        functools.partial(_matmul_residual_kernel, n_pairs=len(a_list)),
        grid=(m // tm, n // tn),
        in_specs=in_specs,
        out_specs=pl.BlockSpec((tm, tn), lambda i, j: (i, j)),
        out_shape=jax.ShapeDtypeStruct((m, n), F32),
        compiler_params=_params("parallel", "arbitrary"),
        name="matmul_residual",
    )(x, *a_list, *w_list)


HEADS_PER_TILE = LANES // HEAD_DIM


def _head_rms(x, seg, g):
    hi, mid, lo = _split3(x * x)
    parts = []
    for a in range(x.shape[1] // LANES):
        sl = slice(a * LANES, (a + 1) * LANES)
        parts.append(_dot(hi[:, sl], seg) + (_dot(mid[:, sl], seg) + _dot(lo[:, sl], seg)))
    ms = jnp.concatenate(parts, axis=1) * (1.0 / HEAD_DIM)
    return x * lax.rsqrt(ms + EPS) * g


def _attn_prompt_kernel(q_ref, kc_ref, kp_ref, vc_ref, vp_ref, qn_ref, kn_ref, sink_ref,
                        o_ref, wk_ref, wv_ref, s_ref, p_ref):
    has_prev = pl.program_id(1) > 0
    r = lax.broadcasted_iota(jnp.int32, (LANES, LANES), 0)
    c = lax.broadcasted_iota(jnp.int32, (LANES, LANES), 1)
    seg = jnp.where(r // HEAD_DIM == c // HEAD_DIM, 1.0, 0.0).astype(BF16)
    low = lax.broadcasted_iota(jnp.int32, (1, LANES), 1) < HEAD_DIM

    qn = _head_rms(q_ref[...], seg, qn_ref[...]) * ATTN_SCALE
    kc = _head_rms(kc_ref[...], seg, kn_ref[...])
    kp = _head_rms(kp_ref[...], seg, kn_ref[...])
    vc = vc_ref[...]
    wk_ref[0] = kc
    wv_ref[0] = vc
    kband = jnp.concatenate([kp, kc], axis=0)
    vband = jnp.concatenate([vp_ref[...], vc], axis=0)

    v_halves = []
    for kvh in range(N_KV_HEADS):
        tl = slice(kvh // HEADS_PER_TILE * LANES, (kvh // HEADS_PER_TILE + 1) * LANES)

        def both_halves(t):
            rolled = pltpu.roll(t, HEAD_DIM, axis=1)
            return jnp.where(low, t, rolled) if kvh % HEADS_PER_TILE == 0 else jnp.where(low, rolled, t)

        kdup = both_halves(kband[:, tl]).astype(BF16)
        vdup = both_halves(vband[:, tl])
        v_halves.append((jnp.where(low, vdup, 0.0).astype(BF16), jnp.where(low, 0.0, vdup).astype(BF16)))
        for pr in range(Q_PER_KV // HEADS_PER_TILE):
            a = kvh * Q_PER_KV // HEADS_PER_TILE + pr
            qt = qn[:, a * LANES:(a + 1) * LANES]
            s_ref[HEADS_PER_TILE * a] = _dot_nt(jnp.where(low, qt, 0.0).astype(BF16), kdup)
            s_ref[HEADS_PER_TILE * a + 1] = _dot_nt(jnp.where(low, 0.0, qt).astype(BF16), kdup)

    row = lax.broadcasted_iota(jnp.int32, (WINDOW, 2 * WINDOW), 0)
    col = lax.broadcasted_iota(jnp.int32, (WINDOW, 2 * WINDOW), 1)
    rel = (WINDOW + row) - col
    mask = (rel >= 0) & (rel <= WINDOW) & ((col >= WINDOW) | has_prev)
    for h in range(N_HEADS):
        s = jnp.where(mask, s_ref[h], NEG_INF)
        sk = sink_ref[h:h + 1, :]
        m = jnp.maximum(jnp.max(s, axis=-1, keepdims=True), sk)
        p = jnp.exp(s - m)
        p = p / (jnp.sum(p, axis=-1, keepdims=True) + jnp.exp(sk - m))
        p_ref[h] = p.astype(BF16)

    for a in range(N_HEADS // HEADS_PER_TILE):
        v_lo, v_hi = v_halves[a * HEADS_PER_TILE // Q_PER_KV]
        o = _dot(p_ref[HEADS_PER_TILE * a], v_lo) + _dot(p_ref[HEADS_PER_TILE * a + 1], v_hi)
        o_ref[:, a * LANES:(a + 1) * LANES] = o.astype(o_ref.dtype)


def attn_prompt(qkv, q_norm, k_norm, sinks, *, batch, seq):
    nb = seq // WINDOW
    kcol = ATTN_WIDTH // KV_WIDTH
    cur = lambda c: (lambda b, n: (b * nb + n, c))
    prev = lambda c: (lambda b, n: (jnp.maximum(b * nb + n - 1, 0), c))
    return pl.pallas_call(
        _attn_prompt_kernel,
        grid=(batch, nb),
        in_specs=[pl.BlockSpec((WINDOW, ATTN_WIDTH), cur(0)),
                  pl.BlockSpec((WINDOW, KV_WIDTH), cur(kcol)),
                  pl.BlockSpec((WINDOW, KV_WIDTH), prev(kcol)),
                  pl.BlockSpec((WINDOW, KV_WIDTH), cur(kcol + 1)),
                  pl.BlockSpec((WINDOW, KV_WIDTH), prev(kcol + 1)),
                  pl.BlockSpec((1, ATTN_WIDTH), lambda b, n: (0, 0)),
                  pl.BlockSpec((1, KV_WIDTH), lambda b, n: (0, 0)),
                  pl.BlockSpec((N_HEADS, 1), lambda b, n: (0, 0))],
        out_specs=[pl.BlockSpec((WINDOW, ATTN_WIDTH), lambda b, n: (b * nb + n, 0)),
                   pl.BlockSpec((1, WINDOW, KV_WIDTH), lambda b, n: (b, 0, 0)),
                   pl.BlockSpec((1, WINDOW, KV_WIDTH), lambda b, n: (b, 0, 0))],
        out_shape=[jax.ShapeDtypeStruct((batch * seq, ATTN_WIDTH), BF16),
                   jax.ShapeDtypeStruct((batch, WINDOW, KV_WIDTH), F32),
                   jax.ShapeDtypeStruct((batch, WINDOW, KV_WIDTH), F32)],
        scratch_shapes=[pltpu.VMEM((N_HEADS, WINDOW, 2 * WINDOW), F32),
                        pltpu.VMEM((N_HEADS, WINDOW, 2 * WINDOW), BF16)],
        compiler_params=_params("parallel", "arbitrary"),
        name="attn_prompt",
    )(qkv, qkv, qkv, qkv, qkv, jnp.tile(q_norm, N_HEADS).reshape(1, ATTN_WIDTH),
      jnp.tile(k_norm, N_KV_HEADS).reshape(1, KV_WIDTH), sinks.reshape(N_HEADS, 1))


def _attn_sample_kernel(qkv_ref, ck_ref, cv_ref, qn_ref, kn_ref, sink_ref,
                        o_ref, wk_ref, wv_ref, *, bt):
    qg = qn_ref[...]
    kg = kn_ref[...]
    kv0 = ATTN_WIDTH
    v0 = ATTN_WIDTH + KV_WIDTH
    wk_ref[:, 0:WINDOW - 1, :] = ck_ref[:, 1:WINDOW, :]
    wv_ref[:, 0:WINDOW - 1, :] = cv_ref[:, 1:WINDOW, :]
    wv_ref[:, WINDOW - 1:WINDOW, :] = qkv_ref[:, v0:v0 + KV_WIDTH][:, None, :]
    for kvh in range(N_KV_HEADS):
        sl = slice(kvh * HEAD_DIM, (kvh + 1) * HEAD_DIM)
        kn = _rms(qkv_ref[:, kv0 + kvh * HEAD_DIM:kv0 + (kvh + 1) * HEAD_DIM], kg)[:, None, :]
        vn = qkv_ref[:, v0 + kvh * HEAD_DIM:v0 + (kvh + 1) * HEAD_DIM][:, None, :]
        wk_ref[:, WINDOW - 1:WINDOW, sl] = kn
        q4 = jnp.concatenate(
            [_rms(qkv_ref[:, (kvh * Q_PER_KV + g) * HEAD_DIM:(kvh * Q_PER_KV + g + 1) * HEAD_DIM], qg)[:, None, :]
             for g in range(Q_PER_KV)], axis=1)
        kwin = ck_ref[:, :, sl].astype(BF16)
        vwin = cv_ref[:, :, sl].astype(BF16)
        s_c = jnp.einsum('bqd,bkd->bqk', q4.astype(BF16), kwin, preferred_element_type=F32) * ATTN_SCALE
        s_n = jnp.sum(q4 * kn, axis=-1, keepdims=True) * ATTN_SCALE
        sk = sink_ref[kvh * Q_PER_KV:(kvh + 1) * Q_PER_KV, :][None]
        m = jnp.maximum(jnp.maximum(jnp.max(s_c, axis=-1, keepdims=True), s_n), sk)
        p_c = jnp.exp(s_c - m)
        p_n = jnp.exp(s_n - m)
        den = jnp.sum(p_c, axis=-1, keepdims=True) + p_n + jnp.exp(sk - m)
        o4 = jnp.einsum('bqk,bkd->bqd', (p_c / den).astype(BF16), vwin,
                        preferred_element_type=F32) + (p_n / den) * vn
        for g in range(Q_PER_KV):
            h = kvh * Q_PER_KV + g
            o_ref[:, h * HEAD_DIM:(h + 1) * HEAD_DIM] = o4[:, g, :].astype(o_ref.dtype)


def attn_sample(qkv, cache_k, cache_v, q_norm, k_norm, sinks, *, bt=8):
    nb = qkv.shape[0]
    width = qkv.shape[1]
    win = pl.BlockSpec((bt, WINDOW, KV_WIDTH), lambda i: (i, 0, 0))
    return pl.pallas_call(
        functools.partial(_attn_sample_kernel, bt=bt),
        grid=(nb // bt,),
        in_specs=[pl.BlockSpec((bt, width), lambda i: (i, 0)), win, win,
                  pl.BlockSpec((1, HEAD_DIM), lambda i: (0, 0)),
                  pl.BlockSpec((1, HEAD_DIM), lambda i: (0, 0)),
                  pl.BlockSpec((N_HEADS, 1), lambda i: (0, 0))],
        out_specs=[pl.BlockSpec((bt, ATTN_WIDTH), lambda i: (i, 0)), win, win],
        out_shape=[jax.ShapeDtypeStruct((nb, ATTN_WIDTH), BF16),
                   jax.ShapeDtypeStruct((nb, WINDOW, KV_WIDTH), F32),
                   jax.ShapeDtypeStruct((nb, WINDOW, KV_WIDTH), F32)],
        compiler_params=_params("parallel"),
        name="attn_sample",
    )(qkv, cache_k, cache_v, q_norm.reshape(1, HEAD_DIM), k_norm.reshape(1, HEAD_DIM),
      sinks.reshape(N_HEADS, 1))


S5_KCH = S5_WIDTH // LANES
S5_CHUNK_STATES = S5_LANES // S5_KCH
S5_SCAN_TILES = 4


def _s5_kernel(u_ref, h0re_ref, h0im_ref, are_ref, aim_ref, wbu_ref, wc_ref, d_ref, wglu_ref,
               y_ref, hre_ref, him_ref, hs_ref, st_ref, *, rows, steps):
    rt = rows * steps
    u = u_ref[...].reshape(rt, S5_WIDTH)

    @pl.when(pl.program_id(0) == 0)
    def _():
        st_ref[:, :S5_LANES] = h0re_ref[...]
        st_ref[:, S5_LANES:] = h0im_ref[...]

    if steps > 1:
        assert rows == SUBLANES
        r = lax.broadcasted_iota(jnp.int32, (rt, rt), 0)
        c = lax.broadcasted_iota(jnp.int32, (rt, rt), 1)
        to_time_major = jnp.where(c == (r % rows) * steps + r // rows, 1.0, 0.0).astype(BF16)
        to_seq_major = jnp.where(c == (r % steps) * rows + r // steps, 1.0, 0.0).astype(BF16)
        hi, mid, lo = _split3(u)
        u_hi = _dot(to_time_major, hi)
        u = u_hi + (_dot(to_time_major, mid) + _dot(to_time_major, lo))
        ub = u_hi.astype(BF16)
    else:
        ub = u.astype(BF16)
    nre = S5_LANES // LANES
    tpk = S5_CHUNK_STATES // LANES
    for k in range(S5_KCH):
        r = _dot(ub[:, k * LANES:(k + 1) * LANES], wbu_ref[k])
        for a in range(tpk):
            hs_ref[k * tpk + a] = r[:, a * LANES:(a + 1) * LANES]
            hs_ref[nre + k * tpk + a] = r[:, (tpk + a) * LANES:(tpk + a + 1) * LANES]

    nt = S5_SCAN_TILES
    for j in range(nre // nt):
        tiles = range(j * nt, (j + 1) * nt)
        ar = [jnp.broadcast_to(are_ref[:, a * LANES:(a + 1) * LANES], (SUBLANES, LANES)) for a in tiles]
        ai = [jnp.broadcast_to(aim_ref[:, a * LANES:(a + 1) * LANES], (SUBLANES, LANES)) for a in tiles]

        def advance(r8, hr, hi):
            nr, ni = [], []
            for n, a in enumerate(tiles):
                nr.append(ar[n] * hr[n] - ai[n] * hi[n] + hs_ref[a, r8, :])
                ni.append(ar[n] * hi[n] + ai[n] * hr[n] + hs_ref[nre + a, r8, :])
                hs_ref[a, r8, :] = nr[n]
                hs_ref[nre + a, r8, :] = ni[n]
            return nr, ni

        if steps == 1:
            def group(rg, carry):
                r8 = pl.ds(pl.multiple_of(rg * SUBLANES, SUBLANES), SUBLANES)
                hr = [st_ref[r8, a * LANES:(a + 1) * LANES] for a in tiles]
                hi = [st_ref[r8, S5_LANES + a * LANES:S5_LANES + (a + 1) * LANES] for a in tiles]
                nr, ni = advance(r8, hr, hi)
                for n, a in enumerate(tiles):
                    st_ref[r8, a * LANES:(a + 1) * LANES] = nr[n]
                    st_ref[r8, S5_LANES + a * LANES:S5_LANES + (a + 1) * LANES] = ni[n]
                return carry
            lax.fori_loop(0, rows // SUBLANES, group, 0)
        else:
            def step(t, carry):
                nr, ni = advance(pl.ds(pl.multiple_of(t * SUBLANES, SUBLANES), SUBLANES), *carry)
                return tuple(nr), tuple(ni)
            hr0 = tuple(st_ref[:, a * LANES:(a + 1) * LANES] for a in tiles)
            hi0 = tuple(st_ref[:, S5_LANES + a * LANES:S5_LANES + (a + 1) * LANES] for a in tiles)
            hr, hi = lax.fori_loop(0, steps, step, (hr0, hi0))
            for n, a in enumerate(tiles):
                st_ref[:, a * LANES:(a + 1) * LANES] = hr[n]
                st_ref[:, S5_LANES + a * LANES:S5_LANES + (a + 1) * LANES] = hi[n]

    ys = []
    for k in range(S5_KCH):
        hre = jnp.concatenate([hs_ref[k * tpk + a] for a in range(tpk)], axis=1).astype(BF16)
        him = jnp.concatenate([hs_ref[nre + k * tpk + a] for a in range(tpk)], axis=1).astype(BF16)
        yk = _dot(hre, wc_ref[0, k]) + _dot(him, wc_ref[1, k])
        ys.append(yk + d_ref[:, k * LANES:(k + 1) * LANES] * u[:, k * LANES:(k + 1) * LANES])
    y = jax.nn.gelu(jnp.concatenate(ys, axis=1))
    out = (y * jax.nn.sigmoid(_dot(y.astype(BF16), wglu_ref[...]))).astype(y_ref.dtype)
    if steps > 1:
        out = _dot(to_seq_major, out).astype(y_ref.dtype)
    y_ref[...] = out.reshape(y_ref.shape)

    @pl.when(pl.program_id(0) == pl.num_programs(0) - 1)
    def _():
        hre_ref[...] = st_ref[:, :S5_LANES]
        him_ref[...] = st_ref[:, S5_LANES:]


def s5_mix(u, h0_re, h0_im, consts, *, rows, seq, steps):
    a_re, a_im, w_bu, w_c, d_skip, w_glu = consts
    if seq > 1:
        u_spec = pl.BlockSpec((rows, steps, S5_WIDTH), lambda c: (0, c, 0))
    else:
        u_spec = pl.BlockSpec((rows, S5_WIDTH), lambda c: (0, 0))
    full = lambda shape: pl.BlockSpec(shape, lambda c: (0,) * len(shape))
    return pl.pallas_call(
        functools.partial(_s5_kernel, rows=rows, steps=steps),
        grid=(seq // steps,),
        in_specs=[u_spec, full((rows, S5_LANES)), full((rows, S5_LANES)),
                  full((1, S5_LANES)), full((1, S5_LANES)),
                  full(w_bu.shape), full(w_c.shape), full((1, S5_WIDTH)), full(w_glu.shape)],
        out_specs=[u_spec, full((rows, S5_LANES)), full((rows, S5_LANES))],
        out_shape=[jax.ShapeDtypeStruct(u.shape, BF16),
                   jax.ShapeDtypeStruct((rows, S5_LANES), F32),
                   jax.ShapeDtypeStruct((rows, S5_LANES), F32)],
        scratch_shapes=[pltpu.VMEM((2 * S5_LANES // LANES, rows * steps, LANES), F32),
                        pltpu.VMEM((rows, 2 * S5_LANES), F32)],
        compiler_params=_params("arbitrary"),
        name="s5_mix",
    )(u, h0_re, h0_im, a_re, a_im, w_bu, w_c, d_skip, w_glu)


def s5_constants(a_re, a_im, log_dt, b_re, b_im, c_re, c_im, d_skip, w_glu):
    lr, li = a_re, a_im
    dt = jnp.exp(log_dt)[:, None]
    mag = jnp.exp(lr * dt)
    ab_re, ab_im = mag * jnp.cos(li * dt), mag * jnp.sin(li * dt)
    den = lr * lr + li * li
    f_re = ((ab_re - 1.0) * lr + ab_im * li) / den
    f_im = (ab_im * lr - (ab_re - 1.0) * li) / den
    bb_re = f_re[..., None] * b_re - f_im[..., None] * b_im
    bb_im = f_re[..., None] * b_im + f_im[..., None] * b_re
    gpc = LANES // S5_GROUP_CH
    eye = jnp.eye(gpc, dtype=F32)

    def bu_blocks(bb):
        t = bb.reshape(S5_KCH, gpc, S5_STATE, S5_GROUP_CH)
        return jnp.einsum('kgpc,gh->kgchp', t, eye).reshape(S5_KCH, LANES, gpc * S5_STATE)

    def c_blocks(c):
        t = c.reshape(S5_KCH, gpc, S5_GROUP_CH, S5_STATE)
        return jnp.einsum('kgcp,gh->kgphc', t, eye).reshape(S5_KCH, gpc * S5_STATE, LANES)

    w_bu = jnp.concatenate([bu_blocks(bb_re), bu_blocks(bb_im)], axis=-1).astype(BF16)
    w_c = jnp.stack([c_blocks(c_re), -c_blocks(c_im)]).astype(BF16)
    return (ab_re.reshape(1, S5_LANES), ab_im.reshape(1, S5_LANES), w_bu, w_c,
            d_skip.reshape(1, S5_WIDTH), w_glu.astype(BF16))


def _ffn_kernel(x_ref, g_ref, wg_ref, wu_ref, wd_ref, o_ref, xn_ref):
    @pl.when(pl.program_id(1) == 0)
    def _():
        x = x_ref[...]
        xn_ref[...] = _rms(x, g_ref[...]).astype(BF16)
        o_ref[...] = x

    xn = xn_ref[...]
    h = jax.nn.silu(_dot(xn, wg_ref[...])) * _dot(xn, wu_ref[...])
    o_ref[...] += _dot(h.astype(BF16), wd_ref[...])


def ffn(x, g, w_gate, w_up, w_down, *, tm, tf):
    m, d = x.shape
    f = w_gate.shape[1]
    return pl.pallas_call(
        _ffn_kernel,
        grid=(m // tm, f // tf),
        in_specs=[pl.BlockSpec((tm, d), lambda i, j: (i, 0), pipeline_mode=pl.Buffered(1)),
                  pl.BlockSpec((1, d), lambda i, j: (0, 0)),
                  pl.BlockSpec((d, tf), lambda i, j: (0, j)),
                  pl.BlockSpec((d, tf), lambda i, j: (0, j)),
                  pl.BlockSpec((tf, d), lambda i, j: (j, 0))],
        out_specs=pl.BlockSpec((tm, d), lambda i, j: (i, 0)),
        out_shape=jax.ShapeDtypeStruct((m, d), F32),
        scratch_shapes=[pltpu.VMEM((tm, d), BF16)],
        compiler_params=_params("parallel", "arbitrary"),
        name="ffn",
    )(x, g.reshape(1, d), w_gate, w_up, w_down)


MOE_TILE = 512
DMA_UNROLL = 8
META_I1, META_I2, META_W1, META_W2, META_R1, META_R2 = range(6)


def _router_kernel(x_ref, g_ref, r_ref, xn_ref, meta_ref, cnt_ref, carry_ref):
    @pl.when(pl.program_id(0) == 0)
    def _():
        carry_ref[...] = jnp.zeros_like(carry_ref)

    xn = _rms(x_ref[...], g_ref[...])
    xn_ref[...] = xn
    x_hi, x_mid, _ = _split3(xn)
    r_hi, r_mid, _ = _split3(r_ref[...])
    logits = _dot(x_hi, r_hi) + (_dot(x_mid, r_hi) + _dot(x_hi, r_mid))
    lane = lax.broadcasted_iota(jnp.int32, logits.shape, 1)
    neg = -jnp.inf
    l1 = jnp.where(lane < N_EXPERTS, logits, neg)
    m1 = jnp.max(l1, axis=-1, keepdims=True)
    i1 = jnp.min(jnp.where(l1 == m1, lane, LANES), axis=-1, keepdims=True)
    l2 = jnp.where(lane == i1, neg, l1)
    m2 = jnp.max(l2, axis=-1, keepdims=True)
    i2 = jnp.min(jnp.where(l2 == m2, lane, LANES), axis=-1, keepdims=True)
    e = jnp.exp(m2 - m1)
    den = 1.0 + e
    sel = jnp.where(lane == i1, 1.0, jnp.where(lane == i2, 1.0, 0.0))
    tm = sel.shape[0]
    row = lax.broadcasted_iota(jnp.int32, (tm, tm), 0)
    col = lax.broadcasted_iota(jnp.int32, (tm, tm), 1)
    incl = _dot(jnp.where(col <= row, 1.0, 0.0).astype(BF16), sel.astype(BF16))
    excl = incl - sel + carry_ref[0:1, :]
    rank1 = jnp.sum(jnp.where(lane == i1, excl, 0.0), axis=-1, keepdims=True)
    rank2 = jnp.sum(jnp.where(lane == i2, excl, 0.0), axis=-1, keepdims=True)
    carry_ref[...] = carry_ref[...] + incl[tm - 1:tm, :]
    cnt_ref[...] = carry_ref[...]
    fields = {META_I1: i1.astype(F32), META_I2: i2.astype(F32), META_W1: 1.0 / den, META_W2: e / den,
              META_R1: rank1, META_R2: rank2}
    meta = jnp.zeros(logits.shape, F32)
    for k, val in fields.items():
        meta = jnp.where(lane == k, val, meta)
    meta_ref[...] = meta


def router(x, g, r, *, tm):
    m, d = x.shape
    r_pad = jnp.zeros((d, LANES), F32).at[:, :N_EXPERTS].set(r)
    return pl.pallas_call(
        _router_kernel,
        grid=(m // tm,),
        in_specs=[pl.BlockSpec((tm, d), lambda i: (i, 0)),
                  pl.BlockSpec((1, d), lambda i: (0, 0)),
                  pl.BlockSpec((d, LANES), lambda i: (0, 0))],
        out_specs=[pl.BlockSpec((tm, d), lambda i: (i, 0)),
                   pl.BlockSpec((tm, LANES), lambda i: (i, 0)),
                   pl.BlockSpec((SUBLANES, LANES), lambda i: (0, 0))],
        out_shape=[jax.ShapeDtypeStruct((m, d), F32),
                   jax.ShapeDtypeStruct((m, LANES), F32),
                   jax.ShapeDtypeStruct((SUBLANES, LANES), F32)],
        scratch_shapes=[pltpu.VMEM((SUBLANES, LANES), F32)],
        compiler_params=_params("arbitrary"),
        name="router",
    )(x, g.reshape(1, d), r_pad)


def _dispatch_kernel(p1_ref, p2_ref, xn_ref, zeros_ref, xs_ref, sem, *, td):
    del zeros_ref
    base = pl.program_id(0) * td

    def copies(r):
        src = xn_ref.at[pl.ds(r, 1), :]
        return [pltpu.make_async_copy(src, xs_ref.at[pl.ds(p_ref[base + r], 1), :], sem)
                for p_ref in (p1_ref, p2_ref)]

    def issue(r, carry):
        for c in copies(r):
            c.start()
        return carry

    def drain(r, carry):
        for c in copies(r):
            c.wait()
        return carry

    lax.fori_loop(0, td, issue, 0, unroll=DMA_UNROLL)
    lax.fori_loop(0, td, drain, 0, unroll=DMA_UNROLL)


def moe_dispatch(xn, pos1, pos2, xs, *, td):
    m, d = xn.shape
    rows = xs.shape[0]
    return pl.pallas_call(
        functools.partial(_dispatch_kernel, td=td),
        grid_spec=pltpu.PrefetchScalarGridSpec(
            num_scalar_prefetch=2,
            grid=(m // td,),
            in_specs=[pl.BlockSpec((td, d), lambda i, p1, p2: (i, 0)),
                      pl.BlockSpec(memory_space=pl.ANY)],
            out_specs=pl.BlockSpec(memory_space=pl.ANY),
            scratch_shapes=[pltpu.SemaphoreType.DMA(())]),
        out_shape=jax.ShapeDtypeStruct((rows, d), F32),
        input_output_aliases={3: 0},
        compiler_params=_params("arbitrary"),
        name="moe_dispatch",
    )(pos1, pos2, xn, xs)


def _expert_kernel(te_ref, tv_ref, xs_ref, wg_ref, wu_ref, wd_ref, ys_ref, xb_ref):
    del te_ref
    f = pl.program_id(1)
    valid = tv_ref[pl.program_id(0)]

    @pl.when(valid > 0)
    def _():
        @pl.when(f == 0)
        def _():
            xb_ref[...] = xs_ref[...].astype(BF16)

        xb = xb_ref[...]
        h = jax.nn.silu(_dot(xb, wg_ref[0])) * _dot(xb, wu_ref[0])
        y = _dot(h.astype(BF16), wd_ref[0])

        @pl.when(f == 0)
        def _():
            ys_ref[...] = y

        @pl.when(f > 0)
        def _():
            ys_ref[...] += y

    @pl.when((valid == 0) & (f == 0))
    def _():
        ys_ref[...] = jnp.zeros_like(ys_ref)


def moe_experts(xs, tile_expert, tile_valid, w_gate, w_up, w_down, *, tile, tf):
    rows, d = xs.shape
    _, _, f = w_gate.shape
    nf = f // tf
    fidx = lambda t, j, tv: jnp.where(tv[t] > 0, j, nf - 1)
    return pl.pallas_call(
        _expert_kernel,
        grid_spec=pltpu.PrefetchScalarGridSpec(
            num_scalar_prefetch=2,
            grid=(rows // tile, nf),
            in_specs=[pl.BlockSpec((tile, d), lambda t, j, te, tv: (t, 0)),
                      pl.BlockSpec((1, d, tf), lambda t, j, te, tv: (te[t], 0, fidx(t, j, tv))),
                      pl.BlockSpec((1, d, tf), lambda t, j, te, tv: (te[t], 0, fidx(t, j, tv))),
                      pl.BlockSpec((1, tf, d), lambda t, j, te, tv: (te[t], fidx(t, j, tv), 0))],
            out_specs=pl.BlockSpec((tile, d), lambda t, j, te, tv: (t, 0)),
            scratch_shapes=[pltpu.VMEM((tile, d), BF16)]),
        out_shape=jax.ShapeDtypeStruct((rows, d), F32),
        compiler_params=_params("arbitrary", "arbitrary"),
        name="moe_experts",
    )(tile_expert, tile_valid, xs, w_gate, w_up, w_down)


def _combine_kernel(p1_ref, p2_ref, x_ref, meta_ref, ys_ref, o_ref, buf_ref, sem, *, tc):
    i = pl.program_id(0)
    slot = i % 2

    def copies(step, sl, r):
        tok = step * tc + r
        return [pltpu.make_async_copy(ys_ref.at[pl.ds(p_ref[tok], 1), :],
                                      buf_ref.at[sl, k, pl.ds(r, 1), :], sem.at[sl])
                for k, p_ref in enumerate((p1_ref, p2_ref))]

    def issue(step, sl):
        def body(r, carry):
            for c in copies(step, sl, r):
                c.start()
            return carry
        lax.fori_loop(0, tc, body, 0, unroll=DMA_UNROLL)

    @pl.when(i == 0)
    def _():
        issue(0, 0)

    @pl.when(i + 1 < pl.num_programs(0))
    def _():
        issue(i + 1, 1 - slot)

    def drain(r, carry):
        for c in copies(i, slot, r):
            c.wait()
        return carry
    lax.fori_loop(0, tc, drain, 0, unroll=DMA_UNROLL)

    meta = meta_ref[...]
    w1 = meta[:, META_W1:META_W1 + 1]
    w2 = meta[:, META_W2:META_W2 + 1]
    o_ref[...] = x_ref[...] + w1 * buf_ref[slot, 0] + w2 * buf_ref[slot, 1]


def moe_combine(x, meta, ys, pos1, pos2, *, tc):
    m, d = x.shape
    return pl.pallas_call(
        functools.partial(_combine_kernel, tc=tc),
        grid_spec=pltpu.PrefetchScalarGridSpec(
            num_scalar_prefetch=2,
            grid=(m // tc,),
            in_specs=[pl.BlockSpec((tc, d), lambda i, p1, p2: (i, 0)),
                      pl.BlockSpec((tc, LANES), lambda i, p1, p2: (i, 0)),
                      pl.BlockSpec(memory_space=pl.ANY)],
            out_specs=pl.BlockSpec((tc, d), lambda i, p1, p2: (i, 0)),
            scratch_shapes=[pltpu.VMEM((2, 2, tc, d), F32), pltpu.SemaphoreType.DMA((2,))]),
        out_shape=jax.ShapeDtypeStruct((m, d), F32),
        compiler_params=_params("arbitrary"),
        name="moe_combine",
    )(pos1, pos2, x, meta, ys)


def moe_sparse(xs_in, g, r, w_gate, w_up, w_down):
    d = xs_in[0].shape[1]
    m_total = sum(x.shape[0] for x in xs_in)
    tile = min(m_total, MOE_TILE)
    routed = [router(x, g, r, tm=min(x.shape[0], 512)) for x in xs_in]
    cnts = [counts[0, :N_EXPERTS].astype(jnp.int32) for _, _, counts in routed]
    cnt = sum(cnts)
    tiles_e = (cnt + tile - 1) // tile
    tile_end = jnp.cumsum(tiles_e)
    tile_start = tile_end - tiles_e
    n_tiles = (2 * m_total) // tile + N_EXPERTS
    t_all = jnp.arange(n_tiles, dtype=jnp.int32)
    t = jnp.minimum(t_all, tile_end[-1] - 1)
    tile_expert = jnp.sum(t[:, None] >= tile_end[None, :], axis=1).astype(jnp.int32)
    tile_valid = jnp.clip(cnt[tile_expert] - (t - tile_start[tile_expert]) * tile, 0, tile)
    tile_valid = jnp.where(t_all < tile_end[-1], tile_valid, 0).astype(jnp.int32)
    buf = jnp.zeros((n_tiles * tile, d), F32)
    base = tile_start * tile
    pos = []
    for (xn, meta, _), c in zip(routed, cnts):
        p1 = base[meta[:, META_I1].astype(jnp.int32)] + meta[:, META_R1].astype(jnp.int32)
        p2 = base[meta[:, META_I2].astype(jnp.int32)] + meta[:, META_R2].astype(jnp.int32)
        pos.append((p1, p2))
        buf = moe_dispatch(xn, p1, p2, buf, td=min(xn.shape[0], 256))
        base = base + c
    ys = moe_experts(buf, tile_expert, tile_valid, w_gate, w_up, w_down, tile=tile, tf=512)
    return [moe_combine(x, meta, ys, p1, p2, tc=min(x.shape[0], 256))
            for x, (_, meta, _), (p1, p2) in zip(xs_in, routed, pos)]


def _gates_kernel(ba_ref, alog_ref, dtb_ref, beta_ref, eg_ref, gcum_ref, *, tg):
    ba = ba_ref[...]
    b = ba[:, :DN_V_HEADS]
    a = ba[:, DN_V_HEADS:2 * DN_V_HEADS]
    beta_ref[...] = jax.nn.sigmoid(b)
    z = a + dtb_ref[...]
    softplus = jnp.maximum(z, 0.0) + jnp.log1p(jnp.exp(-jnp.abs(z)))
    g = -jnp.exp(alog_ref[...]) * softplus
    eg_ref[...] = jnp.exp(g)
    r = lax.broadcasted_iota(jnp.int32, (tg, tg), 0)
    c = lax.broadcasted_iota(jnp.int32, (tg, tg), 1)
    tri = jnp.where((c <= r) & (r // DN_CHUNK == c // DN_CHUNK), 1.0, 0.0).astype(BF16)
    hi, mid, lo = _split3(g)
    gcum_ref[...] = _dot(tri, hi) + (_dot(tri, mid) + _dot(tri, lo))


def dn_gates(ba, a_log, dt_bias, *, tg):
    m = ba.shape[0]
    out = jax.ShapeDtypeStruct((m, DN_V_HEADS), F32)
    spec = pl.BlockSpec((tg, DN_V_HEADS), lambda i: (i, 0))
    return pl.pallas_call(
        functools.partial(_gates_kernel, tg=tg),
        grid=(m // tg,),
        in_specs=[pl.BlockSpec((tg, LANES), lambda i: (i, 0)),
                  pl.BlockSpec((1, DN_V_HEADS), lambda i: (0, 0)),
                  pl.BlockSpec((1, DN_V_HEADS), lambda i: (0, 0))],
        out_specs=[spec, spec, spec],
        out_shape=[out, out, out],
        compiler_params=_params("parallel"),
        name="dn_gates",
    )(ba, a_log.reshape(1, DN_V_HEADS), dt_bias.reshape(1, DN_V_HEADS))


def _l2norm_heads(c, scale):
    parts = []
    for h in range(c.shape[1] // DN_K_DIM):
        t = c[:, h * DN_K_DIM:(h + 1) * DN_K_DIM]
        t = t * lax.rsqrt(jnp.sum(t * t, axis=-1, keepdims=True) + EPS)
        parts.append(t * scale if scale != 1.0 else t)
    return jnp.concatenate(parts, axis=1)


DN_GROUP = 8
DN_LBLOCK = 512
DN_BATCH = 8
DN_HBATCH = 4
DN_CONV_ROWS = 128


def _dn_chunk_kernel(q_ref, k_ref, v_ref, qh_ref, kh_ref, vh_ref, wq_ref, wk_ref, wv_ref,
                     z_ref, gc_ref, beta_ref, gt_ref, onorm_ref,
                     o_ref, s_out_ref, s_ref, xs_ref, add_ref, oacc_ref, egl_ref, qc_ref, kc_ref, vc_ref,
                     xq_ref, xk_ref, xv_ref, *, nc):
    hg = pl.program_id(1)
    lb = pl.program_id(2)
    cz = DN_CHUNK

    def conv(x_ref, halo_ref, w_ref, xcat_ref, out_ref, post):
        n = x_ref.shape[1]
        xcat_ref[0:SUBLANES] = jnp.where(lb > 0, halo_ref[0], 0.0)
        xcat_ref[SUBLANES:SUBLANES + n] = x_ref[0]
        first = SUBLANES - (DN_CONV_K - 1)
        for r0 in range(0, n, DN_CONV_ROWS):
            acc = xcat_ref[first + r0:first + r0 + DN_CONV_ROWS] * w_ref[0:1, :]
            for i in range(1, DN_CONV_K):
                acc = acc + xcat_ref[first + r0 + i:first + r0 + i + DN_CONV_ROWS] * w_ref[i:i + 1, :]
            out_ref[r0:r0 + DN_CONV_ROWS] = post(jax.nn.silu(acc))

    conv(q_ref, qh_ref, wq_ref, xq_ref, qc_ref, lambda c: _l2norm_heads(c, DN_K_DIM ** -0.5))
    conv(k_ref, kh_ref, wk_ref, xk_ref, kc_ref, lambda c: _l2norm_heads(c, 1.0))
    conv(v_ref, vh_ref, wv_ref, xv_ref, vc_ref, lambda c: c)

    @pl.when(lb == 0)
    def _():
        s_ref[...] = jnp.zeros_like(s_ref)

    gc_all = gc_ref[0]
    beta_all = beta_ref[0]
    lane = lax.broadcasted_iota(jnp.int32, gc_all.shape, 1)
    ri = lax.broadcasted_iota(jnp.int32, (cz, cz), 0)
    ci = lax.broadcasted_iota(jnp.int32, (cz, cz), 1)
    incl = (ri >= ci)[None]
    strict = (ri > ci)[None]

    rep = DN_V_HEADS // DN_K_HEADS
    for g0 in range(0, DN_GROUP, DN_HBATCH):
        gs = list(range(g0, g0 + DN_HBATCH))
        gcol_all, bcol_all, grow_all = {}, {}, {}
        for g in gs:
            head = hg * DN_GROUP + g
            gcol_all[g] = jnp.sum(jnp.where(lane == head, gc_all, 0.0), axis=-1, keepdims=True)
            bcol_all[g] = jnp.sum(jnp.where(lane == head, beta_all, 0.0), axis=-1, keepdims=True)
            grow_all[g] = gt_ref[0, pl.ds(head, 1), :]
        for c0 in range(0, nc, DN_BATCH):
            cb = min(DN_BATCH, nc - c0)
            rows = slice(c0 * cz, (c0 + cb) * cz)
            stack = lambda fn: jnp.concatenate([fn(g) for g in gs], axis=0)
            kcols = lambda g: slice(g // rep * DN_K_DIM, (g // rep + 1) * DN_K_DIM)
            q = stack(lambda g: qc_ref[rows, kcols(g)].reshape(cb, cz, DN_K_DIM))
            k = stack(lambda g: kc_ref[rows, kcols(g)].reshape(cb, cz, DN_K_DIM))
            v = stack(lambda g: vc_ref[rows, g * DN_V_DIM:(g + 1) * DN_V_DIM].reshape(cb, cz, DN_V_DIM))
            gcol = stack(lambda g: gcol_all[g][rows].reshape(cb, cz, 1))
            bcol = stack(lambda g: bcol_all[g][rows].reshape(cb, cz, 1))
            grow = stack(lambda g: jnp.stack([grow_all[g][:, c * cz:(c + 1) * cz]
                                              for c in range(c0, c0 + cb)]))
            decay = jnp.where(incl, jnp.exp(gcol - grow), 0.0)
            kb = k * bcol
            kbf = k.astype(BF16)
            kk = jnp.einsum('cid,cjd->cij', kb.astype(BF16), kbf, preferred_element_type=F32)
            neg_l = jnp.where(strict, -(kk * decay), 0.0)
            n_acc = neg_l
            pw = neg_l
            for _ in range(5):
                pwb = pw.astype(BF16)
                pw = jnp.einsum('cij,cjk->cik', pwb, pwb, preferred_element_type=F32)
                n_acc = n_acc + pw + jnp.einsum('cij,cjk->cik', n_acc.astype(BF16), pw.astype(BF16),
                                                preferred_element_type=F32)
            egc = jnp.exp(gcol)
            rhs = jnp.concatenate([kb * egc, v * bcol], axis=-1)
            bmm = lambda x, y: jnp.einsum('cij,cjd->cid', x, y, preferred_element_type=F32)
            wu = (rhs + bmm(n_acc.astype(BF16), rhs.astype(BF16))).astype(BF16)
            qk = jnp.einsum('cid,cjd->cij', q.astype(BF16), kbf, preferred_element_type=F32) * decay
            a_wu = bmm(qk.astype(BF16), wu)
            glast = gcol[:, cz - 1:cz, :]
            kd = k * jnp.exp(glast - gcol)
            qa = (q * egc - a_wu[..., :DN_K_DIM]).astype(BF16)
            egl = jnp.broadcast_to(jnp.exp(glast), (len(gs) * cb, 1, DN_V_DIM))
            for n, g in enumerate(gs):
                for c in range(cb):
                    kd_wu = _dot(jnp.transpose(kd[n * cb + c]).astype(BF16), wu[n * cb + c])
                    xs_ref[g, c0 + c, 0:DN_K_DIM, :] = (-kd_wu[:, :DN_K_DIM]).astype(BF16)
                    add_ref[g, c0 + c, 0:DN_K_DIM, :] = kd_wu[:, DN_K_DIM:]
                xs_ref[g, c0:c0 + cb, DN_K_DIM:DN_K_DIM + cz, :] = qa[n * cb:(n + 1) * cb]
                add_ref[g, c0:c0 + cb, DN_K_DIM:DN_K_DIM + cz, :] = a_wu[n * cb:(n + 1) * cb, :, DN_K_DIM:]
                egl_ref[g, c0:c0 + cb] = egl[n * cb:(n + 1) * cb]

    def chunk(c, carry):
        for g in range(DN_GROUP):
            s = s_ref[g]
            r = _dot(xs_ref[g, c], s.astype(BF16)) + add_ref[g, c]
            oacc_ref[g, pl.ds(pl.multiple_of(c * cz, cz), cz), :] = r[DN_K_DIM:DN_K_DIM + cz]
            s_ref[g] = s * egl_ref[g, c] + r[0:DN_K_DIM]
        return carry

    lax.fori_loop(0, nc, chunk, 0)

    for g in range(DN_GROUP):
        o = oacc_ref[g]
        z = z_ref[0, :, g * DN_V_DIM:(g + 1) * DN_V_DIM]
        o_ref[0, :, g * DN_V_DIM:(g + 1) * DN_V_DIM] = (
            _rms(o, onorm_ref[...]) * jax.nn.silu(z)).astype(o_ref.dtype)

    @pl.when(lb == pl.num_programs(2) - 1)
    def _():
        s_out_ref[0] = s_ref[...]


def dn_chunked(proj, conv_w, gcum, beta, gcum_t, out_norm):
    b, l, _ = proj.shape
    lbk = min(DN_LBLOCK, l)
    nc = lbk // DN_CHUNK
    gk = DN_GROUP // 2 * DN_K_DIM
    gv = DN_GROUP * DN_V_DIM
    cz = DN_CHUNK
    qcol = lambda h: h
    kcol = lambda h: DN_QK_W // gk + h
    vcol = lambda h: 2 * DN_QK_W // gv + h
    hb = lbk // SUBLANES
    cur = lambda w, col: pl.BlockSpec((1, lbk, w), lambda i, h, t: (i, t, col(h)))
    halo = lambda w, col: pl.BlockSpec((1, SUBLANES, w), lambda i, h, t: (i, jnp.maximum(t * hb - 1, 0), col(h)))
    taps = lambda w, col: pl.BlockSpec((DN_CONV_K, w), lambda i, h, t: (0, col(h)))
    return pl.pallas_call(
        functools.partial(_dn_chunk_kernel, nc=nc),
        grid=(b, DN_V_HEADS // DN_GROUP, l // lbk),
        in_specs=[cur(gk, qcol), cur(gk, kcol), cur(gv, vcol),
                  halo(gk, qcol), halo(gk, kcol), halo(gv, vcol),
                  taps(gk, qcol), taps(gk, kcol), taps(gv, vcol),
                  pl.BlockSpec((1, lbk, gv), lambda i, h, t: (i, t, DN_CONV_DIM // gv + h)),
                  pl.BlockSpec((1, lbk, DN_V_HEADS), lambda i, h, t: (i, t, 0)),
                  pl.BlockSpec((1, lbk, DN_V_HEADS), lambda i, h, t: (i, t, 0)),
                  pl.BlockSpec((1, DN_V_HEADS, lbk), lambda i, h, t: (i, 0, t)),
                  pl.BlockSpec((1, DN_V_DIM), lambda i, h, t: (0, 0))],
        out_specs=[pl.BlockSpec((1, lbk, gv), lambda i, h, t: (i, t, h)),
                   pl.BlockSpec((1, DN_GROUP, DN_K_DIM, DN_V_DIM), lambda i, h, t: (i, h, 0, 0))],
        out_shape=[jax.ShapeDtypeStruct((b, l, DN_V_W), BF16),
                   jax.ShapeDtypeStruct((b, DN_V_HEADS, DN_K_DIM, DN_V_DIM), F32)],
        scratch_shapes=[pltpu.VMEM((DN_GROUP, DN_K_DIM, DN_V_DIM), F32),
                        pltpu.VMEM((DN_GROUP, nc, DN_K_DIM + cz, DN_K_DIM), BF16),
                        pltpu.VMEM((DN_GROUP, nc, DN_K_DIM + cz, DN_V_DIM), F32),
                        pltpu.VMEM((DN_GROUP, lbk, DN_V_DIM), F32),
                        pltpu.VMEM((DN_GROUP, nc, 1, DN_V_DIM), F32),
                        pltpu.VMEM((lbk, gk), F32), pltpu.VMEM((lbk, gk), F32), pltpu.VMEM((lbk, gv), F32),
                        pltpu.VMEM((lbk + SUBLANES, gk), F32), pltpu.VMEM((lbk + SUBLANES, gk), F32),
                        pltpu.VMEM((lbk + SUBLANES, gv), F32)],
        compiler_params=_params("parallel", "parallel", "arbitrary"),
        name="dn_chunked",
    )(proj, proj, proj, proj, proj, proj, conv_w, conv_w, conv_w, proj, gcum, beta, gcum_t,
      out_norm.reshape(1, DN_V_DIM))


DN_ROWS = DN_CONV_DIM // LANES
DN_QROWS = DN_K_HEADS
DN_VROW0 = 2 * DN_K_HEADS


def _dn_sample_kernel(x_ref, buf_ref, w_ref, z_ref, eg_ref, beta_ref, s_ref, onorm_ref,
                      o_ref, s_out_ref, buf_out_ref):
    b = pl.program_id(0)
    x = x_ref[0]
    buf = buf_ref[0]
    conv = buf[0] * w_ref[0]
    for i in range(1, DN_CONV_K - 1):
        conv = conv + buf[i] * w_ref[i]
    conv = conv + x * w_ref[DN_CONV_K - 1]
    buf_out_ref[0, 0:DN_CONV_K - 2] = buf[1:DN_CONV_K - 1]
    buf_out_ref[0, DN_CONV_K - 2] = x
    c = jax.nn.silu(conv)
    qk = c[0:DN_VROW0]
    qk = qk * lax.rsqrt(jnp.sum(qk * qk, axis=-1, keepdims=True) + EPS)
    q_t = jnp.transpose(qk[0:DN_QROWS] * (DN_K_DIM ** -0.5))
    k_t = jnp.transpose(qk[DN_QROWS:DN_VROW0])
    for h in range(DN_V_HEADS):
        kh = h // (DN_V_HEADS // DN_K_HEADS)
        kcol = k_t[:, kh:kh + 1]
        qcol = q_t[:, kh:kh + 1]
        v = c[DN_VROW0 + h:DN_VROW0 + h + 1]
        s = s_ref[0, h] * eg_ref[b, h]
        delta = (v - jnp.sum(kcol * s, axis=0, keepdims=True)) * beta_ref[b, h]
        s = s + kcol * delta
        s_out_ref[0, h] = s
        o = jnp.sum(qcol * s, axis=0, keepdims=True)
        o_ref[0, h:h + 1, :] = (_rms(o, onorm_ref[...]) * jax.nn.silu(z_ref[0, h:h + 1, :])
                                ).astype(o_ref.dtype)


def dn_sample(qkv_rows, conv_buf, conv_w, z_rows, eg, beta, state, out_norm):
    nb = qkv_rows.shape[0]
    smem = pl.BlockSpec(memory_space=pltpu.SMEM)
    return pl.pallas_call(
        _dn_sample_kernel,
        grid=(nb,),
        in_specs=[pl.BlockSpec((1, DN_ROWS, LANES), lambda i: (i, 0, 0)),
                  pl.BlockSpec((1, DN_CONV_K - 1, DN_ROWS, LANES), lambda i: (i, 0, 0, 0)),
                  pl.BlockSpec((DN_CONV_K, DN_ROWS, LANES), lambda i: (0, 0, 0)),
                  pl.BlockSpec((1, DN_V_HEADS, DN_V_DIM), lambda i: (i, 0, 0)),
                  smem, smem,
                  pl.BlockSpec((1, DN_V_HEADS, DN_K_DIM, DN_V_DIM), lambda i: (i, 0, 0, 0)),
                  pl.BlockSpec((1, DN_V_DIM), lambda i: (0, 0))],
        out_specs=[pl.BlockSpec((1, DN_V_HEADS, DN_V_DIM), lambda i: (i, 0, 0)),
                   pl.BlockSpec((1, DN_V_HEADS, DN_K_DIM, DN_V_DIM), lambda i: (i, 0, 0, 0)),
                   pl.BlockSpec((1, DN_CONV_K - 1, DN_ROWS, LANES), lambda i: (i, 0, 0, 0))],
        out_shape=[jax.ShapeDtypeStruct((nb, DN_V_HEADS, DN_V_DIM), BF16),
                   jax.ShapeDtypeStruct(state.shape, F32),
                   jax.ShapeDtypeStruct(conv_buf.shape, F32)],
        compiler_params=_params("arbitrary"),
        name="dn_sample",
    )(qkv_rows, conv_buf, conv_w, z_rows, eg, beta, state, out_norm.reshape(1, DN_V_DIM))


def _tile(m, cap):
    return min(m, cap)


def _even_layer(x, cache, w, *, batch, seq):
    m = x.shape[0]
    tm = _tile(m, 1024)
    qkv = norm_matmul(x, w['norm_mix'], w['w_in_qkv'], tm=tm, tn=512)
    u = norm_matmul(x, w['norm_mix'], w['w_in_u'], tm=tm, tn=512)
    if cache is None:
        attn, new_k, new_v = attn_prompt(qkv, w['q_norm'], w['k_norm'], w['sinks'], batch=batch, seq=seq)
        zeros = jnp.zeros((batch, S5_LANES), F32)
        ssm, h_re, h_im = s5_mix(u.reshape(batch, seq, S5_WIDTH), zeros, zeros, w['s5'],
                                 rows=batch, seq=seq, steps=64)
        ssm = ssm.reshape(m, S5_WIDTH)
    else:
        k_win, v_win, h0_re, h0_im = cache
        attn, new_k, new_v = attn_sample(qkv, k_win.reshape(batch, WINDOW, KV_WIDTH),
                                         v_win.reshape(batch, WINDOW, KV_WIDTH),
                                         w['q_norm'], w['k_norm'], w['sinks'])
        ssm, h_re, h_im = s5_mix(u, h0_re.reshape(batch, S5_LANES), h0_im.reshape(batch, S5_LANES),
                                 w['s5'], rows=batch, seq=1, steps=1)
    x = matmul_residual(x, [attn, ssm], [w['w_out_a'], w['w_out_b']], tm=tm, tn=512)
    x = ffn(x, w['norm_ffn'], w['ffn_gate'], w['ffn_up'], w['ffn_down'], tm=tm, tf=512)
    shp = (batch, WINDOW, N_KV_HEADS, HEAD_DIM)
    st = (batch, S5_GROUPS, S5_STATE)
    return x, new_k.reshape(shp), new_v.reshape(shp), h_re.reshape(st), h_im.reshape(st)


def _moe(xs_in, w):
    return moe_sparse(xs_in, w['norm_ffn'], w['router'], w['exp_gate'], w['exp_up'], w['exp_down'])


def _odd_layer(x, cache, w, *, batch, seq):
    x, s_new, new_buf = _odd_mixer(x, cache, w, batch=batch, seq=seq)
    return _moe([x], w)[0], s_new, new_buf


def _odd_mixer(x, cache, w, *, batch, seq):
    m = x.shape[0]
    tm = _tile(m, 1024)
    proj = norm_matmul(x, w['norm_mix'], w['w_in_main'], tm=tm, tn=1024)
    ba = norm_matmul(x, w['norm_mix'], w['w_in_ba'], tm=tm, tn=LANES)
    beta, eg, gcum = dn_gates(ba, w['a_log'], w['dt_bias'], tg=_tile(m, 512))
    if cache is None:
        proj3 = proj.reshape(batch, seq, DN_CONV_DIM + DN_V_W)
        gcum3 = gcum.reshape(batch, seq, DN_V_HEADS)
        o, s_new = dn_chunked(proj3, w['conv_w'], gcum3, beta.reshape(batch, seq, DN_V_HEADS),
                              jnp.swapaxes(gcum3, 1, 2), w['out_norm'])
        o = o.reshape(m, DN_V_W)
        new_buf = proj3[:, seq - (DN_CONV_K - 1):, :DN_CONV_DIM]
    else:
        s0, conv_buf = cache
        o, s_new, new_buf = dn_sample(
            proj[:, :DN_CONV_DIM].reshape(batch, DN_ROWS, LANES),
            conv_buf.reshape(batch, DN_CONV_K - 1, DN_ROWS, LANES),
            w['conv_w'].reshape(DN_CONV_K, DN_ROWS, LANES),
            proj[:, DN_CONV_DIM:].reshape(batch, DN_V_HEADS, DN_V_DIM),
            eg, beta, s0, w['out_norm'])
        o = o.reshape(m, DN_V_W)
        new_buf = new_buf.reshape(batch, DN_CONV_K - 1, DN_CONV_DIM)
    x = matmul_residual(x, [o], [w['w_out']], tm=tm, tn=512)
    return x, s_new, new_buf


def kernel(x_prompt, x_sample, cache_win_k, cache_win_v, state_s5_re, state_s5_im, state_dn, state_dn_conv,
           e_norm_mix, e_w_in, e_q_norm, e_k_norm, e_sinks,
           e_s5_a_re, e_s5_a_im, e_s5_log_dt, e_s5_b_re, e_s5_b_im, e_s5_c_re, e_s5_c_im, e_s5_d, e_s5_w_glu,
           e_w_out, e_norm_ffn, e_ffn_w_gate, e_ffn_w_up, e_ffn_w_down,
           o_norm_mix, o_w_in, o_conv_w, o_a_log, o_dt_bias, o_out_norm, o_w_out, o_norm_ffn,
           o_router, o_exp_w_gate, o_exp_w_up, o_exp_w_down):
    bp, lp, d = x_prompt.shape
    bs, ls, _ = x_sample.shape
    assert ls == 1, "the sample group advances one token per step"
    hp = x_prompt.reshape(bp * lp, d)
    hs = x_sample.reshape(bs * ls, d)
    qkv_w = ATTN_WIDTH + 2 * KV_WIDTH
    main_w = DN_CONV_DIM + DN_V_W

    j = 0
    we = dict(
        norm_mix=e_norm_mix[j], q_norm=e_q_norm[j], k_norm=e_k_norm[j], sinks=e_sinks[j],
        w_in_qkv=e_w_in[j, :, :qkv_w].astype(BF16), w_in_u=e_w_in[j, :, qkv_w:].astype(BF16),
        s5=s5_constants(e_s5_a_re[j], e_s5_a_im[j], e_s5_log_dt[j], e_s5_b_re[j], e_s5_b_im[j],
                        e_s5_c_re[j], e_s5_c_im[j], e_s5_d[j], e_s5_w_glu[j]),
        w_out_a=e_w_out[j, :ATTN_WIDTH].astype(BF16), w_out_b=e_w_out[j, ATTN_WIDTH:].astype(BF16),
        norm_ffn=e_norm_ffn[j], ffn_gate=e_ffn_w_gate[j].astype(BF16),
        ffn_up=e_ffn_w_up[j].astype(BF16), ffn_down=e_ffn_w_down[j].astype(BF16))
    ba_pad = jnp.zeros((d, LANES), F32).at[:, :2 * DN_V_HEADS].set(o_w_in[j, :, main_w:])
    wo = dict(
        norm_mix=o_norm_mix[j], w_in_main=o_w_in[j, :, :main_w].astype(BF16), w_in_ba=ba_pad.astype(BF16),
        conv_w=o_conv_w[j], a_log=o_a_log[j], dt_bias=o_dt_bias[j], out_norm=o_out_norm[j],
        w_out=o_w_out[j].astype(BF16), norm_ffn=o_norm_ffn[j], router=o_router[j],
        exp_gate=o_exp_w_gate[j].astype(BF16), exp_up=o_exp_w_up[j].astype(BF16),
        exp_down=o_exp_w_down[j].astype(BF16))

    hp, kp, vp, rp, ip = _even_layer(hp, None, we, batch=bp, seq=lp)
    hs, ks, vs, rs, is_ = _even_layer(
        hs, (cache_win_k[j], cache_win_v[j], state_s5_re[j], state_s5_im[j]), we, batch=bs, seq=1)
    hp, sp, cp = _odd_mixer(hp, None, wo, batch=bp, seq=lp)
    hs, ss, cs = _odd_mixer(hs, (state_dn[j], state_dn_conv[j]), wo, batch=bs, seq=1)
    hp, hs = _moe([hp, hs], wo)

    one = lambda t: t[None]
    return (hp.reshape(bp, lp, d), hs.reshape(bs, ls, d),
            one(kp), one(vp), one(rp), one(ip), one(sp), one(cp),
            one(ks), one(vs), one(rs), one(is_), one(ss), one(cs))
```

```python
import functools

import jax
import jax.numpy as jnp
from jax import lax
from jax.experimental import pallas as pl
from jax.experimental.pallas import tpu as pltpu

F32 = jnp.float32
BF16 = jnp.bfloat16

D_MODEL = 2048
N_HEADS = 16
N_KV_HEADS = 4
HEAD_DIM = 64
Q_PER_KV = N_HEADS // N_KV_HEADS
WINDOW = 128
ATTN_WIDTH = N_HEADS * HEAD_DIM
KV_WIDTH = N_KV_HEADS * HEAD_DIM
ATTN_SCALE = HEAD_DIM ** -0.5
S5_WIDTH = D_MODEL // 2
S5_GROUP_CH = 16
S5_GROUPS = S5_WIDTH // S5_GROUP_CH
S5_STATE = 64
S5_LANES = S5_GROUPS * S5_STATE
EVEN_IN = ATTN_WIDTH + 2 * KV_WIDTH + S5_WIDTH
DN_K_HEADS = 16
DN_V_HEADS = 32
DN_K_DIM = 128
DN_V_DIM = 128
DN_CONV_K = 4
DN_CHUNK = 64
DN_QK_W = DN_K_HEADS * DN_K_DIM
DN_V_W = DN_V_HEADS * DN_V_DIM
DN_CONV_DIM = 2 * DN_QK_W + DN_V_W
D_FF = 5632
N_EXPERTS = 8
EPS = 1e-6
NEG_INF = -1e30

LANES = 128
SUBLANES = 8
VMEM_LIMIT = 56 * 1024 * 1024


def _params(*sem):
    return pltpu.CompilerParams(dimension_semantics=sem, vmem_limit_bytes=VMEM_LIMIT)


def _rms(x, g):
    return x * lax.rsqrt(jnp.mean(x * x, axis=-1, keepdims=True) + EPS) * g


def _dot(a, b):
    return jnp.dot(a, b, preferred_element_type=F32)


def _dot_nt(a, b):
    return lax.dot_general(a, b, (((1,), (1,)), ((), ())), preferred_element_type=F32)


def _split3(x):
    hi = x.astype(BF16)
    r = x - hi.astype(F32)
    mid = r.astype(BF16)
    lo = (r - mid.astype(F32)).astype(BF16)
    return hi, mid, lo


def _norm_matmul_kernel(x_ref, g_ref, w_ref, o_ref, xn_ref):
    @pl.when(pl.program_id(1) == 0)
    def _():
        xn_ref[...] = _rms(x_ref[...], g_ref[...]).astype(BF16)

    o_ref[...] = _dot(xn_ref[...], w_ref[...]).astype(o_ref.dtype)


def norm_matmul(x, g, w, *, tm, tn, out_dtype=F32):
    m, d = x.shape
    n = w.shape[1]
    return pl.pallas_call(
        _norm_matmul_kernel,
        grid=(m // tm, n // tn),
        in_specs=[pl.BlockSpec((tm, d), lambda i, j: (i, 0)),
                  pl.BlockSpec((1, d), lambda i, j: (0, 0)),
                  pl.BlockSpec((d, tn), lambda i, j: (0, j))],
        out_specs=pl.BlockSpec((tm, tn), lambda i, j: (i, j)),
        out_shape=jax.ShapeDtypeStruct((m, n), out_dtype),
        scratch_shapes=[pltpu.VMEM((tm, d), BF16)],
        compiler_params=_params("parallel", "arbitrary"),
        name="norm_matmul",
    )(x, g.reshape(1, d), w)


def _matmul_residual_kernel(*refs, n_pairs):
    x_ref = refs[0]
    o_ref = refs[-1]
    acc = x_ref[...]
    for a_ref, w_ref in zip(refs[1:1 + n_pairs], refs[1 + n_pairs:1 + 2 * n_pairs]):
        acc = acc + _dot(a_ref[...], w_ref[...])
    o_ref[...] = acc


def matmul_residual(x, a_list, w_list, *, tm, tn):
    m, n = x.shape
    in_specs = [pl.BlockSpec((tm, tn), lambda i, j: (i, j))]
    in_specs += [pl.BlockSpec((tm, a.shape[1]), lambda i, j: (i, 0)) for a in a_list]
    in_specs += [pl.BlockSpec((w.shape[0], tn), lambda i, j: (0, j)) for w in w_list]
    return pl.pallas_call(
        functools.partial(_matmul_residual_kernel, n_pairs=len(a_list)),
        grid=(m // tm, n // tn),
        in_specs=in_specs,
        out_specs=pl.BlockSpec((tm, tn), lambda i, j: (i, j)),
        out_shape=jax.ShapeDtypeStruct((m, n), F32),
        compiler_params=_params("parallel", "arbitrary"),
        name="matmul_residual",
    )(x, *a_list, *w_list)


HEADS_PER_TILE = LANES // HEAD_DIM


def _head_rms(x, seg, g):
    hi, mid, lo = _split3(x * x)
    parts = []
    for a in range(x.shape[1] // LANES):
        sl = slice(a * LANES, (a + 1) * LANES)
        parts.append(_dot(hi[:, sl], seg) + (_dot(mid[:, sl], seg) + _dot(lo[:, sl], seg)))
    ms = jnp.concatenate(parts, axis=1) * (1.0 / HEAD_DIM)
    return x * lax.rsqrt(ms + EPS) * g


def _attn_prompt_kernel(q_ref, kc_ref, kp_ref, vc_ref, vp_ref, qn_ref, kn_ref, sink_ref,
                        o_ref, wk_ref, wv_ref, s_ref, p_ref):
    has_prev = pl.program_id(1) > 0
    r = lax.broadcasted_iota(jnp.int32, (LANES, LANES), 0)
    c = lax.broadcasted_iota(jnp.int32, (LANES, LANES), 1)
    seg = jnp.where(r // HEAD_DIM == c // HEAD_DIM, 1.0, 0.0).astype(BF16)
    low = lax.broadcasted_iota(jnp.int32, (1, LANES), 1) < HEAD_DIM

    qn = _head_rms(q_ref[...], seg, qn_ref[...]) * ATTN_SCALE
    kc = _head_rms(kc_ref[...], seg, kn_ref[...])
    kp = _head_rms(kp_ref[...], seg, kn_ref[...])
    vc = vc_ref[...]
    wk_ref[0] = kc
    wv_ref[0] = vc
    kband = jnp.concatenate([kp, kc], axis=0)
    vband = jnp.concatenate([vp_ref[...], vc], axis=0)

    v_halves = []
    for kvh in range(N_KV_HEADS):
        tl = slice(kvh // HEADS_PER_TILE * LANES, (kvh // HEADS_PER_TILE + 1) * LANES)

        def both_halves(t):
            rolled = pltpu.roll(t, HEAD_DIM, axis=1)
            return jnp.where(low, t, rolled) if kvh % HEADS_PER_TILE == 0 else jnp.where(low, rolled, t)

        kdup = both_halves(kband[:, tl]).astype(BF16)
        vdup = both_halves(vband[:, tl])
        v_halves.append((jnp.where(low, vdup, 0.0).astype(BF16), jnp.where(low, 0.0, vdup).astype(BF16)))
        for pr in range(Q_PER_KV // HEADS_PER_TILE):
            a = kvh * Q_PER_KV // HEADS_PER_TILE + pr
            qt = qn[:, a * LANES:(a + 1) * LANES]
            s_ref[HEADS_PER_TILE * a] = _dot_nt(jnp.where(low, qt, 0.0).astype(BF16), kdup)
            s_ref[HEADS_PER_TILE * a + 1] = _dot_nt(jnp.where(low, 0.0, qt).astype(BF16), kdup)

    row = lax.broadcasted_iota(jnp.int32, (WINDOW, 2 * WINDOW), 0)
    col = lax.broadcasted_iota(jnp.int32, (WINDOW, 2 * WINDOW), 1)
    rel = (WINDOW + row) - col
    mask = (rel >= 0) & (rel <= WINDOW) & ((col >= WINDOW) | has_prev)
    for h in range(N_HEADS):
        s = jnp.where(mask, s_ref[h], NEG_INF)
        sk = sink_ref[h:h + 1, :]
        m = jnp.maximum(jnp.max(s, axis=-1, keepdims=True), sk)
        p = jnp.exp(s - m)
        p = p / (jnp.sum(p, axis=-1, keepdims=True) + jnp.exp(sk - m))
        p_ref[h] = p.astype(BF16)

    for a in range(N_HEADS // HEADS_PER_TILE):
        v_lo, v_hi = v_halves[a * HEADS_PER_TILE // Q_PER_KV]
        o = _dot(p_ref[HEADS_PER_TILE * a], v_lo) + _dot(p_ref[HEADS_PER_TILE * a + 1], v_hi)
        o_ref[:, a * LANES:(a + 1) * LANES] = o.astype(o_ref.dtype)


def attn_prompt(qkv, q_norm, k_norm, sinks, *, batch, seq, col0):
    nb = seq // WINDOW
    qcol = col0 // ATTN_WIDTH
    kcol = (col0 + ATTN_WIDTH) // KV_WIDTH
    cur = lambda c: (lambda b, n: (b * nb + n, c))
    prev = lambda c: (lambda b, n: (jnp.maximum(b * nb + n - 1, 0), c))
    return pl.pallas_call(
        _attn_prompt_kernel,
        grid=(batch, nb),
        in_specs=[pl.BlockSpec((WINDOW, ATTN_WIDTH), cur(qcol)),
                  pl.BlockSpec((WINDOW, KV_WIDTH), cur(kcol)),
                  pl.BlockSpec((WINDOW, KV_WIDTH), prev(kcol)),
                  pl.BlockSpec((WINDOW, KV_WIDTH), cur(kcol + 1)),
                  pl.BlockSpec((WINDOW, KV_WIDTH), prev(kcol + 1)),
                  pl.BlockSpec((1, ATTN_WIDTH), lambda b, n: (0, 0)),
                  pl.BlockSpec((1, KV_WIDTH), lambda b, n: (0, 0)),
                  pl.BlockSpec((N_HEADS, 1), lambda b, n: (0, 0))],
        out_specs=[pl.BlockSpec((WINDOW, ATTN_WIDTH), lambda b, n: (b * nb + n, 0)),
                   pl.BlockSpec((1, WINDOW, KV_WIDTH), lambda b, n: (b, 0, 0)),
                   pl.BlockSpec((1, WINDOW, KV_WIDTH), lambda b, n: (b, 0, 0))],
        out_shape=[jax.ShapeDtypeStruct((batch * seq, ATTN_WIDTH), BF16),
                   jax.ShapeDtypeStruct((batch, WINDOW, KV_WIDTH), F32),
                   jax.ShapeDtypeStruct((batch, WINDOW, KV_WIDTH), F32)],
        scratch_shapes=[pltpu.VMEM((N_HEADS, WINDOW, 2 * WINDOW), F32),
                        pltpu.VMEM((N_HEADS, WINDOW, 2 * WINDOW), BF16)],
        compiler_params=_params("parallel", "arbitrary"),
        name="attn_prompt",
    )(qkv, qkv, qkv, qkv, qkv, jnp.tile(q_norm, N_HEADS).reshape(1, ATTN_WIDTH),
      jnp.tile(k_norm, N_KV_HEADS).reshape(1, KV_WIDTH), sinks.reshape(N_HEADS, 1))


def _attn_sample_kernel(qkv_ref, ck_ref, cv_ref, qn_ref, kn_ref, sink_ref,
                        o_ref, wk_ref, wv_ref, *, col0):
    qg = qn_ref[...]
    kg = kn_ref[...]
    kv0 = col0 + ATTN_WIDTH
    v0 = col0 + ATTN_WIDTH + KV_WIDTH
    wk_ref[:, 0:WINDOW - 1, :] = ck_ref[:, 1:WINDOW, :]
    wv_ref[:, 0:WINDOW - 1, :] = cv_ref[:, 1:WINDOW, :]
    wv_ref[:, WINDOW - 1:WINDOW, :] = qkv_ref[:, v0:v0 + KV_WIDTH][:, None, :]
    for kvh in range(N_KV_HEADS):
        sl = slice(kvh * HEAD_DIM, (kvh + 1) * HEAD_DIM)
        kn = _rms(qkv_ref[:, kv0 + kvh * HEAD_DIM:kv0 + (kvh + 1) * HEAD_DIM], kg)[:, None, :]
        vn = qkv_ref[:, v0 + kvh * HEAD_DIM:v0 + (kvh + 1) * HEAD_DIM][:, None, :]
        wk_ref[:, WINDOW - 1:WINDOW, sl] = kn
        q4 = jnp.concatenate(
            [_rms(qkv_ref[:, col0 + (kvh * Q_PER_KV + g) * HEAD_DIM:col0 + (kvh * Q_PER_KV + g + 1) * HEAD_DIM],
                  qg)[:, None, :]
             for g in range(Q_PER_KV)], axis=1)
        kwin = ck_ref[:, :, sl].astype(BF16)
        vwin = cv_ref[:, :, sl].astype(BF16)
        s_c = jnp.einsum('bqd,bkd->bqk', q4.astype(BF16), kwin, preferred_element_type=F32) * ATTN_SCALE
        s_n = jnp.sum(q4 * kn, axis=-1, keepdims=True) * ATTN_SCALE
        sk = sink_ref[kvh * Q_PER_KV:(kvh + 1) * Q_PER_KV, :][None]
        m = jnp.maximum(jnp.maximum(jnp.max(s_c, axis=-1, keepdims=True), s_n), sk)
        p_c = jnp.exp(s_c - m)
        p_n = jnp.exp(s_n - m)
        den = jnp.sum(p_c, axis=-1, keepdims=True) + p_n + jnp.exp(sk - m)
        o4 = jnp.einsum('bqk,bkd->bqd', (p_c / den).astype(BF16), vwin,
                        preferred_element_type=F32) + (p_n / den) * vn
        for g in range(Q_PER_KV):
            h = kvh * Q_PER_KV + g
            o_ref[:, h * HEAD_DIM:(h + 1) * HEAD_DIM] = o4[:, g, :].astype(o_ref.dtype)


def attn_sample(qkv, cache_k, cache_v, q_norm, k_norm, sinks, *, col0, bt=8):
    nb = qkv.shape[0]
    width = qkv.shape[1]
    win = pl.BlockSpec((bt, WINDOW, KV_WIDTH), lambda i: (i, 0, 0))
    return pl.pallas_call(
        functools.partial(_attn_sample_kernel, col0=col0),
        grid=(nb // bt,),
        in_specs=[pl.BlockSpec((bt, width), lambda i: (i, 0)), win, win,
                  pl.BlockSpec((1, HEAD_DIM), lambda i: (0, 0)),
                  pl.BlockSpec((1, HEAD_DIM), lambda i: (0, 0)),
                  pl.BlockSpec((N_HEADS, 1), lambda i: (0, 0))],
        out_specs=[pl.BlockSpec((bt, ATTN_WIDTH), lambda i: (i, 0)), win, win],
        out_shape=[jax.ShapeDtypeStruct((nb, ATTN_WIDTH), BF16),
                   jax.ShapeDtypeStruct((nb, WINDOW, KV_WIDTH), F32),
                   jax.ShapeDtypeStruct((nb, WINDOW, KV_WIDTH), F32)],
        compiler_params=_params("parallel"),
        name="attn_sample",
    )(qkv, cache_k, cache_v, q_norm.reshape(1, HEAD_DIM), k_norm.reshape(1, HEAD_DIM),
      sinks.reshape(N_HEADS, 1))


S5_KCH = S5_WIDTH // LANES
S5_CHUNK_STATES = S5_LANES // S5_KCH
S5_SCAN_TILES = 4


def _s5_kernel(u_ref, h0re_ref, h0im_ref, are_ref, aim_ref, wbu_ref, wc_ref, d_ref, wglu_ref,
               y_ref, hre_ref, him_ref, hs_ref, st_ref, *, rows, steps):
    rt = rows * steps
    u = u_ref[...].reshape(rt, S5_WIDTH)

    @pl.when(pl.program_id(0) == 0)
    def _():
        st_ref[:, :S5_LANES] = h0re_ref[...]
        st_ref[:, S5_LANES:] = h0im_ref[...]

    if steps > 1:
        assert rows == SUBLANES
        r = lax.broadcasted_iota(jnp.int32, (rt, rt), 0)
        c = lax.broadcasted_iota(jnp.int32, (rt, rt), 1)
        to_time_major = jnp.where(c == (r % rows) * steps + r // rows, 1.0, 0.0).astype(BF16)
        to_seq_major = jnp.where(c == (r % steps) * rows + r // steps, 1.0, 0.0).astype(BF16)
        hi, mid, lo = _split3(u)
        u_hi = _dot(to_time_major, hi)
        u = u_hi + (_dot(to_time_major, mid) + _dot(to_time_major, lo))
        ub = u_hi.astype(BF16)
    else:
        ub = u.astype(BF16)
    nre = S5_LANES // LANES
    tpk = S5_CHUNK_STATES // LANES
    for k in range(S5_KCH):
        r = _dot(ub[:, k * LANES:(k + 1) * LANES], wbu_ref[k])
        for a in range(tpk):
            hs_ref[k * tpk + a] = r[:, a * LANES:(a + 1) * LANES]
            hs_ref[nre + k * tpk + a] = r[:, (tpk + a) * LANES:(tpk + a + 1) * LANES]

    nt = S5_SCAN_TILES
    for j in range(nre // nt):
        tiles = range(j * nt, (j + 1) * nt)
        ar = [jnp.broadcast_to(are_ref[:, a * LANES:(a + 1) * LANES], (SUBLANES, LANES)) for a in tiles]
        ai = [jnp.broadcast_to(aim_ref[:, a * LANES:(a + 1) * LANES], (SUBLANES, LANES)) for a in tiles]

        def advance(r8, hr, hi):
            nr, ni = [], []
            for n, a in enumerate(tiles):
                nr.append(ar[n] * hr[n] - ai[n] * hi[n] + hs_ref[a, r8, :])
                ni.append(ar[n] * hi[n] + ai[n] * hr[n] + hs_ref[nre + a, r8, :])
                hs_ref[a, r8, :] = nr[n]
                hs_ref[nre + a, r8, :] = ni[n]
            return nr, ni

        if steps == 1:
            def group(rg, carry):
                r8 = pl.ds(pl.multiple_of(rg * SUBLANES, SUBLANES), SUBLANES)
                hr = [st_ref[r8, a * LANES:(a + 1) * LANES] for a in tiles]
                hi = [st_ref[r8, S5_LANES + a * LANES:S5_LANES + (a + 1) * LANES] for a in tiles]
                nr, ni = advance(r8, hr, hi)
                for n, a in enumerate(tiles):
                    st_ref[r8, a * LANES:(a + 1) * LANES] = nr[n]
                    st_ref[r8, S5_LANES + a * LANES:S5_LANES + (a + 1) * LANES] = ni[n]
                return carry
            lax.fori_loop(0, rows // SUBLANES, group, 0)
        else:
            def step(t, carry):
                nr, ni = advance(pl.ds(pl.multiple_of(t * SUBLANES, SUBLANES), SUBLANES), *carry)
                return tuple(nr), tuple(ni)
            hr0 = tuple(st_ref[:, a * LANES:(a + 1) * LANES] for a in tiles)
            hi0 = tuple(st_ref[:, S5_LANES + a * LANES:S5_LANES + (a + 1) * LANES] for a in tiles)
            hr, hi = lax.fori_loop(0, steps, step, (hr0, hi0))
            for n, a in enumerate(tiles):
                st_ref[:, a * LANES:(a + 1) * LANES] = hr[n]
                st_ref[:, S5_LANES + a * LANES:S5_LANES + (a + 1) * LANES] = hi[n]

    ys = []
    for k in range(S5_KCH):
        hre = jnp.concatenate([hs_ref[k * tpk + a] for a in range(tpk)], axis=1).astype(BF16)
        him = jnp.concatenate([hs_ref[nre + k * tpk + a] for a in range(tpk)], axis=1).astype(BF16)
        yk = _dot(hre, wc_ref[0, k]) + _dot(him, wc_ref[1, k])
        ys.append(yk + d_ref[:, k * LANES:(k + 1) * LANES] * u[:, k * LANES:(k + 1) * LANES])
    y = jax.nn.gelu(jnp.concatenate(ys, axis=1))
    out = (y * jax.nn.sigmoid(_dot(y.astype(BF16), wglu_ref[...]))).astype(y_ref.dtype)
    if steps > 1:
        out = _dot(to_seq_major, out).astype(y_ref.dtype)
    y_ref[...] = out.reshape(y_ref.shape)

    @pl.when(pl.program_id(0) == pl.num_programs(0) - 1)
    def _():
        hre_ref[...] = st_ref[:, :S5_LANES]
        him_ref[...] = st_ref[:, S5_LANES:]


def s5_mix(u, h0_re, h0_im, consts, *, rows, seq, steps):
    a_re, a_im, w_bu, w_c, d_skip, w_glu = consts
    if seq > 1:
        u_spec = pl.BlockSpec((rows, steps, S5_WIDTH), lambda c: (0, c, 0))
    else:
        u_spec = pl.BlockSpec((rows, S5_WIDTH), lambda c: (0, 0))
    full = lambda shape: pl.BlockSpec(shape, lambda c: (0,) * len(shape))
    return pl.pallas_call(
        functools.partial(_s5_kernel, rows=rows, steps=steps),
        grid=(seq // steps,),
        in_specs=[u_spec, full((rows, S5_LANES)), full((rows, S5_LANES)),
                  full((1, S5_LANES)), full((1, S5_LANES)),
                  full(w_bu.shape), full(w_c.shape), full((1, S5_WIDTH)), full(w_glu.shape)],
        out_specs=[u_spec, full((rows, S5_LANES)), full((rows, S5_LANES))],
        out_shape=[jax.ShapeDtypeStruct(u.shape[:-1] + (S5_WIDTH,), BF16),
                   jax.ShapeDtypeStruct((rows, S5_LANES), F32),
                   jax.ShapeDtypeStruct((rows, S5_LANES), F32)],
        scratch_shapes=[pltpu.VMEM((2 * S5_LANES // LANES, rows * steps, LANES), F32),
                        pltpu.VMEM((rows, 2 * S5_LANES), F32)],
        compiler_params=_params("arbitrary"),
        name="s5_mix",
    )(u, h0_re, h0_im, a_re, a_im, w_bu, w_c, d_skip, w_glu)


def s5_constants(a_re, a_im, log_dt, b_re, b_im, c_re, c_im, d_skip, w_glu):
    lr, li = a_re, a_im
    dt = jnp.exp(log_dt)[:, None]
    mag = jnp.exp(lr * dt)
    ab_re, ab_im = mag * jnp.cos(li * dt), mag * jnp.sin(li * dt)
    den = lr * lr + li * li
    f_re = ((ab_re - 1.0) * lr + ab_im * li) / den
    f_im = (ab_im * lr - (ab_re - 1.0) * li) / den
    bb_re = f_re[..., None] * b_re - f_im[..., None] * b_im
    bb_im = f_re[..., None] * b_im + f_im[..., None] * b_re
    gpc = LANES // S5_GROUP_CH
    eye = jnp.eye(gpc, dtype=F32)

    def bu_blocks(bb):
        t = bb.reshape(S5_KCH, gpc, S5_STATE, S5_GROUP_CH)
        return jnp.einsum('kgpc,gh->kgchp', t, eye).reshape(S5_KCH, LANES, gpc * S5_STATE)

    def c_blocks(c):
        t = c.reshape(S5_KCH, gpc, S5_GROUP_CH, S5_STATE)
        return jnp.einsum('kgcp,gh->kgphc', t, eye).reshape(S5_KCH, gpc * S5_STATE, LANES)

    w_bu = jnp.concatenate([bu_blocks(bb_re), bu_blocks(bb_im)], axis=-1).astype(BF16)
    w_c = jnp.stack([c_blocks(c_re), -c_blocks(c_im)]).astype(BF16)
    return (ab_re.reshape(1, S5_LANES), ab_im.reshape(1, S5_LANES), w_bu, w_c,
            d_skip.reshape(1, S5_WIDTH), w_glu.astype(BF16))


def _ffn_kernel(x_ref, g_ref, wg_ref, wu_ref, wd_ref, o_ref, xn_ref):
    @pl.when(pl.program_id(1) == 0)
    def _():
        x = x_ref[...]
        xn_ref[...] = _rms(x, g_ref[...]).astype(BF16)
        o_ref[...] = x

    xn = xn_ref[...]
    h = jax.nn.silu(_dot(xn, wg_ref[...])) * _dot(xn, wu_ref[...])
    o_ref[...] += _dot(h.astype(BF16), wd_ref[...])


def ffn(x, g, w_gate, w_up, w_down, *, tm, tf):
    m, d = x.shape
    f = w_gate.shape[1]
    return pl.pallas_call(
        _ffn_kernel,
        grid=(m // tm, f // tf),
        in_specs=[pl.BlockSpec((tm, d), lambda i, j: (i, 0), pipeline_mode=pl.Buffered(1)),
                  pl.BlockSpec((1, d), lambda i, j: (0, 0)),
                  pl.BlockSpec((d, tf), lambda i, j: (0, j)),
                  pl.BlockSpec((d, tf), lambda i, j: (0, j)),
                  pl.BlockSpec((tf, d), lambda i, j: (j, 0))],
        out_specs=pl.BlockSpec((tm, d), lambda i, j: (i, 0)),
        out_shape=jax.ShapeDtypeStruct((m, d), F32),
        scratch_shapes=[pltpu.VMEM((tm, d), BF16)],
        compiler_params=_params("parallel", "arbitrary"),
        name="ffn",
    )(x, g.reshape(1, d), w_gate, w_up, w_down)


MOE_TILE = 512
DMA_UNROLL = 8
META_I1, META_I2, META_W1, META_W2, META_R1, META_R2 = range(6)


def _router_kernel(x_ref, g_ref, r_ref, xn_ref, meta_ref, cnt_ref, carry_ref):
    @pl.when(pl.program_id(0) == 0)
    def _():
        carry_ref[...] = jnp.zeros_like(carry_ref)

    xn = _rms(x_ref[...], g_ref[...])
    xn_ref[...] = xn
    x_hi, x_mid, _ = _split3(xn)
    r_hi, r_mid, _ = _split3(r_ref[...])
    logits = _dot(x_hi, r_hi) + (_dot(x_mid, r_hi) + _dot(x_hi, r_mid))
    lane = lax.broadcasted_iota(jnp.int32, logits.shape, 1)
    neg = -jnp.inf
    l1 = jnp.where(lane < N_EXPERTS, logits, neg)
    m1 = jnp.max(l1, axis=-1, keepdims=True)
    i1 = jnp.min(jnp.where(l1 == m1, lane, LANES), axis=-1, keepdims=True)
    l2 = jnp.where(lane == i1, neg, l1)
    m2 = jnp.max(l2, axis=-1, keepdims=True)
    i2 = jnp.min(jnp.where(l2 == m2, lane, LANES), axis=-1, keepdims=True)
    e = jnp.exp(m2 - m1)
    den = 1.0 + e
    sel = jnp.where(lane == i1, 1.0, jnp.where(lane == i2, 1.0, 0.0))
    tm = sel.shape[0]
    row = lax.broadcasted_iota(jnp.int32, (tm, tm), 0)
    col = lax.broadcasted_iota(jnp.int32, (tm, tm), 1)
    incl = _dot(jnp.where(col <= row, 1.0, 0.0).astype(BF16), sel.astype(BF16))
    excl = incl - sel + carry_ref[0:1, :]
    rank1 = jnp.sum(jnp.where(lane == i1, excl, 0.0), axis=-1, keepdims=True)
    rank2 = jnp.sum(jnp.where(lane == i2, excl, 0.0), axis=-1, keepdims=True)
    carry_ref[...] = carry_ref[...] + incl[tm - 1:tm, :]
    cnt_ref[...] = carry_ref[...]
    fields = {META_I1: i1.astype(F32), META_I2: i2.astype(F32), META_W1: 1.0 / den, META_W2: e / den,
              META_R1: rank1, META_R2: rank2}
    meta = jnp.zeros(logits.shape, F32)
    for k, val in fields.items():
        meta = jnp.where(lane == k, val, meta)
    meta_ref[...] = meta


def router(x, g, r, *, tm):
    m, d = x.shape
    r_pad = jnp.zeros((d, LANES), F32).at[:, :N_EXPERTS].set(r)
    return pl.pallas_call(
        _router_kernel,
        grid=(m // tm,),
        in_specs=[pl.BlockSpec((tm, d), lambda i: (i, 0)),
                  pl.BlockSpec((1, d), lambda i: (0, 0)),
                  pl.BlockSpec((d, LANES), lambda i: (0, 0))],
        out_specs=[pl.BlockSpec((tm, d), lambda i: (i, 0)),
                   pl.BlockSpec((tm, LANES), lambda i: (i, 0)),
                   pl.BlockSpec((SUBLANES, LANES), lambda i: (0, 0))],
        out_shape=[jax.ShapeDtypeStruct((m, d), F32),
                   jax.ShapeDtypeStruct((m, LANES), F32),
                   jax.ShapeDtypeStruct((SUBLANES, LANES), F32)],
        scratch_shapes=[pltpu.VMEM((SUBLANES, LANES), F32)],
        compiler_params=_params("arbitrary"),
        name="router",
    )(x, g.reshape(1, d), r_pad)


def _dispatch_kernel(p1_ref, p2_ref, xn_ref, zeros_ref, xs_ref, sem, *, td):
    del zeros_ref
    base = pl.program_id(0) * td

    def copies(r):
        src = xn_ref.at[pl.ds(r, 1), :]
        return [pltpu.make_async_copy(src, xs_ref.at[pl.ds(p_ref[base + r], 1), :], sem)
                for p_ref in (p1_ref, p2_ref)]

    def issue(r, carry):
        for c in copies(r):
            c.start()
        return carry

    def drain(r, carry):
        for c in copies(r):
            c.wait()
        return carry

    lax.fori_loop(0, td, issue, 0, unroll=DMA_UNROLL)
    lax.fori_loop(0, td, drain, 0, unroll=DMA_UNROLL)


def moe_dispatch(xn, pos1, pos2, xs, *, td):
    m, d = xn.shape
    rows = xs.shape[0]
    return pl.pallas_call(
        functools.partial(_dispatch_kernel, td=td),
        grid_spec=pltpu.PrefetchScalarGridSpec(
            num_scalar_prefetch=2,
            grid=(m // td,),
            in_specs=[pl.BlockSpec((td, d), lambda i, p1, p2: (i, 0)),
                      pl.BlockSpec(memory_space=pl.ANY)],
            out_specs=pl.BlockSpec(memory_space=pl.ANY),
            scratch_shapes=[pltpu.SemaphoreType.DMA(())]),
        out_shape=jax.ShapeDtypeStruct((rows, d), F32),
        input_output_aliases={3: 0},
        compiler_params=_params("arbitrary"),
        name="moe_dispatch",
    )(pos1, pos2, xn, xs)


def _expert_kernel(te_ref, tv_ref, xs_ref, wg_ref, wu_ref, wd_ref, ys_ref, xb_ref):
    del te_ref
    f = pl.program_id(1)
    valid = tv_ref[pl.program_id(0)]

    @pl.when(valid > 0)
    def _():
        @pl.when(f == 0)
        def _():
            xb_ref[...] = xs_ref[...].astype(BF16)

        xb = xb_ref[...]
        h = jax.nn.silu(_dot(xb, wg_ref[0])) * _dot(xb, wu_ref[0])
        y = _dot(h.astype(BF16), wd_ref[0])

        @pl.when(f == 0)
        def _():
            ys_ref[...] = y

        @pl.when(f > 0)
        def _():
            ys_ref[...] += y

    @pl.when((valid == 0) & (f == 0))
    def _():
        ys_ref[...] = jnp.zeros_like(ys_ref)


def moe_experts(xs, tile_expert, tile_valid, w_gate, w_up, w_down, *, tile, tf):
    rows, d = xs.shape
    _, _, f = w_gate.shape
    nf = f // tf
    fidx = lambda t, j, tv: jnp.where(tv[t] > 0, j, nf - 1)
    return pl.pallas_call(
        _expert_kernel,
        grid_spec=pltpu.PrefetchScalarGridSpec(
            num_scalar_prefetch=2,
            grid=(rows // tile, nf),
            in_specs=[pl.BlockSpec((tile, d), lambda t, j, te, tv: (t, 0)),
                      pl.BlockSpec((1, d, tf), lambda t, j, te, tv: (te[t], 0, fidx(t, j, tv))),
                      pl.BlockSpec((1, d, tf), lambda t, j, te, tv: (te[t], 0, fidx(t, j, tv))),
                      pl.BlockSpec((1, tf, d), lambda t, j, te, tv: (te[t], fidx(t, j, tv), 0))],
            out_specs=pl.BlockSpec((tile, d), lambda t, j, te, tv: (t, 0)),
            scratch_shapes=[pltpu.VMEM((tile, d), BF16)]),
        out_shape=jax.ShapeDtypeStruct((rows, d), F32),
        compiler_params=_params("arbitrary", "arbitrary"),
        name="moe_experts",
    )(tile_expert, tile_valid, xs, w_gate, w_up, w_down)


def _combine_kernel(p1_ref, p2_ref, x_ref, meta_ref, ys_ref, o_ref, buf_ref, sem, *, tc):
    i = pl.program_id(0)
    slot = i % 2

    def copies(step, sl, r):
        tok = step * tc + r
        return [pltpu.make_async_copy(ys_ref.at[pl.ds(p_ref[tok], 1), :],
                                      buf_ref.at[sl, k, pl.ds(r, 1), :], sem.at[sl])
                for k, p_ref in enumerate((p1_ref, p2_ref))]

    def issue(step, sl):
        def body(r, carry):
            for c in copies(step, sl, r):
                c.start()
            return carry
        lax.fori_loop(0, tc, body, 0, unroll=DMA_UNROLL)

    @pl.when(i == 0)
    def _():
        issue(0, 0)

    @pl.when(i + 1 < pl.num_programs(0))
    def _():
        issue(i + 1, 1 - slot)

    def drain(r, carry):
        for c in copies(i, slot, r):
            c.wait()
        return carry
    lax.fori_loop(0, tc, drain, 0, unroll=DMA_UNROLL)

    meta = meta_ref[...]
    w1 = meta[:, META_W1:META_W1 + 1]
    w2 = meta[:, META_W2:META_W2 + 1]
    o_ref[...] = x_ref[...] + w1 * buf_ref[slot, 0] + w2 * buf_ref[slot, 1]


def moe_combine(x, meta, ys, pos1, pos2, *, tc):
    m, d = x.shape
    return pl.pallas_call(
        functools.partial(_combine_kernel, tc=tc),
        grid_spec=pltpu.PrefetchScalarGridSpec(
            num_scalar_prefetch=2,
            grid=(m // tc,),
            in_specs=[pl.BlockSpec((tc, d), lambda i, p1, p2: (i, 0)),
                      pl.BlockSpec((tc, LANES), lambda i, p1, p2: (i, 0)),
                      pl.BlockSpec(memory_space=pl.ANY)],
            out_specs=pl.BlockSpec((tc, d), lambda i, p1, p2: (i, 0)),
            scratch_shapes=[pltpu.VMEM((2, 2, tc, d), F32), pltpu.SemaphoreType.DMA((2,))]),
        out_shape=jax.ShapeDtypeStruct((m, d), F32),
        compiler_params=_params("arbitrary"),
        name="moe_combine",
    )(pos1, pos2, x, meta, ys)


def moe_sparse(xs_in, g, r, w_gate, w_up, w_down):
    d = xs_in[0].shape[1]
    m_total = sum(x.shape[0] for x in xs_in)
    tile = min(m_total, MOE_TILE)
    routed = [router(x, g, r, tm=min(x.shape[0], 512)) for x in xs_in]
    cnts = [counts[0, :N_EXPERTS].astype(jnp.int32) for _, _, counts in routed]
    cnt = sum(cnts)
    tiles_e = (cnt + tile - 1) // tile
    tile_end = jnp.cumsum(tiles_e)
    tile_start = tile_end - tiles_e
    n_tiles = (2 * m_total) // tile + N_EXPERTS
    t_all = jnp.arange(n_tiles, dtype=jnp.int32)
    t = jnp.minimum(t_all, tile_end[-1] - 1)
    tile_expert = jnp.sum(t[:, None] >= tile_end[None, :], axis=1).astype(jnp.int32)
    tile_valid = jnp.clip(cnt[tile_expert] - (t - tile_start[tile_expert]) * tile, 0, tile)
    tile_valid = jnp.where(t_all < tile_end[-1], tile_valid, 0).astype(jnp.int32)
    buf = jnp.zeros((n_tiles * tile, d), F32)
    base = tile_start * tile
    pos = []
    for (xn, meta, _), c in zip(routed, cnts):
        p1 = base[meta[:, META_I1].astype(jnp.int32)] + meta[:, META_R1].astype(jnp.int32)
        p2 = base[meta[:, META_I2].astype(jnp.int32)] + meta[:, META_R2].astype(jnp.int32)
        pos.append((p1, p2))
        buf = moe_dispatch(xn, p1, p2, buf, td=min(xn.shape[0], 256))
        base = base + c
    ys = moe_experts(buf, tile_expert, tile_valid, w_gate, w_up, w_down, tile=tile, tf=512)
    return [moe_combine(x, meta, ys, p1, p2, tc=min(x.shape[0], 256))
            for x, (_, meta, _), (p1, p2) in zip(xs_in, routed, pos)]


def _gates_kernel(ba_ref, alog_ref, dtb_ref, beta_ref, eg_ref, gcum_ref, *, tg):
    ba = ba_ref[...]
    b = ba[:, :DN_V_HEADS]
    a = ba[:, DN_V_HEADS:2 * DN_V_HEADS]
    beta_ref[...] = jax.nn.sigmoid(b)
    z = a + dtb_ref[...]
    softplus = jnp.maximum(z, 0.0) + jnp.log1p(jnp.exp(-jnp.abs(z)))
    g = -jnp.exp(alog_ref[...]) * softplus
    eg_ref[...] = jnp.exp(g)
    r = lax.broadcasted_iota(jnp.int32, (tg, tg), 0)
    c = lax.broadcasted_iota(jnp.int32, (tg, tg), 1)
    tri = jnp.where((c <= r) & (r // DN_CHUNK == c // DN_CHUNK), 1.0, 0.0).astype(BF16)
    hi, mid, lo = _split3(g)
    gcum_ref[...] = _dot(tri, hi) + (_dot(tri, mid) + _dot(tri, lo))


def dn_gates(ba, a_log, dt_bias, *, tg):
    m = ba.shape[0]
    out = jax.ShapeDtypeStruct((m, DN_V_HEADS), F32)
    spec = pl.BlockSpec((tg, DN_V_HEADS), lambda i: (i, 0))
    return pl.pallas_call(
        functools.partial(_gates_kernel, tg=tg),
        grid=(m // tg,),
        in_specs=[pl.BlockSpec((tg, LANES), lambda i: (i, 0)),
                  pl.BlockSpec((1, DN_V_HEADS), lambda i: (0, 0)),
                  pl.BlockSpec((1, DN_V_HEADS), lambda i: (0, 0))],
        out_specs=[spec, spec, spec],
        out_shape=[out, out, out],
        compiler_params=_params("parallel"),
        name="dn_gates",
    )(ba, a_log.reshape(1, DN_V_HEADS), dt_bias.reshape(1, DN_V_HEADS))


def _l2norm_heads(c, scale):
    parts = []
    for h in range(c.shape[1] // DN_K_DIM):
        t = c[:, h * DN_K_DIM:(h + 1) * DN_K_DIM]
        t = t * lax.rsqrt(jnp.sum(t * t, axis=-1, keepdims=True) + EPS)
        parts.append(t * scale if scale != 1.0 else t)
    return jnp.concatenate(parts, axis=1)


DN_GROUP = 8
DN_LBLOCK = 512
DN_BATCH = 8
DN_HBATCH = 4
DN_CONV_ROWS = 128


def _dn_chunk_kernel(q_ref, k_ref, v_ref, qh_ref, kh_ref, vh_ref, wq_ref, wk_ref, wv_ref,
                     z_ref, gc_ref, beta_ref, gt_ref, onorm_ref,
                     o_ref, s_out_ref, s_ref, xs_ref, add_ref, oacc_ref, egl_ref, qc_ref, kc_ref, vc_ref,
                     xq_ref, xk_ref, xv_ref, *, nc):
    hg = pl.program_id(1)
    lb = pl.program_id(2)
    cz = DN_CHUNK

    def conv(x_ref, halo_ref, w_ref, xcat_ref, out_ref, post):
        n = x_ref.shape[1]
        xcat_ref[0:SUBLANES] = jnp.where(lb > 0, halo_ref[0], 0.0)
        xcat_ref[SUBLANES:SUBLANES + n] = x_ref[0]
        first = SUBLANES - (DN_CONV_K - 1)
        for r0 in range(0, n, DN_CONV_ROWS):
            acc = xcat_ref[first + r0:first + r0 + DN_CONV_ROWS] * w_ref[0:1, :]
            for i in range(1, DN_CONV_K):
                acc = acc + xcat_ref[first + r0 + i:first + r0 + i + DN_CONV_ROWS] * w_ref[i:i + 1, :]
            out_ref[r0:r0 + DN_CONV_ROWS] = post(jax.nn.silu(acc))

    conv(q_ref, qh_ref, wq_ref, xq_ref, qc_ref, lambda c: _l2norm_heads(c, DN_K_DIM ** -0.5))
    conv(k_ref, kh_ref, wk_ref, xk_ref, kc_ref, lambda c: _l2norm_heads(c, 1.0))
    conv(v_ref, vh_ref, wv_ref, xv_ref, vc_ref, lambda c: c)

    @pl.when(lb == 0)
    def _():
        s_ref[...] = jnp.zeros_like(s_ref)

    gc_all = gc_ref[0]
    beta_all = beta_ref[0]
    lane = lax.broadcasted_iota(jnp.int32, gc_all.shape, 1)
    ri = lax.broadcasted_iota(jnp.int32, (cz, cz), 0)
    ci = lax.broadcasted_iota(jnp.int32, (cz, cz), 1)
    incl = (ri >= ci)[None]
    strict = (ri > ci)[None]

    rep = DN_V_HEADS // DN_K_HEADS
    for g0 in range(0, DN_GROUP, DN_HBATCH):
        gs = list(range(g0, g0 + DN_HBATCH))
        gcol_all, bcol_all, grow_all = {}, {}, {}
        for g in gs:
            head = hg * DN_GROUP + g
            gcol_all[g] = jnp.sum(jnp.where(lane == head, gc_all, 0.0), axis=-1, keepdims=True)
            bcol_all[g] = jnp.sum(jnp.where(lane == head, beta_all, 0.0), axis=-1, keepdims=True)
            grow_all[g] = gt_ref[0, pl.ds(head, 1), :]
        for c0 in range(0, nc, DN_BATCH):
            cb = min(DN_BATCH, nc - c0)
            rows = slice(c0 * cz, (c0 + cb) * cz)
            stack = lambda fn: jnp.concatenate([fn(g) for g in gs], axis=0)
            kcols = lambda g: slice(g // rep * DN_K_DIM, (g // rep + 1) * DN_K_DIM)
            q = stack(lambda g: qc_ref[rows, kcols(g)].reshape(cb, cz, DN_K_DIM))
            k = stack(lambda g: kc_ref[rows, kcols(g)].reshape(cb, cz, DN_K_DIM))
            v = stack(lambda g: vc_ref[rows, g * DN_V_DIM:(g + 1) * DN_V_DIM].reshape(cb, cz, DN_V_DIM))
            gcol = stack(lambda g: gcol_all[g][rows].reshape(cb, cz, 1))
            bcol = stack(lambda g: bcol_all[g][rows].reshape(cb, cz, 1))
            grow = stack(lambda g: jnp.stack([grow_all[g][:, c * cz:(c + 1) * cz]
                                              for c in range(c0, c0 + cb)]))
            decay = jnp.where(incl, jnp.exp(gcol - grow), 0.0)
            kb = k * bcol
            kbf = k.astype(BF16)
            kk = jnp.einsum('cid,cjd->cij', kb.astype(BF16), kbf, preferred_element_type=F32)
            neg_l = jnp.where(strict, -(kk * decay), 0.0)
            n_acc = neg_l
            pw = neg_l
            for _ in range(5):
                pwb = pw.astype(BF16)
                pw = jnp.einsum('cij,cjk->cik', pwb, pwb, preferred_element_type=F32)
                n_acc = n_acc + pw + jnp.einsum('cij,cjk->cik', n_acc.astype(BF16), pw.astype(BF16),
                                                preferred_element_type=F32)
            egc = jnp.exp(gcol)
            rhs = jnp.concatenate([kb * egc, v * bcol], axis=-1)
            bmm = lambda x, y: jnp.einsum('cij,cjd->cid', x, y, preferred_element_type=F32)
            wu = (rhs + bmm(n_acc.astype(BF16), rhs.astype(BF16))).astype(BF16)
            qk = jnp.einsum('cid,cjd->cij', q.astype(BF16), kbf, preferred_element_type=F32) * decay
            a_wu = bmm(qk.astype(BF16), wu)
            glast = gcol[:, cz - 1:cz, :]
            kd = k * jnp.exp(glast - gcol)
            qa = (q * egc - a_wu[..., :DN_K_DIM]).astype(BF16)
            egl = jnp.broadcast_to(jnp.exp(glast), (len(gs) * cb, 1, DN_V_DIM))
            for n, g in enumerate(gs):
                for c in range(cb):
                    kd_wu = _dot(jnp.transpose(kd[n * cb + c]).astype(BF16), wu[n * cb + c])
                    xs_ref[g, c0 + c, 0:DN_K_DIM, :] = (-kd_wu[:, :DN_K_DIM]).astype(BF16)
                    add_ref[g, c0 + c, 0:DN_K_DIM, :] = kd_wu[:, DN_K_DIM:]
                xs_ref[g, c0:c0 + cb, DN_K_DIM:DN_K_DIM + cz, :] = qa[n * cb:(n + 1) * cb]
                add_ref[g, c0:c0 + cb, DN_K_DIM:DN_K_DIM + cz, :] = a_wu[n * cb:(n + 1) * cb, :, DN_K_DIM:]
                egl_ref[g, c0:c0 + cb] = egl[n * cb:(n + 1) * cb]

    def chunk(c, carry):
        for g in range(DN_GROUP):
            s = s_ref[g]
            r = _dot(xs_ref[g, c], s.astype(BF16)) + add_ref[g, c]
            oacc_ref[g, pl.ds(pl.multiple_of(c * cz, cz), cz), :] = r[DN_K_DIM:DN_K_DIM + cz]
            s_ref[g] = s * egl_ref[g, c] + r[0:DN_K_DIM]
        return carry

    lax.fori_loop(0, nc, chunk, 0)

    for g in range(DN_GROUP):
        o = oacc_ref[g]
        z = z_ref[0, :, g * DN_V_DIM:(g + 1) * DN_V_DIM]
        o_ref[0, :, g * DN_V_DIM:(g + 1) * DN_V_DIM] = (
            _rms(o, onorm_ref[...]) * jax.nn.silu(z)).astype(o_ref.dtype)

    @pl.when(lb == pl.num_programs(2) - 1)
    def _():
        s_out_ref[0] = s_ref[...]


def dn_chunked(proj, conv_w, gcum, beta, gcum_t, out_norm):
    b, l, _ = proj.shape
    lbk = min(DN_LBLOCK, l)
    nc = lbk // DN_CHUNK
    gk = DN_GROUP // 2 * DN_K_DIM
    gv = DN_GROUP * DN_V_DIM
    cz = DN_CHUNK
    qcol = lambda h: h
    kcol = lambda h: DN_QK_W // gk + h
    vcol = lambda h: 2 * DN_QK_W // gv + h
    hb = lbk // SUBLANES
    cur = lambda w, col: pl.BlockSpec((1, lbk, w), lambda i, h, t: (i, t, col(h)))
    halo = lambda w, col: pl.BlockSpec((1, SUBLANES, w), lambda i, h, t: (i, jnp.maximum(t * hb - 1, 0), col(h)))
    taps = lambda w, col: pl.BlockSpec((DN_CONV_K, w), lambda i, h, t: (0, col(h)))
    return pl.pallas_call(
        functools.partial(_dn_chunk_kernel, nc=nc),
        grid=(b, DN_V_HEADS // DN_GROUP, l // lbk),
        in_specs=[cur(gk, qcol), cur(gk, kcol), cur(gv, vcol),
                  halo(gk, qcol), halo(gk, kcol), halo(gv, vcol),
                  taps(gk, qcol), taps(gk, kcol), taps(gv, vcol),
                  pl.BlockSpec((1, lbk, gv), lambda i, h, t: (i, t, DN_CONV_DIM // gv + h)),
                  pl.BlockSpec((1, lbk, DN_V_HEADS), lambda i, h, t: (i, t, 0)),
                  pl.BlockSpec((1, lbk, DN_V_HEADS), lambda i, h, t: (i, t, 0)),
                  pl.BlockSpec((1, DN_V_HEADS, lbk), lambda i, h, t: (i, 0, t)),
                  pl.BlockSpec((1, DN_V_DIM), lambda i, h, t: (0, 0))],
        out_specs=[pl.BlockSpec((1, lbk, gv), lambda i, h, t: (i, t, h)),
                   pl.BlockSpec((1, DN_GROUP, DN_K_DIM, DN_V_DIM), lambda i, h, t: (i, h, 0, 0))],
        out_shape=[jax.ShapeDtypeStruct((b, l, DN_V_W), BF16),
                   jax.ShapeDtypeStruct((b, DN_V_HEADS, DN_K_DIM, DN_V_DIM), F32)],
        scratch_shapes=[pltpu.VMEM((DN_GROUP, DN_K_DIM, DN_V_DIM), F32),
                        pltpu.VMEM((DN_GROUP, nc, DN_K_DIM + cz, DN_K_DIM), BF16),
                        pltpu.VMEM((DN_GROUP, nc, DN_K_DIM + cz, DN_V_DIM), F32),
                        pltpu.VMEM((DN_GROUP, lbk, DN_V_DIM), F32),
                        pltpu.VMEM((DN_GROUP, nc, 1, DN_V_DIM), F32),
                        pltpu.VMEM((lbk, gk), F32), pltpu.VMEM((lbk, gk), F32), pltpu.VMEM((lbk, gv), F32),
                        pltpu.VMEM((lbk + SUBLANES, gk), F32), pltpu.VMEM((lbk + SUBLANES, gk), F32),
                        pltpu.VMEM((lbk + SUBLANES, gv), F32)],
        compiler_params=_params("parallel", "parallel", "arbitrary"),
        name="dn_chunked",
    )(proj, proj, proj, proj, proj, proj, conv_w, conv_w, conv_w, proj, gcum, beta, gcum_t,
      out_norm.reshape(1, DN_V_DIM))


DN_ROWS = DN_CONV_DIM // LANES
DN_QROWS = DN_K_HEADS
DN_VROW0 = 2 * DN_K_HEADS
DN_SAMPLE_SEQS = 1


def _dn_sample_kernel(x_ref, buf_ref, w_ref, z_ref, eg_ref, beta_ref, s_ref, onorm_ref,
                      o_ref, s_out_ref, buf_out_ref, *, bt):
    for i in range(bt):
        b = pl.program_id(0) * bt + i
        x = x_ref[i]
        buf = buf_ref[i]
        conv = buf[0] * w_ref[0]
        for t in range(1, DN_CONV_K - 1):
            conv = conv + buf[t] * w_ref[t]
        conv = conv + x * w_ref[DN_CONV_K - 1]
        buf_out_ref[i, 0:DN_CONV_K - 2] = buf[1:DN_CONV_K - 1]
        buf_out_ref[i, DN_CONV_K - 2] = x
        c = jax.nn.silu(conv)
        qk = c[0:DN_VROW0]
        qk = qk * lax.rsqrt(jnp.sum(qk * qk, axis=-1, keepdims=True) + EPS)
        q_t = jnp.transpose(qk[0:DN_QROWS] * (DN_K_DIM ** -0.5))
        k_t = jnp.transpose(qk[DN_QROWS:DN_VROW0])
        for h in range(DN_V_HEADS):
            kh = h // (DN_V_HEADS // DN_K_HEADS)
            kcol = k_t[:, kh:kh + 1]
            qcol = q_t[:, kh:kh + 1]
            v = c[DN_VROW0 + h:DN_VROW0 + h + 1]
            s = s_ref[i, h] * eg_ref[b, h]
            delta = (v - jnp.sum(kcol * s, axis=0, keepdims=True)) * beta_ref[b, h]
            s = s + kcol * delta
            s_out_ref[i, h] = s
            o = jnp.sum(qcol * s, axis=0, keepdims=True)
            o_ref[i, h:h + 1, :] = (_rms(o, onorm_ref[...]) * jax.nn.silu(z_ref[i, h:h + 1, :])
                                    ).astype(o_ref.dtype)


def dn_sample(qkv_rows, conv_buf, conv_w, z_rows, eg, beta, state, out_norm, *, bt=DN_SAMPLE_SEQS):
    nb = qkv_rows.shape[0]
    smem = pl.BlockSpec(memory_space=pltpu.SMEM)
    return pl.pallas_call(
        functools.partial(_dn_sample_kernel, bt=bt),
        grid=(nb // bt,),
        in_specs=[pl.BlockSpec((bt, DN_ROWS, LANES), lambda i: (i, 0, 0)),
                  pl.BlockSpec((bt, DN_CONV_K - 1, DN_ROWS, LANES), lambda i: (i, 0, 0, 0)),
                  pl.BlockSpec((DN_CONV_K, DN_ROWS, LANES), lambda i: (0, 0, 0)),
                  pl.BlockSpec((bt, DN_V_HEADS, DN_V_DIM), lambda i: (i, 0, 0)),
                  smem, smem,
                  pl.BlockSpec((bt, DN_V_HEADS, DN_K_DIM, DN_V_DIM), lambda i: (i, 0, 0, 0)),
                  pl.BlockSpec((1, DN_V_DIM), lambda i: (0, 0))],
        out_specs=[pl.BlockSpec((bt, DN_V_HEADS, DN_V_DIM), lambda i: (i, 0, 0)),
                   pl.BlockSpec((bt, DN_V_HEADS, DN_K_DIM, DN_V_DIM), lambda i: (i, 0, 0, 0)),
                   pl.BlockSpec((bt, DN_CONV_K - 1, DN_ROWS, LANES), lambda i: (i, 0, 0, 0))],
        out_shape=[jax.ShapeDtypeStruct((nb, DN_V_HEADS, DN_V_DIM), BF16),
                   jax.ShapeDtypeStruct(state.shape, F32),
                   jax.ShapeDtypeStruct(conv_buf.shape, F32)],
        compiler_params=_params("arbitrary"),
        name="dn_sample",
    )(qkv_rows, conv_buf, conv_w, z_rows, eg, beta, state, out_norm.reshape(1, DN_V_DIM))


def _tile(m, cap):
    return min(m, cap)


def _even_layer(x, cache, w, *, batch, seq):
    m = x.shape[0]
    tm = _tile(m, 1024)
    proj = norm_matmul(x, w['norm_mix'], w['w_in'], tm=tm, tn=512)
    if cache is None:
        attn, new_k, new_v = attn_prompt(proj, w['q_norm'], w['k_norm'], w['sinks'], batch=batch, seq=seq,
                                         col0=S5_WIDTH)
        zeros = jnp.zeros((batch, S5_LANES), F32)
        ssm, h_re, h_im = s5_mix(proj.reshape(batch, seq, EVEN_IN), zeros, zeros, w['s5'],
                                 rows=batch, seq=seq, steps=64)
        ssm = ssm.reshape(m, S5_WIDTH)
    else:
        k_win, v_win, h0_re, h0_im = cache
        attn, new_k, new_v = attn_sample(proj, k_win.reshape(batch, WINDOW, KV_WIDTH),
                                         v_win.reshape(batch, WINDOW, KV_WIDTH),
                                         w['q_norm'], w['k_norm'], w['sinks'], col0=S5_WIDTH)
        ssm, h_re, h_im = s5_mix(proj, h0_re.reshape(batch, S5_LANES), h0_im.reshape(batch, S5_LANES),
                                 w['s5'], rows=batch, seq=1, steps=1)
    x = matmul_residual(x, [attn, ssm], [w['w_out_a'], w['w_out_b']], tm=tm, tn=1024)
    x = ffn(x, w['norm_ffn'], w['ffn_gate'], w['ffn_up'], w['ffn_down'], tm=tm, tf=512)
    shp = (batch, WINDOW, N_KV_HEADS, HEAD_DIM)
    st = (batch, S5_GROUPS, S5_STATE)
    return x, new_k.reshape(shp), new_v.reshape(shp), h_re.reshape(st), h_im.reshape(st)


def _moe(xs_in, w):
    return moe_sparse(xs_in, w['norm_ffn'], w['router'], w['exp_gate'], w['exp_up'], w['exp_down'])


def _odd_layer(x, cache, w, *, batch, seq):
    x, s_new, new_buf = _odd_mixer(x, cache, w, batch=batch, seq=seq)
    return _moe([x], w)[0], s_new, new_buf


def _odd_mixer(x, cache, w, *, batch, seq):
    m = x.shape[0]
    tm = _tile(m, 1024)
    proj = norm_matmul(x, w['norm_mix'], w['w_in_main'], tm=tm, tn=1024)
    ba = norm_matmul(x, w['norm_mix'], w['w_in_ba'], tm=tm, tn=LANES)
    beta, eg, gcum = dn_gates(ba, w['a_log'], w['dt_bias'], tg=_tile(m, 512))
    if cache is None:
        proj3 = proj.reshape(batch, seq, DN_CONV_DIM + DN_V_W)
        gcum3 = gcum.reshape(batch, seq, DN_V_HEADS)
        o, s_new = dn_chunked(proj3, w['conv_w'], gcum3, beta.reshape(batch, seq, DN_V_HEADS),
                              jnp.swapaxes(gcum3, 1, 2), w['out_norm'])
        o = o.reshape(m, DN_V_W)
        new_buf = proj3[:, seq - (DN_CONV_K - 1):, :DN_CONV_DIM]
    else:
        s0, conv_buf = cache
        o, s_new, new_buf = dn_sample(
            proj[:, :DN_CONV_DIM].reshape(batch, DN_ROWS, LANES),
            conv_buf.reshape(batch, DN_CONV_K - 1, DN_ROWS, LANES),
            w['conv_w'].reshape(DN_CONV_K, DN_ROWS, LANES),
            proj[:, DN_CONV_DIM:].reshape(batch, DN_V_HEADS, DN_V_DIM),
            eg, beta, s0, w['out_norm'])
        o = o.reshape(m, DN_V_W)
        new_buf = new_buf.reshape(batch, DN_CONV_K - 1, DN_CONV_DIM)
    x = matmul_residual(x, [o], [w['w_out']], tm=tm, tn=1024)
    return x, s_new, new_buf


def kernel(x_prompt, x_sample, cache_win_k, cache_win_v, state_s5_re, state_s5_im, state_dn, state_dn_conv,
           e_norm_mix, e_w_in, e_q_norm, e_k_norm, e_sinks,
           e_s5_a_re, e_s5_a_im, e_s5_log_dt, e_s5_b_re, e_s5_b_im, e_s5_c_re, e_s5_c_im, e_s5_d, e_s5_w_glu,
           e_w_out, e_norm_ffn, e_ffn_w_gate, e_ffn_w_up, e_ffn_w_down,
           o_norm_mix, o_w_in, o_conv_w, o_a_log, o_dt_bias, o_out_norm, o_w_out, o_norm_ffn,
           o_router, o_exp_w_gate, o_exp_w_up, o_exp_w_down):
    bp, lp, d = x_prompt.shape
    bs, ls, _ = x_sample.shape
    assert ls == 1, "the sample group advances one token per step"
    hp = x_prompt.reshape(bp * lp, d)
    hs = x_sample.reshape(bs * ls, d)
    qkv_w = ATTN_WIDTH + 2 * KV_WIDTH
    main_w = DN_CONV_DIM + DN_V_W

    j = 0
    we = dict(
        norm_mix=e_norm_mix[j], q_norm=e_q_norm[j], k_norm=e_k_norm[j], sinks=e_sinks[j],
        w_in=jnp.concatenate([e_w_in[j, :, qkv_w:], e_w_in[j, :, :qkv_w]], axis=1).astype(BF16),
        s5=s5_constants(e_s5_a_re[j], e_s5_a_im[j], e_s5_log_dt[j], e_s5_b_re[j], e_s5_b_im[j],
                        e_s5_c_re[j], e_s5_c_im[j], e_s5_d[j], e_s5_w_glu[j]),
        w_out_a=e_w_out[j, :ATTN_WIDTH].astype(BF16), w_out_b=e_w_out[j, ATTN_WIDTH:].astype(BF16),
        norm_ffn=e_norm_ffn[j], ffn_gate=e_ffn_w_gate[j].astype(BF16),
        ffn_up=e_ffn_w_up[j].astype(BF16), ffn_down=e_ffn_w_down[j].astype(BF16))
    ba_pad = jnp.zeros((d, LANES), F32).at[:, :2 * DN_V_HEADS].set(o_w_in[j, :, main_w:])
    wo = dict(
        norm_mix=o_norm_mix[j], w_in_main=o_w_in[j, :, :main_w].astype(BF16), w_in_ba=ba_pad.astype(BF16),
        conv_w=o_conv_w[j], a_log=o_a_log[j], dt_bias=o_dt_bias[j], out_norm=o_out_norm[j],
        w_out=o_w_out[j].astype(BF16), norm_ffn=o_norm_ffn[j], router=o_router[j],
        exp_gate=o_exp_w_gate[j].astype(BF16), exp_up=o_exp_w_up[j].astype(BF16),
        exp_down=o_exp_w_down[j].astype(BF16))

    hp, kp, vp, rp, ip = _even_layer(hp, None, we, batch=bp, seq=lp)
    hs, ks, vs, rs, is_ = _even_layer(
        hs, (cache_win_k[j], cache_win_v[j], state_s5_re[j], state_s5_im[j]), we, batch=bs, seq=1)
    hp, sp, cp = _odd_mixer(hp, None, wo, batch=bp, seq=lp)
    hs, ss, cs = _odd_mixer(hs, (state_dn[j], state_dn_conv[j]), wo, batch=bs, seq=1)
    hp, hs = _moe([hp, hs], wo)

    one = lambda t: t[None]
    return (hp.reshape(bp, lp, d), hs.reshape(bs, ls, d),
            one(kp), one(vp), one(rp), one(ip), one(sp), one(cp),
            one(ks), one(vs), one(rs), one(is_), one(ss), one(cs))
```

```python
import functools

import jax
import jax.numpy as jnp
from jax import lax
from jax.experimental import pallas as pl
from jax.experimental.pallas import tpu as pltpu

F32 = jnp.float32
BF16 = jnp.bfloat16

D_MODEL = 2048
N_HEADS = 16
N_KV_HEADS = 4
HEAD_DIM = 64
Q_PER_KV = N_HEADS // N_KV_HEADS
WINDOW = 128
ATTN_WIDTH = N_HEADS * HEAD_DIM
KV_WIDTH = N_KV_HEADS * HEAD_DIM
ATTN_SCALE = HEAD_DIM ** -0.5
S5_WIDTH = D_MODEL // 2
S5_GROUP_CH = 16
S5_GROUPS = S5_WIDTH // S5_GROUP_CH
S5_STATE = 64
S5_LANES = S5_GROUPS * S5_STATE
EVEN_IN = ATTN_WIDTH + 2 * KV_WIDTH + S5_WIDTH
DN_K_HEADS = 16
DN_V_HEADS = 32
DN_K_DIM = 128
DN_V_DIM = 128
DN_CONV_K = 4
DN_CHUNK = 64
DN_QK_W = DN_K_HEADS * DN_K_DIM
DN_V_W = DN_V_HEADS * DN_V_DIM
DN_CONV_DIM = 2 * DN_QK_W + DN_V_W
D_FF = 5632
N_EXPERTS = 8
EPS = 1e-6
NEG_INF = -1e30

LANES = 128
SUBLANES = 8
VMEM_LIMIT = 56 * 1024 * 1024


def _params(*sem):
    return pltpu.CompilerParams(dimension_semantics=sem, vmem_limit_bytes=VMEM_LIMIT)


def _rms(x, g):
    return x * lax.rsqrt(jnp.mean(x * x, axis=-1, keepdims=True) + EPS) * g


def _dot(a, b):
    return jnp.dot(a, b, preferred_element_type=F32)


def _dot_nt(a, b):
    return lax.dot_general(a, b, (((1,), (1,)), ((), ())), preferred_element_type=F32)


def _split3(x):
    hi = x.astype(BF16)
    r = x - hi.astype(F32)
    mid = r.astype(BF16)
    lo = (r - mid.astype(F32)).astype(BF16)
    return hi, mid, lo


def _norm_matmul_kernel(x_ref, g_ref, w_ref, o_ref, xn_ref):
    @pl.when(pl.program_id(1) == 0)
    def _():
        xn_ref[...] = _rms(x_ref[...], g_ref[...]).astype(BF16)

    o_ref[...] = _dot(xn_ref[...], w_ref[...]).astype(o_ref.dtype)


def norm_matmul(x, g, w, *, tm, tn, out_dtype=F32):
    m, d = x.shape
    n = w.shape[1]
    return pl.pallas_call(
        _norm_matmul_kernel,
        grid=(m // tm, n // tn),
        in_specs=[pl.BlockSpec((tm, d), lambda i, j: (i, 0)),
                  pl.BlockSpec((1, d), lambda i, j: (0, 0)),
                  pl.BlockSpec((d, tn), lambda i, j: (0, j))],
        out_specs=pl.BlockSpec((tm, tn), lambda i, j: (i, j)),
        out_shape=jax.ShapeDtypeStruct((m, n), out_dtype),
        scratch_shapes=[pltpu.VMEM((tm, d), BF16)],
        compiler_params=_params("parallel", "arbitrary"),
        name="norm_matmul",
    )(x, g.reshape(1, d), w)


def _matmul_residual_kernel(*refs, n_pairs):
    x_ref = refs[0]
    o_ref = refs[-1]
    acc = x_ref[...]
    for a_ref, w_ref in zip(refs[1:1 + n_pairs], refs[1 + n_pairs:1 + 2 * n_pairs]):
        acc = acc + _dot(a_ref[...], w_ref[...])
    o_ref[...] = acc


def matmul_residual(x, a_list, w_list, *, tm, tn):
    m, n = x.shape
    in_specs = [pl.BlockSpec((tm, tn), lambda i, j: (i, j))]
    in_specs += [pl.BlockSpec((tm, a.shape[1]), lambda i, j: (i, 0)) for a in a_list]
    in_specs += [pl.BlockSpec((w.shape[0], tn), lambda i, j: (0, j)) for w in w_list]
    return pl.pallas_call(
        functools.partial(_matmul_residual_kernel, n_pairs=len(a_list)),
        grid=(m // tm, n // tn),
        in_specs=in_specs,
        out_specs=pl.BlockSpec((tm, tn), lambda i, j: (i, j)),
        out_shape=jax.ShapeDtypeStruct((m, n), F32),
        compiler_params=_params("parallel", "arbitrary"),
        name="matmul_residual",
    )(x, *a_list, *w_list)


HEADS_PER_TILE = LANES // HEAD_DIM


def _head_rms(x, seg, g):
    hi, mid, lo = _split3(x * x)
    parts = []
    for a in range(x.shape[1] // LANES):
        sl = slice(a * LANES, (a + 1) * LANES)
        parts.append(_dot(hi[:, sl], seg) + (_dot(mid[:, sl], seg) + _dot(lo[:, sl], seg)))
    ms = jnp.concatenate(parts, axis=1) * (1.0 / HEAD_DIM)
    return x * lax.rsqrt(ms + EPS) * g


def _attn_prompt_kernel(q_ref, kc_ref, kp_ref, vc_ref, vp_ref, qn_ref, kn_ref, sink_ref,
                        o_ref, wk_ref, wv_ref, s_ref, p_ref):
    has_prev = pl.program_id(1) > 0
    r = lax.broadcasted_iota(jnp.int32, (LANES, LANES), 0)
    c = lax.broadcasted_iota(jnp.int32, (LANES, LANES), 1)
    seg = jnp.where(r // HEAD_DIM == c // HEAD_DIM, 1.0, 0.0).astype(BF16)
    low = lax.broadcasted_iota(jnp.int32, (1, LANES), 1) < HEAD_DIM

    qn = _head_rms(q_ref[...], seg, qn_ref[...]) * ATTN_SCALE
    kc = _head_rms(kc_ref[...], seg, kn_ref[...])
    kp = _head_rms(kp_ref[...], seg, kn_ref[...])
    vc = vc_ref[...]
    wk_ref[0] = kc
    wv_ref[0] = vc
    kband = jnp.concatenate([kp, kc], axis=0)
    vband = jnp.concatenate([vp_ref[...], vc], axis=0)

    v_halves = []
    for kvh in range(N_KV_HEADS):
        tl = slice(kvh // HEADS_PER_TILE * LANES, (kvh // HEADS_PER_TILE + 1) * LANES)

        def both_halves(t):
            rolled = pltpu.roll(t, HEAD_DIM, axis=1)
            return jnp.where(low, t, rolled) if kvh % HEADS_PER_TILE == 0 else jnp.where(low, rolled, t)

        kdup = both_halves(kband[:, tl]).astype(BF16)
        vdup = both_halves(vband[:, tl])
        v_halves.append((jnp.where(low, vdup, 0.0).astype(BF16), jnp.where(low, 0.0, vdup).astype(BF16)))
        for pr in range(Q_PER_KV // HEADS_PER_TILE):
            a = kvh * Q_PER_KV // HEADS_PER_TILE + pr
            qt = qn[:, a * LANES:(a + 1) * LANES]
            s_ref[HEADS_PER_TILE * a] = _dot_nt(jnp.where(low, qt, 0.0).astype(BF16), kdup)
            s_ref[HEADS_PER_TILE * a + 1] = _dot_nt(jnp.where(low, 0.0, qt).astype(BF16), kdup)

    row = lax.broadcasted_iota(jnp.int32, (WINDOW, 2 * WINDOW), 0)
    col = lax.broadcasted_iota(jnp.int32, (WINDOW, 2 * WINDOW), 1)
    rel = (WINDOW + row) - col
    mask = (rel >= 0) & (rel <= WINDOW) & ((col >= WINDOW) | has_prev)
    for h in range(N_HEADS):
        s = jnp.where(mask, s_ref[h], NEG_INF)
        sk = sink_ref[h:h + 1, :]
        m = jnp.maximum(jnp.max(s, axis=-1, keepdims=True), sk)
        p = jnp.exp(s - m)
        p = p / (jnp.sum(p, axis=-1, keepdims=True) + jnp.exp(sk - m))
        p_ref[h] = p.astype(BF16)

    for a in range(N_HEADS // HEADS_PER_TILE):
        v_lo, v_hi = v_halves[a * HEADS_PER_TILE // Q_PER_KV]
        o = _dot(p_ref[HEADS_PER_TILE * a], v_lo) + _dot(p_ref[HEADS_PER_TILE * a + 1], v_hi)
        o_ref[:, a * LANES:(a + 1) * LANES] = o.astype(o_ref.dtype)


def attn_prompt(qkv, q_norm, k_norm, sinks, *, batch, seq, col0):
    nb = seq // WINDOW
    qcol = col0 // ATTN_WIDTH
    kcol = (col0 + ATTN_WIDTH) // KV_WIDTH
    cur = lambda c: (lambda b, n: (b * nb + n, c))
    prev = lambda c: (lambda b, n: (jnp.maximum(b * nb + n - 1, 0), c))
    return pl.pallas_call(
        _attn_prompt_kernel,
        grid=(batch, nb),
        in_specs=[pl.BlockSpec((WINDOW, ATTN_WIDTH), cur(qcol)),
                  pl.BlockSpec((WINDOW, KV_WIDTH), cur(kcol)),
                  pl.BlockSpec((WINDOW, KV_WIDTH), prev(kcol)),
                  pl.BlockSpec((WINDOW, KV_WIDTH), cur(kcol + 1)),
                  pl.BlockSpec((WINDOW, KV_WIDTH), prev(kcol + 1)),
                  pl.BlockSpec((1, ATTN_WIDTH), lambda b, n: (0, 0)),
                  pl.BlockSpec((1, KV_WIDTH), lambda b, n: (0, 0)),
                  pl.BlockSpec((N_HEADS, 1), lambda b, n: (0, 0))],
        out_specs=[pl.BlockSpec((WINDOW, ATTN_WIDTH), lambda b, n: (b * nb + n, 0)),
                   pl.BlockSpec((1, WINDOW, KV_WIDTH), lambda b, n: (b, 0, 0)),
                   pl.BlockSpec((1, WINDOW, KV_WIDTH), lambda b, n: (b, 0, 0))],
        out_shape=[jax.ShapeDtypeStruct((batch * seq, ATTN_WIDTH), BF16),
                   jax.ShapeDtypeStruct((batch, WINDOW, KV_WIDTH), F32),
                   jax.ShapeDtypeStruct((batch, WINDOW, KV_WIDTH), F32)],
        scratch_shapes=[pltpu.VMEM((N_HEADS, WINDOW, 2 * WINDOW), F32),
                        pltpu.VMEM((N_HEADS, WINDOW, 2 * WINDOW), BF16)],
        compiler_params=_params("parallel", "arbitrary"),
        name="attn_prompt",
    )(qkv, qkv, qkv, qkv, qkv, jnp.tile(q_norm, N_HEADS).reshape(1, ATTN_WIDTH),
      jnp.tile(k_norm, N_KV_HEADS).reshape(1, KV_WIDTH), sinks.reshape(N_HEADS, 1))


def _attn_sample_kernel(qkv_ref, ck_ref, cv_ref, qn_ref, kn_ref, sink_ref,
                        o_ref, wk_ref, wv_ref, *, col0):
    qg = qn_ref[...]
    kg = kn_ref[...]
    kv0 = col0 + ATTN_WIDTH
    v0 = col0 + ATTN_WIDTH + KV_WIDTH
    wk_ref[:, 0:WINDOW - 1, :] = ck_ref[:, 1:WINDOW, :]
    wv_ref[:, 0:WINDOW - 1, :] = cv_ref[:, 1:WINDOW, :]
    wv_ref[:, WINDOW - 1:WINDOW, :] = qkv_ref[:, v0:v0 + KV_WIDTH][:, None, :]
    for kvh in range(N_KV_HEADS):
        sl = slice(kvh * HEAD_DIM, (kvh + 1) * HEAD_DIM)
        kn = _rms(qkv_ref[:, kv0 + kvh * HEAD_DIM:kv0 + (kvh + 1) * HEAD_DIM], kg)[:, None, :]
        vn = qkv_ref[:, v0 + kvh * HEAD_DIM:v0 + (kvh + 1) * HEAD_DIM][:, None, :]
        wk_ref[:, WINDOW - 1:WINDOW, sl] = kn
        q4 = jnp.concatenate(
            [_rms(qkv_ref[:, col0 + (kvh * Q_PER_KV + g) * HEAD_DIM:col0 + (kvh * Q_PER_KV + g + 1) * HEAD_DIM],
                  qg)[:, None, :]
             for g in range(Q_PER_KV)], axis=1)
        kwin = ck_ref[:, :, sl].astype(BF16)
        vwin = cv_ref[:, :, sl].astype(BF16)
        s_c = jnp.einsum('bqd,bkd->bqk', q4.astype(BF16), kwin, preferred_element_type=F32) * ATTN_SCALE
        s_n = jnp.sum(q4 * kn, axis=-1, keepdims=True) * ATTN_SCALE
        sk = sink_ref[kvh * Q_PER_KV:(kvh + 1) * Q_PER_KV, :][None]
        m = jnp.maximum(jnp.maximum(jnp.max(s_c, axis=-1, keepdims=True), s_n), sk)
        p_c = jnp.exp(s_c - m)
        p_n = jnp.exp(s_n - m)
        den = jnp.sum(p_c, axis=-1, keepdims=True) + p_n + jnp.exp(sk - m)
        o4 = jnp.einsum('bqk,bkd->bqd', (p_c / den).astype(BF16), vwin,
                        preferred_element_type=F32) + (p_n / den) * vn
        for g in range(Q_PER_KV):
            h = kvh * Q_PER_KV + g
            o_ref[:, h * HEAD_DIM:(h + 1) * HEAD_DIM] = o4[:, g, :].astype(o_ref.dtype)


def attn_sample(qkv, cache_k, cache_v, q_norm, k_norm, sinks, *, col0, bt=8):
    nb = qkv.shape[0]
    width = qkv.shape[1]
    win = pl.BlockSpec((bt, WINDOW, KV_WIDTH), lambda i: (i, 0, 0))
    return pl.pallas_call(
        functools.partial(_attn_sample_kernel, col0=col0),
        grid=(nb // bt,),
        in_specs=[pl.BlockSpec((bt, width), lambda i: (i, 0)), win, win,
                  pl.BlockSpec((1, HEAD_DIM), lambda i: (0, 0)),
                  pl.BlockSpec((1, HEAD_DIM), lambda i: (0, 0)),
                  pl.BlockSpec((N_HEADS, 1), lambda i: (0, 0))],
        out_specs=[pl.BlockSpec((bt, ATTN_WIDTH), lambda i: (i, 0)), win, win],
        out_shape=[jax.ShapeDtypeStruct((nb, ATTN_WIDTH), BF16),
                   jax.ShapeDtypeStruct((nb, WINDOW, KV_WIDTH), F32),
                   jax.ShapeDtypeStruct((nb, WINDOW, KV_WIDTH), F32)],
        compiler_params=_params("parallel"),
        name="attn_sample",
    )(qkv, cache_k, cache_v, q_norm.reshape(1, HEAD_DIM), k_norm.reshape(1, HEAD_DIM),
      sinks.reshape(N_HEADS, 1))


S5_KCH = S5_WIDTH // LANES
S5_CHUNK_STATES = S5_LANES // S5_KCH
S5_SCAN_TILES = 4


def _s5_kernel(u_ref, h0re_ref, h0im_ref, are_ref, aim_ref, wbu_ref, wc_ref, d_ref, wglu_ref,
               y_ref, hre_ref, him_ref, hs_ref, st_ref, *, rows, steps):
    rt = rows * steps
    u = u_ref[...].reshape(rt, S5_WIDTH)

    @pl.when(pl.program_id(0) == 0)
    def _():
        st_ref[:, :S5_LANES] = h0re_ref[...]
        st_ref[:, S5_LANES:] = h0im_ref[...]

    if steps > 1:
        assert rows == SUBLANES
        r = lax.broadcasted_iota(jnp.int32, (rt, rt), 0)
        c = lax.broadcasted_iota(jnp.int32, (rt, rt), 1)
        to_time_major = jnp.where(c == (r % rows) * steps + r // rows, 1.0, 0.0).astype(BF16)
        to_seq_major = jnp.where(c == (r % steps) * rows + r // steps, 1.0, 0.0).astype(BF16)
        hi, mid, lo = _split3(u)
        u_hi = _dot(to_time_major, hi)
        u = u_hi + (_dot(to_time_major, mid) + _dot(to_time_major, lo))
        ub = u_hi.astype(BF16)
    else:
        ub = u.astype(BF16)
    nre = S5_LANES // LANES
    tpk = S5_CHUNK_STATES // LANES
    for k in range(S5_KCH):
        r = _dot(ub[:, k * LANES:(k + 1) * LANES], wbu_ref[k])
        for a in range(tpk):
            hs_ref[k * tpk + a] = r[:, a * LANES:(a + 1) * LANES]
            hs_ref[nre + k * tpk + a] = r[:, (tpk + a) * LANES:(tpk + a + 1) * LANES]

    nt = S5_SCAN_TILES
    for j in range(nre // nt):
        tiles = range(j * nt, (j + 1) * nt)
        ar = [jnp.broadcast_to(are_ref[:, a * LANES:(a + 1) * LANES], (SUBLANES, LANES)) for a in tiles]
        ai = [jnp.broadcast_to(aim_ref[:, a * LANES:(a + 1) * LANES], (SUBLANES, LANES)) for a in tiles]

        def advance(r8, hr, hi):
            nr, ni = [], []
            for n, a in enumerate(tiles):
                nr.append(ar[n] * hr[n] - ai[n] * hi[n] + hs_ref[a, r8, :])
                ni.append(ar[n] * hi[n] + ai[n] * hr[n] + hs_ref[nre + a, r8, :])
                hs_ref[a, r8, :] = nr[n]
                hs_ref[nre + a, r8, :] = ni[n]
            return nr, ni

        if steps == 1:
            def group(rg, carry):
                r8 = pl.ds(pl.multiple_of(rg * SUBLANES, SUBLANES), SUBLANES)
                hr = [st_ref[r8, a * LANES:(a + 1) * LANES] for a in tiles]
                hi = [st_ref[r8, S5_LANES + a * LANES:S5_LANES + (a + 1) * LANES] for a in tiles]
                nr, ni = advance(r8, hr, hi)
                for n, a in enumerate(tiles):
                    st_ref[r8, a * LANES:(a + 1) * LANES] = nr[n]
                    st_ref[r8, S5_LANES + a * LANES:S5_LANES + (a + 1) * LANES] = ni[n]
                return carry
            lax.fori_loop(0, rows // SUBLANES, group, 0)
        else:
            def step(t, carry):
                nr, ni = advance(pl.ds(pl.multiple_of(t * SUBLANES, SUBLANES), SUBLANES), *carry)
                return tuple(nr), tuple(ni)
            hr0 = tuple(st_ref[:, a * LANES:(a + 1) * LANES] for a in tiles)
            hi0 = tuple(st_ref[:, S5_LANES + a * LANES:S5_LANES + (a + 1) * LANES] for a in tiles)
            hr, hi = lax.fori_loop(0, steps, step, (hr0, hi0))
            for n, a in enumerate(tiles):
                st_ref[:, a * LANES:(a + 1) * LANES] = hr[n]
                st_ref[:, S5_LANES + a * LANES:S5_LANES + (a + 1) * LANES] = hi[n]

    ys = []
    for k in range(S5_KCH):
        hre = jnp.concatenate([hs_ref[k * tpk + a] for a in range(tpk)], axis=1).astype(BF16)
        him = jnp.concatenate([hs_ref[nre + k * tpk + a] for a in range(tpk)], axis=1).astype(BF16)
        yk = _dot(hre, wc_ref[0, k]) + _dot(him, wc_ref[1, k])
        ys.append(yk + d_ref[:, k * LANES:(k + 1) * LANES] * u[:, k * LANES:(k + 1) * LANES])
    y = jax.nn.gelu(jnp.concatenate(ys, axis=1))
    out = (y * jax.nn.sigmoid(_dot(y.astype(BF16), wglu_ref[...]))).astype(y_ref.dtype)
    if steps > 1:
        out = _dot(to_seq_major, out).astype(y_ref.dtype)
    y_ref[...] = out.reshape(y_ref.shape)

    @pl.when(pl.program_id(0) == pl.num_programs(0) - 1)
    def _():
        hre_ref[...] = st_ref[:, :S5_LANES]
        him_ref[...] = st_ref[:, S5_LANES:]


def s5_mix(u, h0_re, h0_im, consts, *, rows, seq, steps):
    a_re, a_im, w_bu, w_c, d_skip, w_glu = consts
    if seq > 1:
        u_spec = pl.BlockSpec((rows, steps, S5_WIDTH), lambda c: (0, c, 0))
    else:
        u_spec = pl.BlockSpec((rows, S5_WIDTH), lambda c: (0, 0))
    full = lambda shape: pl.BlockSpec(shape, lambda c: (0,) * len(shape))
    return pl.pallas_call(
        functools.partial(_s5_kernel, rows=rows, steps=steps),
        grid=(seq // steps,),
        in_specs=[u_spec, full((rows, S5_LANES)), full((rows, S5_LANES)),
                  full((1, S5_LANES)), full((1, S5_LANES)),
                  full(w_bu.shape), full(w_c.shape), full((1, S5_WIDTH)), full(w_glu.shape)],
        out_specs=[u_spec, full((rows, S5_LANES)), full((rows, S5_LANES))],
        out_shape=[jax.ShapeDtypeStruct(u.shape[:-1] + (S5_WIDTH,), BF16),
                   jax.ShapeDtypeStruct((rows, S5_LANES), F32),
                   jax.ShapeDtypeStruct((rows, S5_LANES), F32)],
        scratch_shapes=[pltpu.VMEM((2 * S5_LANES // LANES, rows * steps, LANES), F32),
                        pltpu.VMEM((rows, 2 * S5_LANES), F32)],
        compiler_params=_params("arbitrary"),
        name="s5_mix",
    )(u, h0_re, h0_im, a_re, a_im, w_bu, w_c, d_skip, w_glu)


def s5_constants(a_re, a_im, log_dt, b_re, b_im, c_re, c_im, d_skip, w_glu):
    lr, li = a_re, a_im
    dt = jnp.exp(log_dt)[:, None]
    mag = jnp.exp(lr * dt)
    ab_re, ab_im = mag * jnp.cos(li * dt), mag * jnp.sin(li * dt)
    den = lr * lr + li * li
    f_re = ((ab_re - 1.0) * lr + ab_im * li) / den
    f_im = (ab_im * lr - (ab_re - 1.0) * li) / den
    bb_re = f_re[..., None] * b_re - f_im[..., None] * b_im
    bb_im = f_re[..., None] * b_im + f_im[..., None] * b_re
    gpc = LANES // S5_GROUP_CH
    eye = jnp.eye(gpc, dtype=F32)

    def bu_blocks(bb):
        t = bb.reshape(S5_KCH, gpc, S5_STATE, S5_GROUP_CH)
        return jnp.einsum('kgpc,gh->kgchp', t, eye).reshape(S5_KCH, LANES, gpc * S5_STATE)

    def c_blocks(c):
        t = c.reshape(S5_KCH, gpc, S5_GROUP_CH, S5_STATE)
        return jnp.einsum('kgcp,gh->kgphc', t, eye).reshape(S5_KCH, gpc * S5_STATE, LANES)

    w_bu = jnp.concatenate([bu_blocks(bb_re), bu_blocks(bb_im)], axis=-1).astype(BF16)
    w_c = jnp.stack([c_blocks(c_re), -c_blocks(c_im)]).astype(BF16)
    return (ab_re.reshape(1, S5_LANES), ab_im.reshape(1, S5_LANES), w_bu, w_c,
            d_skip.reshape(1, S5_WIDTH), w_glu.astype(BF16))


def _ffn_kernel(x_ref, g_ref, wg_ref, wu_ref, wd_ref, o_ref, xn_ref):
    @pl.when(pl.program_id(1) == 0)
    def _():
        x = x_ref[...]
        xn_ref[...] = _rms(x, g_ref[...]).astype(BF16)
        o_ref[...] = x

    xn = xn_ref[...]
    h = jax.nn.silu(_dot(xn, wg_ref[...])) * _dot(xn, wu_ref[...])
    o_ref[...] += _dot(h.astype(BF16), wd_ref[...])


def ffn(x, g, w_gate, w_up, w_down, *, tm, tf):
    m, d = x.shape
    f = w_gate.shape[1]
    return pl.pallas_call(
        _ffn_kernel,
        grid=(m // tm, f // tf),
        in_specs=[pl.BlockSpec((tm, d), lambda i, j: (i, 0), pipeline_mode=pl.Buffered(1)),
                  pl.BlockSpec((1, d), lambda i, j: (0, 0)),
                  pl.BlockSpec((d, tf), lambda i, j: (0, j)),
                  pl.BlockSpec((d, tf), lambda i, j: (0, j)),
                  pl.BlockSpec((tf, d), lambda i, j: (j, 0))],
        out_specs=pl.BlockSpec((tm, d), lambda i, j: (i, 0)),
        out_shape=jax.ShapeDtypeStruct((m, d), F32),
        scratch_shapes=[pltpu.VMEM((tm, d), BF16)],
        compiler_params=_params("parallel", "arbitrary"),
        name="ffn",
    )(x, g.reshape(1, d), w_gate, w_up, w_down)


MOE_TILE = 512
DMA_UNROLL = 8
META_I1, META_I2, META_W1, META_W2, META_R1, META_R2 = range(6)


def _router_kernel(x_ref, g_ref, r_ref, xn_ref, meta_ref, cnt_ref, carry_ref):
    @pl.when(pl.program_id(0) == 0)
    def _():
        carry_ref[...] = jnp.zeros_like(carry_ref)

    xn = _rms(x_ref[...], g_ref[...])
    xn_ref[...] = xn
    x_hi, x_mid, _ = _split3(xn)
    r_hi, r_mid, _ = _split3(r_ref[...])
    logits = _dot(x_hi, r_hi) + (_dot(x_mid, r_hi) + _dot(x_hi, r_mid))
    lane = lax.broadcasted_iota(jnp.int32, logits.shape, 1)
    neg = -jnp.inf
    l1 = jnp.where(lane < N_EXPERTS, logits, neg)
    m1 = jnp.max(l1, axis=-1, keepdims=True)
    i1 = jnp.min(jnp.where(l1 == m1, lane, LANES), axis=-1, keepdims=True)
    l2 = jnp.where(lane == i1, neg, l1)
    m2 = jnp.max(l2, axis=-1, keepdims=True)
    i2 = jnp.min(jnp.where(l2 == m2, lane, LANES), axis=-1, keepdims=True)
    e = jnp.exp(m2 - m1)
    den = 1.0 + e
    sel = jnp.where(lane == i1, 1.0, jnp.where(lane == i2, 1.0, 0.0))
    tm = sel.shape[0]
    row = lax.broadcasted_iota(jnp.int32, (tm, tm), 0)
    col = lax.broadcasted_iota(jnp.int32, (tm, tm), 1)
    incl = _dot(jnp.where(col <= row, 1.0, 0.0).astype(BF16), sel.astype(BF16))
    excl = incl - sel + carry_ref[0:1, :]
    rank1 = jnp.sum(jnp.where(lane == i1, excl, 0.0), axis=-1, keepdims=True)
    rank2 = jnp.sum(jnp.where(lane == i2, excl, 0.0), axis=-1, keepdims=True)
    carry_ref[...] = carry_ref[...] + incl[tm - 1:tm, :]
    cnt_ref[...] = carry_ref[...]
    fields = {META_I1: i1.astype(F32), META_I2: i2.astype(F32), META_W1: 1.0 / den, META_W2: e / den,
              META_R1: rank1, META_R2: rank2}
    meta = jnp.zeros(logits.shape, F32)
    for k, val in fields.items():
        meta = jnp.where(lane == k, val, meta)
    meta_ref[...] = meta


def router(x, g, r, *, tm):
    m, d = x.shape
    r_pad = jnp.zeros((d, LANES), F32).at[:, :N_EXPERTS].set(r)
    return pl.pallas_call(
        _router_kernel,
        grid=(m // tm,),
        in_specs=[pl.BlockSpec((tm, d), lambda i: (i, 0)),
                  pl.BlockSpec((1, d), lambda i: (0, 0)),
                  pl.BlockSpec((d, LANES), lambda i: (0, 0))],
        out_specs=[pl.BlockSpec((tm, d), lambda i: (i, 0)),
                   pl.BlockSpec((tm, LANES), lambda i: (i, 0)),
                   pl.BlockSpec((SUBLANES, LANES), lambda i: (0, 0))],
        out_shape=[jax.ShapeDtypeStruct((m, d), F32),
                   jax.ShapeDtypeStruct((m, LANES), F32),
                   jax.ShapeDtypeStruct((SUBLANES, LANES), F32)],
        scratch_shapes=[pltpu.VMEM((SUBLANES, LANES), F32)],
        compiler_params=_params("arbitrary"),
        name="router",
    )(x, g.reshape(1, d), r_pad)


def _dispatch_kernel(p1_ref, p2_ref, xn_ref, zeros_ref, xs_ref, sem, *, td):
    del zeros_ref
    base = pl.program_id(0) * td

    def copies(r):
        src = xn_ref.at[pl.ds(r, 1), :]
        return [pltpu.make_async_copy(src, xs_ref.at[pl.ds(p_ref[base + r], 1), :], sem)
                for p_ref in (p1_ref, p2_ref)]

    def issue(r, carry):
        for c in copies(r):
            c.start()
        return carry

    def drain(r, carry):
        for c in copies(r):
            c.wait()
        return carry

    lax.fori_loop(0, td, issue, 0, unroll=DMA_UNROLL)
    lax.fori_loop(0, td, drain, 0, unroll=DMA_UNROLL)


def moe_dispatch(xn, pos1, pos2, xs, *, td):
    m, d = xn.shape
    rows = xs.shape[0]
    return pl.pallas_call(
        functools.partial(_dispatch_kernel, td=td),
        grid_spec=pltpu.PrefetchScalarGridSpec(
            num_scalar_prefetch=2,
            grid=(m // td,),
            in_specs=[pl.BlockSpec((td, d), lambda i, p1, p2: (i, 0)),
                      pl.BlockSpec(memory_space=pl.ANY)],
            out_specs=pl.BlockSpec(memory_space=pl.ANY),
            scratch_shapes=[pltpu.SemaphoreType.DMA(())]),
        out_shape=jax.ShapeDtypeStruct((rows, d), F32),
        input_output_aliases={3: 0},
        compiler_params=_params("arbitrary"),
        name="moe_dispatch",
    )(pos1, pos2, xn, xs)


def _expert_kernel(te_ref, tv_ref, xs_ref, wg_ref, wu_ref, wd_ref, ys_ref, xb_ref):
    del te_ref
    f = pl.program_id(1)
    valid = tv_ref[pl.program_id(0)]

    @pl.when(valid > 0)
    def _():
        @pl.when(f == 0)
        def _():
            xb_ref[...] = xs_ref[...].astype(BF16)

        xb = xb_ref[...]
        h = jax.nn.silu(_dot(xb, wg_ref[0])) * _dot(xb, wu_ref[0])
        y = _dot(h.astype(BF16), wd_ref[0])

        @pl.when(f == 0)
        def _():
            ys_ref[...] = y

        @pl.when(f > 0)
        def _():
            ys_ref[...] += y

    @pl.when((valid == 0) & (f == 0))
    def _():
        ys_ref[...] = jnp.zeros_like(ys_ref)


def moe_experts(xs, tile_expert, tile_valid, w_gate, w_up, w_down, *, tile, tf):
    rows, d = xs.shape
    _, _, f = w_gate.shape
    nf = f // tf
    fidx = lambda t, j, tv: jnp.where(tv[t] > 0, j, nf - 1)
    return pl.pallas_call(
        _expert_kernel,
        grid_spec=pltpu.PrefetchScalarGridSpec(
            num_scalar_prefetch=2,
            grid=(rows // tile, nf),
            in_specs=[pl.BlockSpec((tile, d), lambda t, j, te, tv: (t, 0)),
                      pl.BlockSpec((1, d, tf), lambda t, j, te, tv: (te[t], 0, fidx(t, j, tv))),
                      pl.BlockSpec((1, d, tf), lambda t, j, te, tv: (te[t], 0, fidx(t, j, tv))),
                      pl.BlockSpec((1, tf, d), lambda t, j, te, tv: (te[t], fidx(t, j, tv), 0))],
            out_specs=pl.BlockSpec((tile, d), lambda t, j, te, tv: (t, 0)),
            scratch_shapes=[pltpu.VMEM((tile, d), BF16)]),
        out_shape=jax.ShapeDtypeStruct((rows, d), F32),
        compiler_params=_params("arbitrary", "arbitrary"),
        name="moe_experts",
    )(tile_expert, tile_valid, xs, w_gate, w_up, w_down)


def _combine_kernel(p1_ref, p2_ref, x_ref, meta_ref, ys_ref, o_ref, buf_ref, sem, *, tc):
    i = pl.program_id(0)
    slot = i % 2

    def copies(step, sl, r):
        tok = step * tc + r
        return [pltpu.make_async_copy(ys_ref.at[pl.ds(p_ref[tok], 1), :],
                                      buf_ref.at[sl, k, pl.ds(r, 1), :], sem.at[sl])
                for k, p_ref in enumerate((p1_ref, p2_ref))]

    def issue(step, sl):
        def body(r, carry):
            for c in copies(step, sl, r):
                c.start()
            return carry
        lax.fori_loop(0, tc, body, 0, unroll=DMA_UNROLL)

    @pl.when(i == 0)
    def _():
        issue(0, 0)

    @pl.when(i + 1 < pl.num_programs(0))
    def _():
        issue(i + 1, 1 - slot)

    def drain(r, carry):
        for c in copies(i, slot, r):
            c.wait()
        return carry
    lax.fori_loop(0, tc, drain, 0, unroll=DMA_UNROLL)

    meta = meta_ref[...]
    w1 = meta[:, META_W1:META_W1 + 1]
    w2 = meta[:, META_W2:META_W2 + 1]
    o_ref[...] = x_ref[...] + w1 * buf_ref[slot, 0] + w2 * buf_ref[slot, 1]


def moe_combine(x, meta, ys, pos1, pos2, *, tc):
    m, d = x.shape
    return pl.pallas_call(
        functools.partial(_combine_kernel, tc=tc),
        grid_spec=pltpu.PrefetchScalarGridSpec(
            num_scalar_prefetch=2,
            grid=(m // tc,),
            in_specs=[pl.BlockSpec((tc, d), lambda i, p1, p2: (i, 0)),
                      pl.BlockSpec((tc, LANES), lambda i, p1, p2: (i, 0)),
                      pl.BlockSpec(memory_space=pl.ANY)],
            out_specs=pl.BlockSpec((tc, d), lambda i, p1, p2: (i, 0)),
            scratch_shapes=[pltpu.VMEM((2, 2, tc, d), F32), pltpu.SemaphoreType.DMA((2,))]),
        out_shape=jax.ShapeDtypeStruct((m, d), F32),
        compiler_params=_params("arbitrary"),
        name="moe_combine",
    )(pos1, pos2, x, meta, ys)


def moe_sparse(xs_in, g, r, w_gate, w_up, w_down):
    d = xs_in[0].shape[1]
    m_total = sum(x.shape[0] for x in xs_in)
    tile = min(m_total, MOE_TILE)
    routed = [router(x, g, r, tm=min(x.shape[0], 512)) for x in xs_in]
    cnts = [counts[0, :N_EXPERTS].astype(jnp.int32) for _, _, counts in routed]
    cnt = sum(cnts)
    tiles_e = (cnt + tile - 1) // tile
    tile_end = jnp.cumsum(tiles_e)
    tile_start = tile_end - tiles_e
    n_tiles = (2 * m_total) // tile + N_EXPERTS
    t_all = jnp.arange(n_tiles, dtype=jnp.int32)
    t = jnp.minimum(t_all, tile_end[-1] - 1)
    tile_expert = jnp.sum(t[:, None] >= tile_end[None, :], axis=1).astype(jnp.int32)
    tile_valid = jnp.clip(cnt[tile_expert] - (t - tile_start[tile_expert]) * tile, 0, tile)
    tile_valid = jnp.where(t_all < tile_end[-1], tile_valid, 0).astype(jnp.int32)
    buf = jnp.zeros((n_tiles * tile, d), F32)
    base = tile_start * tile
    pos = []
    for (xn, meta, _), c in zip(routed, cnts):
        p1 = base[meta[:, META_I1].astype(jnp.int32)] + meta[:, META_R1].astype(jnp.int32)
        p2 = base[meta[:, META_I2].astype(jnp.int32)] + meta[:, META_R2].astype(jnp.int32)
        pos.append((p1, p2))
        buf = moe_dispatch(xn, p1, p2, buf, td=min(xn.shape[0], 256))
        base = base + c
    ys = moe_experts(buf, tile_expert, tile_valid, w_gate, w_up, w_down, tile=tile, tf=512)
    return [moe_combine(x, meta, ys, p1, p2, tc=min(x.shape[0], 256))
            for x, (_, meta, _), (p1, p2) in zip(xs_in, routed, pos)]


def _gates_kernel(ba_ref, alog_ref, dtb_ref, beta_ref, eg_ref, gcum_ref, *, tg):
    ba = ba_ref[...]
    b = ba[:, :DN_V_HEADS]
    a = ba[:, DN_V_HEADS:2 * DN_V_HEADS]
    beta_ref[...] = jax.nn.sigmoid(b)
    z = a + dtb_ref[...]
    softplus = jnp.maximum(z, 0.0) + jnp.log1p(jnp.exp(-jnp.abs(z)))
    g = -jnp.exp(alog_ref[...]) * softplus
    eg_ref[...] = jnp.exp(g)
    r = lax.broadcasted_iota(jnp.int32, (tg, tg), 0)
    c = lax.broadcasted_iota(jnp.int32, (tg, tg), 1)
    tri = jnp.where((c <= r) & (r // DN_CHUNK == c // DN_CHUNK), 1.0, 0.0).astype(BF16)
    hi, mid, lo = _split3(g)
    gcum_ref[...] = _dot(tri, hi) + (_dot(tri, mid) + _dot(tri, lo))


def dn_gates(ba, a_log, dt_bias, *, tg):
    m = ba.shape[0]
    out = jax.ShapeDtypeStruct((m, DN_V_HEADS), F32)
    spec = pl.BlockSpec((tg, DN_V_HEADS), lambda i: (i, 0))
    return pl.pallas_call(
        functools.partial(_gates_kernel, tg=tg),
        grid=(m // tg,),
        in_specs=[pl.BlockSpec((tg, LANES), lambda i: (i, 0)),
                  pl.BlockSpec((1, DN_V_HEADS), lambda i: (0, 0)),
                  pl.BlockSpec((1, DN_V_HEADS), lambda i: (0, 0))],
        out_specs=[spec, spec, spec],
        out_shape=[out, out, out],
        compiler_params=_params("parallel"),
        name="dn_gates",
    )(ba, a_log.reshape(1, DN_V_HEADS), dt_bias.reshape(1, DN_V_HEADS))


def _l2norm_heads(c, scale):
    parts = []
    for h in range(c.shape[1] // DN_K_DIM):
        t = c[:, h * DN_K_DIM:(h + 1) * DN_K_DIM]
        t = t * lax.rsqrt(jnp.sum(t * t, axis=-1, keepdims=True) + EPS)
        parts.append(t * scale if scale != 1.0 else t)
    return jnp.concatenate(parts, axis=1)


DN_GROUP = 8
DN_LBLOCK = 512
DN_BATCH = 8
DN_HBATCH = 4
DN_CONV_ROWS = 128


def _dn_chunk_kernel(q_ref, k_ref, v_ref, qh_ref, kh_ref, vh_ref, wq_ref, wk_ref, wv_ref,
                     z_ref, gc_ref, beta_ref, gt_ref, onorm_ref,
                     o_ref, s_out_ref, s_ref, xs_ref, add_ref, oacc_ref, egl_ref, qc_ref, kc_ref, vc_ref,
                     xq_ref, xk_ref, xv_ref, *, nc):
    hg = pl.program_id(1)
    lb = pl.program_id(2)
    cz = DN_CHUNK

    def conv(x_ref, halo_ref, w_ref, xcat_ref, out_ref, post):
        n = x_ref.shape[1]
        xcat_ref[0:SUBLANES] = jnp.where(lb > 0, halo_ref[0], 0.0)
        xcat_ref[SUBLANES:SUBLANES + n] = x_ref[0]
        first = SUBLANES - (DN_CONV_K - 1)
        for r0 in range(0, n, DN_CONV_ROWS):
            acc = xcat_ref[first + r0:first + r0 + DN_CONV_ROWS] * w_ref[0:1, :]
            for i in range(1, DN_CONV_K):
                acc = acc + xcat_ref[first + r0 + i:first + r0 + i + DN_CONV_ROWS] * w_ref[i:i + 1, :]
            out_ref[r0:r0 + DN_CONV_ROWS] = post(jax.nn.silu(acc))

    conv(q_ref, qh_ref, wq_ref, xq_ref, qc_ref, lambda c: _l2norm_heads(c, DN_K_DIM ** -0.5))
    conv(k_ref, kh_ref, wk_ref, xk_ref, kc_ref, lambda c: _l2norm_heads(c, 1.0))
    conv(v_ref, vh_ref, wv_ref, xv_ref, vc_ref, lambda c: c)

    @pl.when(lb == 0)
    def _():
        s_ref[...] = jnp.zeros_like(s_ref)

    gc_all = gc_ref[0]
    beta_all = beta_ref[0]
    lane = lax.broadcasted_iota(jnp.int32, gc_all.shape, 1)
    ri = lax.broadcasted_iota(jnp.int32, (cz, cz), 0)
    ci = lax.broadcasted_iota(jnp.int32, (cz, cz), 1)
    incl = (ri >= ci)[None]
    strict = (ri > ci)[None]

    rep = DN_V_HEADS // DN_K_HEADS
    for g0 in range(0, DN_GROUP, DN_HBATCH):
        gs = list(range(g0, g0 + DN_HBATCH))
        gcol_all, bcol_all, grow_all = {}, {}, {}
        for g in gs:
            head = hg * DN_GROUP + g
            gcol_all[g] = jnp.sum(jnp.where(lane == head, gc_all, 0.0), axis=-1, keepdims=True)
            bcol_all[g] = jnp.sum(jnp.where(lane == head, beta_all, 0.0), axis=-1, keepdims=True)
            grow_all[g] = gt_ref[0, pl.ds(head, 1), :]
        for c0 in range(0, nc, DN_BATCH):
            cb = min(DN_BATCH, nc - c0)
            rows = slice(c0 * cz, (c0 + cb) * cz)
            stack = lambda fn: jnp.concatenate([fn(g) for g in gs], axis=0)
            kcols = lambda g: slice(g // rep * DN_K_DIM, (g // rep + 1) * DN_K_DIM)
            q = stack(lambda g: qc_ref[rows, kcols(g)].reshape(cb, cz, DN_K_DIM))
            k = stack(lambda g: kc_ref[rows, kcols(g)].reshape(cb, cz, DN_K_DIM))
            v = stack(lambda g: vc_ref[rows, g * DN_V_DIM:(g + 1) * DN_V_DIM].reshape(cb, cz, DN_V_DIM))
            gcol = stack(lambda g: gcol_all[g][rows].reshape(cb, cz, 1))
            bcol = stack(lambda g: bcol_all[g][rows].reshape(cb, cz, 1))
            grow = stack(lambda g: jnp.stack([grow_all[g][:, c * cz:(c + 1) * cz]
                                              for c in range(c0, c0 + cb)]))
            decay = jnp.where(incl, jnp.exp(gcol - grow), 0.0)
            kb = k * bcol
            kbf = k.astype(BF16)
            kk = jnp.einsum('cid,cjd->cij', kb.astype(BF16), kbf, preferred_element_type=F32)
            neg_l = jnp.where(strict, -(kk * decay), 0.0)
            n_acc = neg_l
            pw = neg_l
            for _ in range(5):
                pwb = pw.astype(BF16)
                pw = jnp.einsum('cij,cjk->cik', pwb, pwb, preferred_element_type=F32)
                n_acc = n_acc + pw + jnp.einsum('cij,cjk->cik', n_acc.astype(BF16), pw.astype(BF16),
                                                preferred_element_type=F32)
            egc = jnp.exp(gcol)
            rhs = jnp.concatenate([kb * egc, v * bcol], axis=-1)
            bmm = lambda x, y: jnp.einsum('cij,cjd->cid', x, y, preferred_element_type=F32)
            wu = (rhs + bmm(n_acc.astype(BF16), rhs.astype(BF16))).astype(BF16)
            qk = jnp.einsum('cid,cjd->cij', q.astype(BF16), kbf, preferred_element_type=F32) * decay
            a_wu = bmm(qk.astype(BF16), wu)
            glast = gcol[:, cz - 1:cz, :]
            kd = k * jnp.exp(glast - gcol)
            qa = (q * egc - a_wu[..., :DN_K_DIM]).astype(BF16)
            egl = jnp.broadcast_to(jnp.exp(glast), (len(gs) * cb, 1, DN_V_DIM))
            for n, g in enumerate(gs):
                for c in range(cb):
                    kd_wu = _dot(jnp.transpose(kd[n * cb + c]).astype(BF16), wu[n * cb + c])
                    xs_ref[g, c0 + c, 0:DN_K_DIM, :] = (-kd_wu[:, :DN_K_DIM]).astype(BF16)
                    add_ref[g, c0 + c, 0:DN_K_DIM, :] = kd_wu[:, DN_K_DIM:]
                xs_ref[g, c0:c0 + cb, DN_K_DIM:DN_K_DIM + cz, :] = qa[n * cb:(n + 1) * cb]
                add_ref[g, c0:c0 + cb, DN_K_DIM:DN_K_DIM + cz, :] = a_wu[n * cb:(n + 1) * cb, :, DN_K_DIM:]
                egl_ref[g, c0:c0 + cb] = egl[n * cb:(n + 1) * cb]

    def chunk(c, carry):
        for g in range(DN_GROUP):
            s = s_ref[g]
            r = _dot(xs_ref[g, c], s.astype(BF16)) + add_ref[g, c]
            oacc_ref[g, pl.ds(pl.multiple_of(c * cz, cz), cz), :] = r[DN_K_DIM:DN_K_DIM + cz]
            s_ref[g] = s * egl_ref[g, c] + r[0:DN_K_DIM]
        return carry

    lax.fori_loop(0, nc, chunk, 0)

    for g in range(DN_GROUP):
        o = oacc_ref[g]
        z = z_ref[0, :, g * DN_V_DIM:(g + 1) * DN_V_DIM]
        o_ref[0, :, g * DN_V_DIM:(g + 1) * DN_V_DIM] = (
            _rms(o, onorm_ref[...]) * jax.nn.silu(z)).astype(o_ref.dtype)

    @pl.when(lb == pl.num_programs(2) - 1)
    def _():
        s_out_ref[0] = s_ref[...]


def dn_chunked(proj, conv_w, gcum, beta, gcum_t, out_norm):
    b, l, _ = proj.shape
    lbk = min(DN_LBLOCK, l)
    nc = lbk // DN_CHUNK
    gk = DN_GROUP // 2 * DN_K_DIM
    gv = DN_GROUP * DN_V_DIM
    cz = DN_CHUNK
    qcol = lambda h: h
    kcol = lambda h: DN_QK_W // gk + h
    vcol = lambda h: 2 * DN_QK_W // gv + h
    hb = lbk // SUBLANES
    cur = lambda w, col: pl.BlockSpec((1, lbk, w), lambda i, h, t: (i, t, col(h)))
    halo = lambda w, col: pl.BlockSpec((1, SUBLANES, w), lambda i, h, t: (i, jnp.maximum(t * hb - 1, 0), col(h)))
    taps = lambda w, col: pl.BlockSpec((DN_CONV_K, w), lambda i, h, t: (0, col(h)))
    return pl.pallas_call(
        functools.partial(_dn_chunk_kernel, nc=nc),
        grid=(b, DN_V_HEADS // DN_GROUP, l // lbk),
        in_specs=[cur(gk, qcol), cur(gk, kcol), cur(gv, vcol),
                  halo(gk, qcol), halo(gk, kcol), halo(gv, vcol),
                  taps(gk, qcol), taps(gk, kcol), taps(gv, vcol),
                  pl.BlockSpec((1, lbk, gv), lambda i, h, t: (i, t, DN_CONV_DIM // gv + h)),
                  pl.BlockSpec((1, lbk, DN_V_HEADS), lambda i, h, t: (i, t, 0)),
                  pl.BlockSpec((1, lbk, DN_V_HEADS), lambda i, h, t: (i, t, 0)),
                  pl.BlockSpec((1, DN_V_HEADS, lbk), lambda i, h, t: (i, 0, t)),
                  pl.BlockSpec((1, DN_V_DIM), lambda i, h, t: (0, 0))],
        out_specs=[pl.BlockSpec((1, lbk, gv), lambda i, h, t: (i, t, h)),
                   pl.BlockSpec((1, DN_GROUP, DN_K_DIM, DN_V_DIM), lambda i, h, t: (i, h, 0, 0))],
        out_shape=[jax.ShapeDtypeStruct((b, l, DN_V_W), BF16),
                   jax.ShapeDtypeStruct((b, DN_V_HEADS, DN_K_DIM, DN_V_DIM), F32)],
        scratch_shapes=[pltpu.VMEM((DN_GROUP, DN_K_DIM, DN_V_DIM), F32),
                        pltpu.VMEM((DN_GROUP, nc, DN_K_DIM + cz, DN_K_DIM), BF16),
                        pltpu.VMEM((DN_GROUP, nc, DN_K_DIM + cz, DN_V_DIM), F32),
                        pltpu.VMEM((DN_GROUP, lbk, DN_V_DIM), F32),
                        pltpu.VMEM((DN_GROUP, nc, 1, DN_V_DIM), F32),
                        pltpu.VMEM((lbk, gk), F32), pltpu.VMEM((lbk, gk), F32), pltpu.VMEM((lbk, gv), F32),
                        pltpu.VMEM((lbk + SUBLANES, gk), F32), pltpu.VMEM((lbk + SUBLANES, gk), F32),
                        pltpu.VMEM((lbk + SUBLANES, gv), F32)],
        compiler_params=_params("parallel", "parallel", "arbitrary"),
        name="dn_chunked",
    )(proj, proj, proj, proj, proj, proj, conv_w, conv_w, conv_w, proj, gcum, beta, gcum_t,
      out_norm.reshape(1, DN_V_DIM))


DN_ROWS = DN_CONV_DIM // LANES
DN_QROWS = DN_K_HEADS
DN_VROW0 = 2 * DN_K_HEADS
DN_SAMPLE_SEQS = 1


def _dn_sample_kernel(x_ref, buf_ref, w_ref, z_ref, eg_ref, beta_ref, s_ref, onorm_ref,
                      o_ref, s_out_ref, buf_out_ref, *, bt):
    for i in range(bt):
        b = pl.program_id(0) * bt + i
        x = x_ref[i]
        buf = buf_ref[i]
        conv = buf[0] * w_ref[0]
        for t in range(1, DN_CONV_K - 1):
            conv = conv + buf[t] * w_ref[t]
        conv = conv + x * w_ref[DN_CONV_K - 1]
        buf_out_ref[i, 0:DN_CONV_K - 2] = buf[1:DN_CONV_K - 1]
        buf_out_ref[i, DN_CONV_K - 2] = x
        c = jax.nn.silu(conv)
        qk = c[0:DN_VROW0]
        qk = qk * lax.rsqrt(jnp.sum(qk * qk, axis=-1, keepdims=True) + EPS)
        q_t = jnp.transpose(qk[0:DN_QROWS] * (DN_K_DIM ** -0.5))
        k_t = jnp.transpose(qk[DN_QROWS:DN_VROW0])
        for h in range(DN_V_HEADS):
            kh = h // (DN_V_HEADS // DN_K_HEADS)
            kcol = k_t[:, kh:kh + 1]
            qcol = q_t[:, kh:kh + 1]
            v = c[DN_VROW0 + h:DN_VROW0 + h + 1]
            s = s_ref[i, h] * eg_ref[b, h]
            delta = (v - jnp.sum(kcol * s, axis=0, keepdims=True)) * beta_ref[b, h]
            s = s + kcol * delta
            s_out_ref[i, h] = s
            o = jnp.sum(qcol * s, axis=0, keepdims=True)
            o_ref[i, h:h + 1, :] = (_rms(o, onorm_ref[...]) * jax.nn.silu(z_ref[i, h:h + 1, :])
                                    ).astype(o_ref.dtype)


def dn_sample(qkv_rows, conv_buf, conv_w, z_rows, eg, beta, state, out_norm, *, bt=DN_SAMPLE_SEQS):
    nb = qkv_rows.shape[0]
    smem = pl.BlockSpec(memory_space=pltpu.SMEM)
    return pl.pallas_call(
        functools.partial(_dn_sample_kernel, bt=bt),
        grid=(nb // bt,),
        in_specs=[pl.BlockSpec((bt, DN_ROWS, LANES), lambda i: (i, 0, 0)),
                  pl.BlockSpec((bt, DN_CONV_K - 1, DN_ROWS, LANES), lambda i: (i, 0, 0, 0)),
                  pl.BlockSpec((DN_CONV_K, DN_ROWS, LANES), lambda i: (0, 0, 0)),
                  pl.BlockSpec((bt, DN_V_HEADS, DN_V_DIM), lambda i: (i, 0, 0)),
                  smem, smem,
                  pl.BlockSpec((bt, DN_V_HEADS, DN_K_DIM, DN_V_DIM), lambda i: (i, 0, 0, 0)),
                  pl.BlockSpec((1, DN_V_DIM), lambda i: (0, 0))],
        out_specs=[pl.BlockSpec((bt, DN_V_HEADS, DN_V_DIM), lambda i: (i, 0, 0)),
                   pl.BlockSpec((bt, DN_V_HEADS, DN_K_DIM, DN_V_DIM), lambda i: (i, 0, 0, 0)),
                   pl.BlockSpec((bt, DN_CONV_K - 1, DN_ROWS, LANES), lambda i: (i, 0, 0, 0))],
        out_shape=[jax.ShapeDtypeStruct((nb, DN_V_HEADS, DN_V_DIM), BF16),
                   jax.ShapeDtypeStruct(state.shape, F32),
                   jax.ShapeDtypeStruct(conv_buf.shape, F32)],
        compiler_params=_params("arbitrary"),
        name="dn_sample",
    )(qkv_rows, conv_buf, conv_w, z_rows, eg, beta, state, out_norm.reshape(1, DN_V_DIM))


def _tile(m, cap):
    return min(m, cap)


def _even_layer(x, cache, w, *, batch, seq):
    m = x.shape[0]
    tm = _tile(m, 1024)
    proj = norm_matmul(x, w['norm_mix'], w['w_in'], tm=tm, tn=1280)
    if cache is None:
        attn, new_k, new_v = attn_prompt(proj, w['q_norm'], w['k_norm'], w['sinks'], batch=batch, seq=seq,
                                         col0=S5_WIDTH)
        zeros = jnp.zeros((batch, S5_LANES), F32)
        ssm, h_re, h_im = s5_mix(proj.reshape(batch, seq, EVEN_IN), zeros, zeros, w['s5'],
                                 rows=batch, seq=seq, steps=64)
        ssm = ssm.reshape(m, S5_WIDTH)
    else:
        k_win, v_win, h0_re, h0_im = cache
        attn, new_k, new_v = attn_sample(proj, k_win.reshape(batch, WINDOW, KV_WIDTH),
                                         v_win.reshape(batch, WINDOW, KV_WIDTH),
                                         w['q_norm'], w['k_norm'], w['sinks'], col0=S5_WIDTH)
        ssm, h_re, h_im = s5_mix(proj, h0_re.reshape(batch, S5_LANES), h0_im.reshape(batch, S5_LANES),
                                 w['s5'], rows=batch, seq=1, steps=1)
    x = matmul_residual(x, [attn, ssm], [w['w_out_a'], w['w_out_b']], tm=tm, tn=1024)
    x = ffn(x, w['norm_ffn'], w['ffn_gate'], w['ffn_up'], w['ffn_down'], tm=tm, tf=512)
    shp = (batch, WINDOW, N_KV_HEADS, HEAD_DIM)
    st = (batch, S5_GROUPS, S5_STATE)
    return x, new_k.reshape(shp), new_v.reshape(shp), h_re.reshape(st), h_im.reshape(st)


def _moe(xs_in, w):
    return moe_sparse(xs_in, w['norm_ffn'], w['router'], w['exp_gate'], w['exp_up'], w['exp_down'])


def _odd_layer(x, cache, w, *, batch, seq):
    x, s_new, new_buf = _odd_mixer(x, cache, w, batch=batch, seq=seq)
    return _moe([x], w)[0], s_new, new_buf


def _odd_mixer(x, cache, w, *, batch, seq):
    m = x.shape[0]
    tm = _tile(m, 1024)
    proj = norm_matmul(x, w['norm_mix'], w['w_in_main'], tm=tm, tn=1536)
    ba = norm_matmul(x, w['norm_mix'], w['w_in_ba'], tm=tm, tn=LANES)
    beta, eg, gcum = dn_gates(ba, w['a_log'], w['dt_bias'], tg=_tile(m, 512))
    if cache is None:
        proj3 = proj.reshape(batch, seq, DN_CONV_DIM + DN_V_W)
        gcum3 = gcum.reshape(batch, seq, DN_V_HEADS)
        o, s_new = dn_chunked(proj3, w['conv_w'], gcum3, beta.reshape(batch, seq, DN_V_HEADS),
                              jnp.swapaxes(gcum3, 1, 2), w['out_norm'])
        o = o.reshape(m, DN_V_W)
        new_buf = proj3[:, seq - (DN_CONV_K - 1):, :DN_CONV_DIM]
    else:
        s0, conv_buf = cache
        o, s_new, new_buf = dn_sample(
            proj[:, :DN_CONV_DIM].reshape(batch, DN_ROWS, LANES),
            conv_buf.reshape(batch, DN_CONV_K - 1, DN_ROWS, LANES),
            w['conv_w'].reshape(DN_CONV_K, DN_ROWS, LANES),
            proj[:, DN_CONV_DIM:].reshape(batch, DN_V_HEADS, DN_V_DIM),
            eg, beta, s0, w['out_norm'])
        o = o.reshape(m, DN_V_W)
        new_buf = new_buf.reshape(batch, DN_CONV_K - 1, DN_CONV_DIM)
    x = matmul_residual(x, [o], [w['w_out']], tm=tm, tn=1024)
    return x, s_new, new_buf


def kernel(x_prompt, x_sample, cache_win_k, cache_win_v, state_s5_re, state_s5_im, state_dn, state_dn_conv,
           e_norm_mix, e_w_in, e_q_norm, e_k_norm, e_sinks,
           e_s5_a_re, e_s5_a_im, e_s5_log_dt, e_s5_b_re, e_s5_b_im, e_s5_c_re, e_s5_c_im, e_s5_d, e_s5_w_glu,
           e_w_out, e_norm_ffn, e_ffn_w_gate, e_ffn_w_up, e_ffn_w_down,
           o_norm_mix, o_w_in, o_conv_w, o_a_log, o_dt_bias, o_out_norm, o_w_out, o_norm_ffn,
           o_router, o_exp_w_gate, o_exp_w_up, o_exp_w_down):
    bp, lp, d = x_prompt.shape
    bs, ls, _ = x_sample.shape
    assert ls == 1, "the sample group advances one token per step"
    hp = x_prompt.reshape(bp * lp, d)
    hs = x_sample.reshape(bs * ls, d)
    qkv_w = ATTN_WIDTH + 2 * KV_WIDTH
    main_w = DN_CONV_DIM + DN_V_W

    j = 0
    we = dict(
        norm_mix=e_norm_mix[j], q_norm=e_q_norm[j], k_norm=e_k_norm[j], sinks=e_sinks[j],
        w_in=jnp.concatenate([e_w_in[j, :, qkv_w:], e_w_in[j, :, :qkv_w]], axis=1).astype(BF16),
        s5=s5_constants(e_s5_a_re[j], e_s5_a_im[j], e_s5_log_dt[j], e_s5_b_re[j], e_s5_b_im[j],
                        e_s5_c_re[j], e_s5_c_im[j], e_s5_d[j], e_s5_w_glu[j]),
        w_out_a=e_w_out[j, :ATTN_WIDTH].astype(BF16), w_out_b=e_w_out[j, ATTN_WIDTH:].astype(BF16),
        norm_ffn=e_norm_ffn[j], ffn_gate=e_ffn_w_gate[j].astype(BF16),
        ffn_up=e_ffn_w_up[j].astype(BF16), ffn_down=e_ffn_w_down[j].astype(BF16))
    ba_pad = jnp.zeros((d, LANES), F32).at[:, :2 * DN_V_HEADS].set(o_w_in[j, :, main_w:])
    wo = dict(
        norm_mix=o_norm_mix[j], w_in_main=o_w_in[j, :, :main_w].astype(BF16), w_in_ba=ba_pad.astype(BF16),
        conv_w=o_conv_w[j], a_log=o_a_log[j], dt_bias=o_dt_bias[j], out_norm=o_out_norm[j],
        w_out=o_w_out[j].astype(BF16), norm_ffn=o_norm_ffn[j], router=o_router[j],
        exp_gate=o_exp_w_gate[j].astype(BF16), exp_up=o_exp_w_up[j].astype(BF16),
        exp_down=o_exp_w_down[j].astype(BF16))

    hp, kp, vp, rp, ip = _even_layer(hp, None, we, batch=bp, seq=lp)
    hs, ks, vs, rs, is_ = _even_layer(
        hs, (cache_win_k[j], cache_win_v[j], state_s5_re[j], state_s5_im[j]), we, batch=bs, seq=1)
    hp, sp, cp = _odd_mixer(hp, None, wo, batch=bp, seq=lp)
    hs, ss, cs = _odd_mixer(hs, (state_dn[j], state_dn_conv[j]), wo, batch=bs, seq=1)
    hp, hs = _moe([hp, hs], wo)

    one = lambda t: t[None]
    return (hp.reshape(bp, lp, d), hs.reshape(bs, ls, d),
            one(kp), one(vp), one(rp), one(ip), one(sp), one(cp),
            one(ks), one(vs), one(rs), one(is_), one(ss), one(cs))
```
